```python
import math
import jax, jax.numpy as jnp
from jax import lax
import numpy as np

D_MODEL = 1024
BATCH = 8
SEQ = 8192
DEPTH = 4

HEAD_DIM = 64
N_DIFF_HEADS = 4
DIFF_V_DIM = 2 * HEAD_DIM
N_FOX_HEADS = 8
D_DIFF_QK = N_DIFF_HEADS * HEAD_DIM
D_DIFF_OUT = N_DIFF_HEADS * DIFF_V_DIM
D_FOX = N_FOX_HEADS * HEAD_DIM
D_MIX = D_DIFF_OUT + D_FOX
D_IN = 4 * D_DIFF_QK + D_DIFF_OUT + 3 * D_FOX + N_FOX_HEADS
SPLIT_POINTS = (D_DIFF_QK, 2 * D_DIFF_QK, 3 * D_DIFF_QK, 4 * D_DIFF_QK,
                4 * D_DIFF_QK + D_DIFF_OUT,
                4 * D_DIFF_QK + D_DIFF_OUT + D_FOX,
                4 * D_DIFF_QK + D_DIFF_OUT + 2 * D_FOX,
                4 * D_DIFF_QK + D_DIFF_OUT + 3 * D_FOX)
ATTN_SCALE = 1.0 / math.sqrt(HEAD_DIM)
ROPE_DIM = HEAD_DIM // 4
ROPE_THETA = 500000.0
Q_BLOCK = 128
N_MEM = 256
MEM_HEADS = 4
MEM_HEAD_DIM = D_MODEL // MEM_HEADS
MEM_SCALE = 1.0 / math.sqrt(MEM_HEAD_DIM)
D_FF = 2816
N_EXPERTS = 8
TOP_K = 2
D_EXPERT = 3584
N_DENSE = (DEPTH + 1) // 2
N_MOE = DEPTH // 2
EPS = 1e-6
SUBLN_EPS = 1e-5

kernel_name = "hybrid_diff_fox_moe_trunk"


def rmsnorm(x, g, eps=EPS):
    xf = x.astype(jnp.float32)
    y = xf * lax.rsqrt(jnp.mean(xf * xf, axis=-1, keepdims=True) + eps)
    return (y * g.astype(jnp.float32)).astype(x.dtype)


def rope_tables(positions, dtype):
    inv_freq = ROPE_THETA ** (-jnp.arange(0, ROPE_DIM, 2, dtype=jnp.float32) / ROPE_DIM)
    ang = positions.astype(jnp.float32)[..., None] * inv_freq
    return (jnp.cos(ang)[:, :, None, :].astype(dtype),
            jnp.sin(ang)[:, :, None, :].astype(dtype))


def apply_partial_rope(t, cos, sin):
    half = ROPE_DIM // 2
    t1, t2, rest = t[..., :half], t[..., half:ROPE_DIM], t[..., ROPE_DIM:]
    return jnp.concatenate([t1 * cos - t2 * sin, t2 * cos + t1 * sin, rest], axis=-1)


def hybrid_mixer(h, cos, sin, w_in, lam_q1, lam_k1, lam_q2, lam_k2, diff_subln_g, fox_bias, w_out, lam_init):
    B, S, _ = h.shape
    proj = h @ w_in
    q1, q2, k1, k2, v_d, q_f, k_f, v_f, f_logit = jnp.split(proj, SPLIT_POINTS, axis=-1)
    hd = lambda t, n: t.reshape(B, S, n, -1)
    q1 = apply_partial_rope(hd(q1, N_DIFF_HEADS), cos, sin)
    q2 = apply_partial_rope(hd(q2, N_DIFF_HEADS), cos, sin)
    k1 = apply_partial_rope(hd(k1, N_DIFF_HEADS), cos, sin)
    k2 = apply_partial_rope(hd(k2, N_DIFF_HEADS), cos, sin)
    v_d = hd(v_d, N_DIFF_HEADS)
    q_f, k_f, v_f = hd(q_f, N_FOX_HEADS), hd(k_f, N_FOX_HEADS), hd(v_f, N_FOX_HEADS)

    lam = (jnp.exp(jnp.sum(lam_q1.astype(jnp.float32) * lam_k1.astype(jnp.float32)))
           - jnp.exp(jnp.sum(lam_q2.astype(jnp.float32) * lam_k2.astype(jnp.float32)))
           + lam_init)

    log_f = jax.nn.log_sigmoid((f_logit + fox_bias).astype(jnp.float32))
    c = lax.cumsum(log_f, axis=1)
    c_k = jnp.transpose(c, (0, 2, 1))[:, :, None, :]

    nb = S // Q_BLOCK
    key_idx = jnp.arange(S)

    def to_blocks(t):
        return jnp.moveaxis(t.reshape(B, nb, Q_BLOCK, *t.shape[2:]), 1, 0)

    def from_blocks(t):
        return jnp.moveaxis(t, 0, 1).reshape(B, S, *t.shape[3:])

    def block(args):
        qb1, qb2, qbf, cb, start = args
        causal = key_idx[None, :] <= (start + jnp.arange(Q_BLOCK))[:, None]
        s1 = jnp.einsum('bqhd,bkhd->bhqk', qb1, k1).astype(jnp.float32) * ATTN_SCALE
        s2 = jnp.einsum('bqhd,bkhd->bhqk', qb2, k2).astype(jnp.float32) * ATTN_SCALE
        p1 = jax.nn.softmax(jnp.where(causal, s1, -jnp.inf), axis=-1)
        p2 = jax.nn.softmax(jnp.where(causal, s2, -jnp.inf), axis=-1)
        o_d = jnp.einsum('bhqk,bkhe->bqhe', (p1 - lam * p2).astype(v_d.dtype), v_d)
        sf = (jnp.einsum('bqhd,bkhd->bhqk', qbf, k_f).astype(jnp.float32) * ATTN_SCALE
              + jnp.transpose(cb, (0, 2, 1))[..., None] - c_k)
        pf = jax.nn.softmax(jnp.where(causal, sf, -jnp.inf), axis=-1)
        o_f = jnp.einsum('bhqk,bkhd->bqhd', pf.astype(v_f.dtype), v_f)
        return o_d, o_f

    o_d, o_f = lax.map(block, (to_blocks(q1), to_blocks(q2), to_blocks(q_f), to_blocks(c),
                               jnp.arange(nb) * Q_BLOCK))
    o_d = rmsnorm(from_blocks(o_d), diff_subln_g, SUBLN_EPS) * (1.0 - lam_init)
    o_f = from_blocks(o_f)
    mixed = jnp.concatenate([o_d.reshape(B, S, D_DIFF_OUT), o_f.reshape(B, S, D_FOX)], axis=-1)
    return mixed @ w_out


def memory_cross_attention(h, mem_n, w_q, w_kv, w_o):
    B, S, _ = h.shape
    q = (h @ w_q).reshape(B, S, MEM_HEADS, MEM_HEAD_DIM)
    k, v = jnp.split(mem_n @ w_kv, 2, axis=-1)
    k = k.reshape(B, N_MEM, MEM_HEADS, MEM_HEAD_DIM)
    v = v.reshape(B, N_MEM, MEM_HEADS, MEM_HEAD_DIM)
    s = jnp.einsum('bqhd,bmhd->bhqm', q, k).astype(jnp.float32) * MEM_SCALE
    p = jax.nn.softmax(s, axis=-1).astype(v.dtype)
    o = jnp.einsum('bhqm,bmhd->bqhd', p, v).reshape(B, S, D_MODEL)
    return o @ w_o


def dense_swiglu(h, w1, w3, w2):
    return (jax.nn.silu(h @ w1) * (h @ w3)) @ w2


def moe_swiglu(h, w_router, w1, w3, w2):
    B, S, D = h.shape
    xt = h.reshape(B * S, D)
    logits = (xt @ w_router).astype(jnp.float32)
    top_vals, top_idx = lax.top_k(logits, TOP_K)
    gates = jax.nn.softmax(top_vals, axis=-1)
    combine = jnp.sum(jax.nn.one_hot(top_idx, N_EXPERTS, dtype=jnp.float32) * gates[..., None], axis=1)
    combine = combine.astype(h.dtype)
    y = jnp.zeros_like(xt)
    for e in range(N_EXPERTS):
        y = y + combine[:, e:e + 1] * ((jax.nn.silu(xt @ w1[e]) * (xt @ w3[e])) @ w2[e])
    return y.reshape(B, S, D)


def setup_inputs(seed: int = 0) -> dict:
    key = jax.random.key(seed)
    ks = jax.random.split(key, 32)
    f32 = jnp.float32
    nrm = lambda k, shape, fan_in: jax.random.normal(k, shape, f32) * (fan_in ** -0.5)
    gain = lambda k, shape: 1.0 + 0.02 * jax.random.normal(k, shape, f32)
    x = jax.random.normal(ks[0], (BATCH, SEQ, D_MODEL), f32)
    mem = jax.random.normal(ks[1], (BATCH, N_MEM, D_MODEL), f32)
    offsets = jax.random.randint(ks[2], (BATCH, 1), 0, 4096, dtype=jnp.int32)
    positions = offsets + jnp.arange(SEQ, dtype=jnp.int32)[None, :]
    return {
        "x": x,
        "mem": mem,
        "positions": positions,
        "norm_mix_g": gain(ks[3], (DEPTH, D_MODEL)),
        "w_in": nrm(ks[4], (DEPTH, D_MODEL, D_IN), D_MODEL),
        "lam_q1": 0.1 * jax.random.normal(ks[5], (DEPTH, HEAD_DIM), f32),
        "lam_k1": 0.1 * jax.random.normal(ks[6], (DEPTH, HEAD_DIM), f32),
        "lam_q2": 0.1 * jax.random.normal(ks[7], (DEPTH, HEAD_DIM), f32),
        "lam_k2": 0.1 * jax.random.normal(ks[8], (DEPTH, HEAD_DIM), f32),
        "diff_subln_g": gain(ks[9], (DEPTH, DIFF_V_DIM)),
        "fox_bias": 3.0 + 0.1 * jax.random.normal(ks[10], (DEPTH, N_FOX_HEADS), f32),
        "w_out": nrm(ks[11], (DEPTH, D_MIX, D_MODEL), D_MIX),
        "norm_mem_g": gain(ks[12], (DEPTH, D_MODEL)),
        "mem_norm_g": gain(ks[13], (DEPTH, D_MODEL)),
        "w_mq": nrm(ks[14], (DEPTH, D_MODEL, D_MODEL), D_MODEL),
        "w_mkv": nrm(ks[15], (DEPTH, D_MODEL, 2 * D_MODEL), D_MODEL),
        "w_mo": nrm(ks[16], (DEPTH, D_MODEL, D_MODEL), D_MODEL),
        "norm_ffn_g": gain(ks[17], (DEPTH, D_MODEL)),
        "w_ff1": nrm(ks[18], (N_DENSE, D_MODEL, D_FF), D_MODEL),
        "w_ff3": nrm(ks[19], (N_DENSE, D_MODEL, D_FF), D_MODEL),
        "w_ff2": nrm(ks[20], (N_DENSE, D_FF, D_MODEL), D_FF),
        "w_router": nrm(ks[21], (N_MOE, D_MODEL, N_EXPERTS), D_MODEL),
        "w_e1": nrm(ks[22], (N_MOE, N_EXPERTS, D_MODEL, D_EXPERT), D_MODEL),
        "w_e3": nrm(ks[23], (N_MOE, N_EXPERTS, D_MODEL, D_EXPERT), D_MODEL),
        "w_e2": nrm(ks[24], (N_MOE, N_EXPERTS, D_EXPERT, D_MODEL), D_EXPERT),
        "final_norm_g": gain(ks[25], (D_MODEL,)),
    }


def reference(x, mem, positions, norm_mix_g, w_in, lam_q1, lam_k1, lam_q2, lam_k2, diff_subln_g,
              fox_bias, w_out, norm_mem_g, mem_norm_g, w_mq, w_mkv, w_mo, norm_ffn_g,
              w_ff1, w_ff3, w_ff2, w_router, w_e1, w_e3, w_e2, final_norm_g):
    cos, sin = rope_tables(positions, x.dtype)
    h = x
    for l in range(DEPTH):
        lam_init = 0.8 - 0.6 * math.exp(-0.3 * l)
        h = h + hybrid_mixer(rmsnorm(h, norm_mix_g[l]), cos, sin, w_in[l], lam_q1[l], lam_k1[l],
                             lam_q2[l], lam_k2[l], diff_subln_g[l], fox_bias[l], w_out[l], lam_init)
        h = h + memory_cross_attention(rmsnorm(h, norm_mem_g[l]), rmsnorm(mem, mem_norm_g[l]),
                                       w_mq[l], w_mkv[l], w_mo[l])
        hn = rmsnorm(h, norm_ffn_g[l])
        if l % 2 == 0:
            i = l // 2
            h = h + dense_swiglu(hn, w_ff1[i], w_ff3[i], w_ff2[i])
        else:
            i = l // 2
            h = h + moe_swiglu(hn, w_router[i], w_e1[i], w_e3[i], w_e2[i])
    return rmsnorm(h, final_norm_g)
```

```python
import functools
import math

import jax
import jax.numpy as jnp
from jax import lax
from jax.experimental import pallas as pl
from jax.experimental.pallas import tpu as pltpu

F32 = jnp.float32
BF16 = jnp.bfloat16

HEAD_DIM = 64
N_DIFF_HEADS = 4
DIFF_V_DIM = 128
N_FOX_HEADS = 8
D_DIFF_QK = N_DIFF_HEADS * HEAD_DIM
D_DIFF_OUT = N_DIFF_HEADS * DIFF_V_DIM
D_FOX = N_FOX_HEADS * HEAD_DIM
ATTN_SCALE = 1.0 / math.sqrt(HEAD_DIM)
ROPE_DIM = HEAD_DIM // 4
ROPE_HALF = ROPE_DIM // 2
ROPE_THETA = 500000.0
MEM_HEADS = 4
N_EXPERTS = 8
EPS = 1e-6
SUBLN_EPS = 1e-5

LANES = 128
VMEM_LIMIT = 48 * 1024 * 1024

TM_PROJ = 1024
TN_PROJ = 512
TQ = 512
TK = 512
TM_MIX = 512
TM_FFN = 1024
TF_DENSE = 256
TF_MOE = 512
CUM_CHUNK = 256


def _params(sem):
    return pltpu.CompilerParams(dimension_semantics=sem, vmem_limit_bytes=VMEM_LIMIT)


def _rms_scale(x, eps):
    return lax.rsqrt(jnp.mean(x * x, axis=-1, keepdims=True) + eps)


def _proj_kernel(x_ref, g_ref, w_ref, wf_ref, fb_ref, cos_ref, sa_ref, sb_ref,
                 qk_ref, vt_ref, c_ref, xn_ref, carry_ref, *, tiles_per_seq, n_rope, n_qk):
    i = pl.program_id(0)
    j = pl.program_id(1)
    tm = x_ref.shape[0]

    @pl.when(j == 0)
    def _():
        x = x_ref[...]
        xn = (x * _rms_scale(x, EPS)) * g_ref[...]
        xnb = xn.astype(BF16)
        xn_ref[...] = xnb
        logit = jnp.dot(xnb, wf_ref[...], preferred_element_type=F32) + fb_ref[...]
        logf = jnp.minimum(logit, 0.0) - jnp.log1p(jnp.exp(-jnp.abs(logit)))

        @pl.when(i % tiles_per_seq == 0)
        def _():
            carry_ref[...] = jnp.zeros_like(carry_ref)

        r = lax.broadcasted_iota(jnp.int32, (CUM_CHUNK, CUM_CHUNK), 0)
        cidx = lax.broadcasted_iota(jnp.int32, (CUM_CHUNK, CUM_CHUNK), 1)
        tri = (cidx <= r).astype(BF16)
        carry = carry_ref[...]
        for ch in range(tm // CUM_CHUNK):
            v = logf[ch * CUM_CHUNK:(ch + 1) * CUM_CHUNK]
            hi = v.astype(BF16)
            r1 = v - hi.astype(F32)
            mid = r1.astype(BF16)
            lo = (r1 - mid.astype(F32)).astype(BF16)
            cs = (jnp.dot(tri, hi, preferred_element_type=F32)
                  + jnp.dot(tri, mid, preferred_element_type=F32)
                  + jnp.dot(tri, lo, preferred_element_type=F32)) + carry
            c_ref[ch * CUM_CHUNK:(ch + 1) * CUM_CHUNK, :] = cs
            carry = cs[CUM_CHUNK - 1:CUM_CHUNK, :]
        carry_ref[...] = carry

    acc = jnp.dot(xn_ref[...], w_ref[...], preferred_element_type=F32)

    @pl.when(j < n_rope)
    def _():
        cos = cos_ref[...]
        sa = sa_ref[...]
        sb = sb_ref[...]
        for s in range(TN_PROJ // LANES):
            t = acc[:, s * LANES:(s + 1) * LANES]
            rot = (t * cos + pltpu.roll(t, ROPE_HALF, 1) * sa
                   + pltpu.roll(t, LANES - ROPE_HALF, 1) * sb)
            qk_ref[:, s * LANES:(s + 1) * LANES] = rot.astype(qk_ref.dtype)

    @pl.when(jnp.logical_and(j >= n_rope, j < n_qk))
    def _():
        qk_ref[...] = acc.astype(qk_ref.dtype)

    @pl.when(j >= n_qk)
    def _():
        at = acc.T
        for hd in range(TN_PROJ // LANES):
            for kb in range(tm // TK):
                vt_ref[0, hd, kb] = at[hd * LANES:(hd + 1) * LANES,
                                       kb * TK:(kb + 1) * TK].astype(vt_ref.dtype)


def _proj_call(h, g, w_main, w_f, f_bias, cos_t, sa_t, sb_t, batch, seq):
    t, d = h.shape
    tm = TM_PROJ
    tiles_per_seq = seq // tm
    n_col = w_main.shape[1] // TN_PROJ
    n_rope, n_qk = 2, 4
    n_vb = TN_PROJ // LANES
    kern = functools.partial(_proj_kernel, tiles_per_seq=tiles_per_seq, n_rope=n_rope, n_qk=n_qk)
    return pl.pallas_call(
        kern,
        grid=(t // tm, n_col),
        in_specs=[
            pl.BlockSpec((tm, d), lambda i, j: (i, 0)),
            pl.BlockSpec((1, d), lambda i, j: (0, 0)),
            pl.BlockSpec((d, TN_PROJ), lambda i, j: (0, j)),
            pl.BlockSpec((d, LANES), lambda i, j: (0, 0)),
            pl.BlockSpec((1, LANES), lambda i, j: (0, 0)),
            pl.BlockSpec((tm, LANES), lambda i, j: (i, 0)),
            pl.BlockSpec((tm, LANES), lambda i, j: (i, 0)),
            pl.BlockSpec((tm, LANES), lambda i, j: (i, 0)),
        ],
        out_specs=[
            pl.BlockSpec((tm, TN_PROJ), lambda i, j: (i, jnp.minimum(j, n_qk - 1))),
            pl.BlockSpec((1, n_vb, tm // TK, LANES, TK),
                         lambda i, j: (i // tiles_per_seq, jnp.maximum(j - n_qk, 0),
                                       i % tiles_per_seq, 0, 0)),
            pl.BlockSpec((tm, LANES), lambda i, j: (i, 0)),
        ],
        out_shape=[
            jax.ShapeDtypeStruct((t, n_qk * TN_PROJ), BF16),
            jax.ShapeDtypeStruct((batch, (n_col - n_qk) * n_vb, seq // TK, LANES, TK), BF16),
            jax.ShapeDtypeStruct((t, LANES), F32),
        ],
        scratch_shapes=[pltpu.VMEM((tm, d), BF16), pltpu.VMEM((1, LANES), F32)],
        compiler_params=_params(("arbitrary", "arbitrary")),
    )(h, g, w_main, w_f, f_bias, cos_t, sa_t, sb_t)


def _head_mask(q, half):
    lane = lax.broadcasted_iota(jnp.int32, q.shape, 1)
    return jnp.where(lane // HEAD_DIM == half, q, jnp.zeros_like(q))


def _causal_mask():
    kpos = lax.broadcasted_iota(jnp.int32, (TK, TQ), 0)
    qpos = lax.broadcasted_iota(jnp.int32, (TK, TQ), 1)
    return kpos <= qpos


def _scores_t(k, qm):
    return lax.dot_general(k, qm, (((1,), (1,)), ((), ())), preferred_element_type=F32)


def _fox_kernel(q_ref, k_ref, vt_ref, ct_ref, c_ref, o_ref, acc_ref, m_ref, l_ref):
    pair = pl.program_id(1)
    qi = pl.program_id(2)
    q = q_ref[0]
    causal = _causal_mask()
    lane_c = lax.broadcasted_iota(jnp.int32, (TK, LANES), 1)
    qms = [_head_mask(q, hh) for hh in range(2)]
    cqs = [ct_ref[0, pl.ds(2 * pair + hh, 1), :] for hh in range(2)]

    def decayed_scores(j, hh):
        start = pl.multiple_of(j * TK, TK)
        s = _scores_t(k_ref[0, pl.ds(start, TK), :], qms[hh])
        cblk = c_ref[0, pl.ds(start, TK), :]
        ck = jnp.sum(jnp.where(lane_c == 2 * pair + hh, cblk, 0.0), axis=1, keepdims=True)
        return s - ck

    for hh in range(2):
        u = jnp.where(causal, decayed_scores(qi, hh), -jnp.inf)
        m = jnp.max(u, axis=0, keepdims=True) + cqs[hh]
        p = jnp.exp(u - (m - cqs[hh]))
        m_ref[hh] = m
        l_ref[hh] = jnp.sum(p, axis=0, keepdims=True)
        acc_ref[hh] = jnp.dot(vt_ref[0, 0, qi], p.astype(BF16), preferred_element_type=F32)

    def body(j, carry):
        for hh in range(2):
            u = decayed_scores(j, hh)
            m_old = m_ref[hh]
            m_new = jnp.maximum(m_old, jnp.max(u, axis=0, keepdims=True) + cqs[hh])
            alpha = jnp.exp(m_old - m_new)
            p = jnp.exp(u - (m_new - cqs[hh]))
            l_ref[hh] = alpha * l_ref[hh] + jnp.sum(p, axis=0, keepdims=True)
            acc_ref[hh] = alpha * acc_ref[hh] + jnp.dot(vt_ref[0, 0, j], p.astype(BF16),
                                                        preferred_element_type=F32)
            m_ref[hh] = m_new
        return carry

    lax.fori_loop(0, qi, body, 0)
    o0 = acc_ref[0] * (1.0 / l_ref[0])
    o1 = acc_ref[1] * (1.0 / l_ref[1])
    ot = jnp.concatenate([o0[:HEAD_DIM], o1[HEAD_DIM:]], axis=0)
    o_ref[0] = ot.T.astype(o_ref.dtype)


def _fox_call(qk, vt, c, ct, batch, seq):
    n_pairs = N_FOX_HEADS // 2
    nkb = seq // TK
    return pl.pallas_call(
        _fox_kernel,
        grid=(batch, n_pairs, seq // TQ),
        in_specs=[
            pl.BlockSpec((1, TQ, LANES), lambda b, p, qi: (b, qi, 8 + p)),
            pl.BlockSpec((1, seq, LANES), lambda b, p, qi: (b, 0, 12 + p)),
            pl.BlockSpec((1, 1, nkb, LANES, TK), lambda b, p, qi: (b, N_DIFF_HEADS + p, 0, 0, 0)),
            pl.BlockSpec((1, N_FOX_HEADS, TQ), lambda b, p, qi: (b, 0, qi)),
            pl.BlockSpec((1, seq, LANES), lambda b, p, qi: (b, 0, 0)),
        ],
        out_specs=pl.BlockSpec((1, TQ, LANES), lambda b, p, qi: (b, qi, p)),
        out_shape=jax.ShapeDtypeStruct((batch, seq, D_FOX), BF16),
        scratch_shapes=[pltpu.VMEM((2, LANES, TQ), F32), pltpu.VMEM((2, 1, TQ), F32),
                        pltpu.VMEM((2, 1, TQ), F32)],
        compiler_params=_params(("arbitrary", "arbitrary", "arbitrary")),
    )(qk, qk, vt, ct, c)


def _diff_kernel(lam_ref, q1_ref, q2_ref, k1_ref, k2_ref, vt_ref, g_ref, o_ref,
                 acc_ref, m_ref, l_ref, *, out_scale):
    head = pl.program_id(1)
    qi = pl.program_id(2)
    half = head % 2
    causal = _causal_mask()
    qms = [_head_mask(q1_ref[0], half), _head_mask(q2_ref[0], half)]
    k_refs = [k1_ref, k2_ref]

    def scores(j, br):
        start = pl.multiple_of(j * TK, TK)
        return _scores_t(k_refs[br][0, pl.ds(start, TK), :], qms[br])

    for br in range(2):
        s = jnp.where(causal, scores(qi, br), -jnp.inf)
        m = jnp.max(s, axis=0, keepdims=True)
        p = jnp.exp(s - m)
        m_ref[br] = m
        l_ref[br] = jnp.sum(p, axis=0, keepdims=True)
        acc_ref[br] = jnp.dot(vt_ref[0, 0, qi], p.astype(BF16), preferred_element_type=F32)

    def body(j, carry):
        for br in range(2):
            s = scores(j, br)
            m_old = m_ref[br]
            m_new = jnp.maximum(m_old, jnp.max(s, axis=0, keepdims=True))
            alpha = jnp.exp(m_old - m_new)
            p = jnp.exp(s - m_new)
            l_ref[br] = alpha * l_ref[br] + jnp.sum(p, axis=0, keepdims=True)
            acc_ref[br] = alpha * acc_ref[br] + jnp.dot(vt_ref[0, 0, j], p.astype(BF16),
                                                        preferred_element_type=F32)
            m_ref[br] = m_new
        return carry

    lax.fori_loop(0, qi, body, 0)
    lam = lam_ref[0]
    ot = acc_ref[0] * (1.0 / l_ref[0]) - lam * (acc_ref[1] * (1.0 / l_ref[1]))
    y = ot * lax.rsqrt(jnp.mean(ot * ot, axis=0, keepdims=True) + SUBLN_EPS)
    y = (y * g_ref[...]) * out_scale
    o_ref[0] = y.T.astype(o_ref.dtype)


def _diff_call(lam, qk, vt, g_col, out_scale, batch, seq):
    nkb = seq // TK
    kern = functools.partial(_diff_kernel, out_scale=out_scale)
    grid_spec = pltpu.PrefetchScalarGridSpec(
        num_scalar_prefetch=1,
        grid=(batch, N_DIFF_HEADS, seq // TQ),
        in_specs=[
            pl.BlockSpec((1, TQ, LANES), lambda b, h, qi, lam: (b, qi, h // 2)),
            pl.BlockSpec((1, TQ, LANES), lambda b, h, qi, lam: (b, qi, 2 + h // 2)),
            pl.BlockSpec((1, seq, LANES), lambda b, h, qi, lam: (b, 0, 4 + h // 2)),
            pl.BlockSpec((1, seq, LANES), lambda b, h, qi, lam: (b, 0, 6 + h // 2)),
            pl.BlockSpec((1, 1, nkb, LANES, TK), lambda b, h, qi, lam: (b, h, 0, 0, 0)),
            pl.BlockSpec((DIFF_V_DIM, 1), lambda b, h, qi, lam: (0, 0)),
        ],
        out_specs=pl.BlockSpec((1, TQ, LANES), lambda b, h, qi, lam: (b, qi, h)),
        scratch_shapes=[pltpu.VMEM((2, LANES, TQ), F32), pltpu.VMEM((2, 1, TQ), F32),
                        pltpu.VMEM((2, 1, TQ), F32)],
    )
    return pl.pallas_call(
        kern,
        grid_spec=grid_spec,
        out_shape=jax.ShapeDtypeStruct((batch, seq, D_DIFF_OUT), BF16),
        compiler_params=_params(("arbitrary", "arbitrary", "arbitrary")),
    )(lam, qk, qk, qk, qk, vt, g_col)


def _norm_matmul_kernel(x_ref, g_ref, w_ref, o_ref):
    x = x_ref[...]
    xn = ((x * _rms_scale(x, EPS)) * g_ref[...]).astype(BF16)
    o_ref[...] = jnp.dot(xn, w_ref[...], preferred_element_type=F32).astype(o_ref.dtype)


def _norm_matmul_call(x, g, w, tm, tn):
    t, d = x.shape
    n = w.shape[1]
    return pl.pallas_call(
        _norm_matmul_kernel,
        grid=(t // tm, n // tn),
        in_specs=[pl.BlockSpec((tm, d), lambda i, j: (i, 0)),
                  pl.BlockSpec((1, d), lambda i, j: (0, 0)),
                  pl.BlockSpec((d, tn), lambda i, j: (0, j))],
        out_specs=pl.BlockSpec((tm, tn), lambda i, j: (i, j)),
        out_shape=jax.ShapeDtypeStruct((t, n), BF16),
        compiler_params=_params(("arbitrary", "arbitrary")),
    )(x, g, w)


def _mix_mem_kernel(h_ref, od_ref, of_ref, wod_ref, wof_ref, g_ref, wq_ref, k_ref, v_ref, wo_ref,
                    o_ref, *, mem_scale):
    h1 = (h_ref[...]
          + jnp.dot(od_ref[...], wod_ref[...], preferred_element_type=F32)
          + jnp.dot(of_ref[...], wof_ref[...], preferred_element_type=F32))
    xn = ((h1 * _rms_scale(h1, EPS)) * g_ref[...]).astype(BF16)
    q = jnp.dot(xn, wq_ref[...], preferred_element_type=F32).astype(BF16)
    d = q.shape[1]
    hd = d // MEM_HEADS
    outs = []
    for hh in range(MEM_HEADS):
        qh = q[:, hh * hd:(hh + 1) * hd]
        kh = k_ref[:, hh * hd:(hh + 1) * hd]
        vh = v_ref[:, hh * hd:(hh + 1) * hd]
        s = lax.dot_general(qh, kh, (((1,), (1,)), ((), ())),
                            preferred_element_type=F32) * mem_scale
        m = jnp.max(s, axis=-1, keepdims=True)
        e = jnp.exp(s - m)
        p = e * (1.0 / jnp.sum(e, axis=-1, keepdims=True))
        outs.append(jnp.dot(p.astype(BF16), vh, preferred_element_type=F32).astype(BF16))
    o = jnp.concatenate(outs, axis=1)
    o_ref[...] = h1 + jnp.dot(o, wo_ref[...], preferred_element_type=F32)


def _mix_mem_call(h, o_d, o_f, w_od, w_of, g, w_q, kv, w_o, seq, n_mem):
    t, d = h.shape
    tm = TM_MIX
    tiles_per_seq = seq // tm
    kern = functools.partial(_mix_mem_kernel, mem_scale=1.0 / math.sqrt(d // MEM_HEADS))
    const = lambda i: (0, 0)
    return pl.pallas_call(
        kern,
        grid=(t // tm,),
        in_specs=[
            pl.BlockSpec((tm, d), lambda i: (i, 0)),
            pl.BlockSpec((tm, D_DIFF_OUT), lambda i: (i, 0)),
            pl.BlockSpec((tm, D_FOX), lambda i: (i, 0)),
            pl.BlockSpec((D_DIFF_OUT, d), const),
            pl.BlockSpec((D_FOX, d), const),
            pl.BlockSpec((1, d), const),
            pl.BlockSpec((d, d), const),
            pl.BlockSpec((n_mem, d), lambda i: (i // tiles_per_seq, 0)),
            pl.BlockSpec((n_mem, d), lambda i: (i // tiles_per_seq, 1)),
            pl.BlockSpec((d, d), const),
        ],
        out_specs=pl.BlockSpec((tm, d), lambda i: (i, 0)),
        out_shape=jax.ShapeDtypeStruct((t, d), F32),
        compiler_params=_params(("arbitrary",)),
    )(h, o_d, o_f, w_od, w_of, g, w_q, kv, kv, w_o)


def _swiglu_chunk(x, w1, w3, w2):
    a = jnp.dot(x, w1, preferred_element_type=F32)
    b = jnp.dot(x, w3, preferred_element_type=F32)
    mid = (a * (1.0 / (1.0 + jnp.exp(-a)))) * b
    return jnp.dot(mid.astype(BF16), w2, preferred_element_type=F32)


def _dense_ffn_kernel(h_ref, g_ref, w1_ref, w3_ref, w2_ref, o_ref, xn_ref):
    f = pl.program_id(1)

    @pl.when(f == 0)
    def _():
        x = h_ref[...]
        xn_ref[...] = ((x * _rms_scale(x, EPS)) * g_ref[...]).astype(BF16)
        o_ref[...] = x

    o_ref[...] += _swiglu_chunk(xn_ref[...], w1_ref[...], w3_ref[...], w2_ref[...])


def _dense_ffn_call(h, g, w1, w3, w2):
    t, d = h.shape
    tm, tf = TM_FFN, TF_DENSE
    dff = w1.shape[1]
    return pl.pallas_call(
        _dense_ffn_kernel,
        grid=(t // tm, dff // tf),
        in_specs=[pl.BlockSpec((tm, d), lambda i, f: (i, 0)),
                  pl.BlockSpec((1, d), lambda i, f: (0, 0)),
                  pl.BlockSpec((d, tf), lambda i, f: (0, f)),
                  pl.BlockSpec((d, tf), lambda i, f: (0, f)),
                  pl.BlockSpec((tf, d), lambda i, f: (f, 0))],
        out_specs=pl.BlockSpec((tm, d), lambda i, f: (i, 0)),
        out_shape=jax.ShapeDtypeStruct((t, d), F32),
        scratch_shapes=[pltpu.VMEM((tm, d), BF16)],
        compiler_params=_params(("arbitrary", "arbitrary")),
    )(h, g, w1, w3, w2)


def _router_kernel(h_ref, g_ref, wr_ref, hn_ref, idx_ref, gate_ref):
    x = h_ref[...]
    xn = ((x * _rms_scale(x, EPS)) * g_ref[...]).astype(BF16)
    hn_ref[...] = xn
    logits = jnp.dot(xn, wr_ref[...], preferred_element_type=F32)
    lane = lax.broadcasted_iota(jnp.int32, logits.shape, 1)
    logits = jnp.where(lane < N_EXPERTS, logits, -jnp.inf)
    v1 = jnp.max(logits, axis=-1, keepdims=True)
    i1 = jnp.min(jnp.where(logits == v1, lane, LANES), axis=-1, keepdims=True)
    rest = jnp.where(lane == i1, -jnp.inf, logits)
    v2 = jnp.max(rest, axis=-1, keepdims=True)
    i2 = jnp.min(jnp.where(rest == v2, lane, LANES), axis=-1, keepdims=True)
    e = jnp.exp(v2 - v1)
    inv = 1.0 / (1.0 + e)
    idx_ref[...] = jnp.where(lane == 0, i1, jnp.where(lane == 1, i2, 0))
    gate_ref[...] = jnp.where(lane == 0, inv, jnp.where(lane == 1, e * inv, 0.0))


def _router_call(h, g, w_r):
    t, d = h.shape
    tm = TM_FFN
    return pl.pallas_call(
        _router_kernel,
        grid=(t // tm,),
        in_specs=[pl.BlockSpec((tm, d), lambda i: (i, 0)),
                  pl.BlockSpec((1, d), lambda i: (0, 0)),
                  pl.BlockSpec((d, LANES), lambda i: (0, 0))],
        out_specs=[pl.BlockSpec((tm, d), lambda i: (i, 0)),
                   pl.BlockSpec((tm, LANES), lambda i: (i, 0)),
                   pl.BlockSpec((tm, LANES), lambda i: (i, 0))],
        out_shape=[jax.ShapeDtypeStruct((t, d), BF16),
                   jax.ShapeDtypeStruct((t, LANES), jnp.int32),
                   jax.ShapeDtypeStruct((t, LANES), F32)],
        compiler_params=_params(("arbitrary",)),
    )(h, g, w_r)


def _expert_ffn_kernel(te_ref, nu_ref, x_ref, gate_ref, w1_ref, w3_ref, w2_ref, o_ref, acc_ref,
                       *, n_chunks):
    i = pl.program_id(0)
    f = pl.program_id(1)

    @pl.when(i < nu_ref[0])
    def _():
        @pl.when(f == 0)
        def _():
            acc_ref[...] = jnp.zeros_like(acc_ref)

        acc_ref[...] += _swiglu_chunk(x_ref[...], w1_ref[0], w3_ref[0], w2_ref[0])

        @pl.when(f == n_chunks - 1)
        def _():
            o_ref[...] = (acc_ref[...] * gate_ref[...]).astype(o_ref.dtype)


def _expert_ffn_call(tile_expert, n_used, x_sorted, row_gate, w1, w3, w2):
    p_rows, d = x_sorted.shape
    tm, tf = TM_FFN, TF_MOE
    dff = w1.shape[2]
    n_chunks = dff // tf
    kern = functools.partial(_expert_ffn_kernel, n_chunks=n_chunks)

    def row_map(i, f, te, nu):
        return (jnp.minimum(i, nu[0] - 1), 0)

    def chunk(i, f, nu):
        return jnp.where(i < nu[0], f, n_chunks - 1)

    grid_spec = pltpu.PrefetchScalarGridSpec(
        num_scalar_prefetch=2,
        grid=(p_rows // tm, n_chunks),
        in_specs=[
            pl.BlockSpec((tm, d), row_map),
            pl.BlockSpec((tm, 1), row_map),
            pl.BlockSpec((1, d, tf), lambda i, f, te, nu: (te[i], 0, chunk(i, f, nu))),
            pl.BlockSpec((1, d, tf), lambda i, f, te, nu: (te[i], 0, chunk(i, f, nu))),
            pl.BlockSpec((1, tf, d), lambda i, f, te, nu: (te[i], chunk(i, f, nu), 0)),
        ],
        out_specs=pl.BlockSpec((tm, d), row_map),
        scratch_shapes=[pltpu.VMEM((tm, d), F32)],
    )
    return pl.pallas_call(
        kern,
        grid_spec=grid_spec,
        out_shape=jax.ShapeDtypeStruct((p_rows, d), BF16),
        compiler_params=_params(("arbitrary", "arbitrary")),
    )(tile_expert, n_used, x_sorted, row_gate, w1, w3, w2)


def _route(idx, gates, tm):
    t = idx.shape[0]
    n_assign = 2 * t
    p_rows = n_assign + N_EXPERTS * tm
    e_flat = idx.reshape(-1)
    onehot = (e_flat[:, None] == jnp.arange(N_EXPERTS, dtype=jnp.int32)[None, :]).astype(jnp.int32)
    running = jnp.cumsum(onehot, axis=0)
    counts = running[-1]
    rank = jnp.sum(running * onehot, axis=1) - 1
    padded = ((counts + tm - 1) // tm) * tm
    pend = jnp.cumsum(padded)
    pstart = pend - padded
    dest = pstart[e_flat] + rank
    token = jnp.arange(n_assign, dtype=jnp.int32) // 2
    row_token = jnp.zeros((p_rows,), jnp.int32).at[dest].set(token)
    row_gate = jnp.zeros((p_rows,), F32).at[dest].set(gates.reshape(-1))
    n_tiles = p_rows // tm
    n_used = (pend[-1] // tm).astype(jnp.int32)
    tile_start = jnp.arange(n_tiles, dtype=jnp.int32) * tm
    tile_expert = jnp.sum((tile_start[:, None] >= pend[None, :]).astype(jnp.int32), axis=1)
    last_expert = jnp.sum((((n_used - 1) * tm) >= pend).astype(jnp.int32))
    tile_expert = jnp.where(tile_start < pend[-1], tile_expert, last_expert).astype(jnp.int32)
    return row_token, row_gate.reshape(p_rows, 1), dest.reshape(t, 2), tile_expert, n_used.reshape(1)


def _moe_ffn(h, g, w_r, w1, w3, w2):
    hn, idx_l, gate_l = _router_call(h, g, w_r)
    row_token, row_gate, slot, tile_expert, n_used = _route(idx_l[:, :2], gate_l[:, :2], TM_FFN)
    x_sorted = jnp.take(hn, row_token, axis=0)
    y = _expert_ffn_call(tile_expert, n_used, x_sorted, row_gate, w1, w3, w2)
    return (h + jnp.take(y, slot[:, 0], axis=0).astype(F32)
            + jnp.take(y, slot[:, 1], axis=0).astype(F32))


def _final_norm_kernel(x_ref, g_ref, o_ref):
    x = x_ref[...]
    o_ref[...] = (x * _rms_scale(x, EPS)) * g_ref[...]


def _final_norm_call(h, g):
    t, d = h.shape
    tm = TM_FFN
    return pl.pallas_call(
        _final_norm_kernel,
        grid=(t // tm,),
        in_specs=[pl.BlockSpec((tm, d), lambda i: (i, 0)), pl.BlockSpec((1, d), lambda i: (0, 0))],
        out_specs=pl.BlockSpec((tm, d), lambda i: (i, 0)),
        out_shape=jax.ShapeDtypeStruct((t, d), F32),
        compiler_params=_params(("arbitrary",)),
    )(h, g)


def _rope_tables(positions):
    inv_freq = ROPE_THETA ** (-jnp.arange(0, ROPE_DIM, 2, dtype=F32) / ROPE_DIM)
    ang = positions.astype(F32)[..., None] * inv_freq
    cos, sin = jnp.cos(ang), jnp.sin(ang)
    b, s = positions.shape
    ones = jnp.ones((b, s, HEAD_DIM - ROPE_DIM), F32)
    zeros_h = jnp.zeros((b, s, ROPE_HALF), F32)
    zeros_r = jnp.zeros((b, s, HEAD_DIM - ROPE_DIM), F32)
    cos_h = jnp.concatenate([cos, cos, ones], axis=-1)
    sa_h = jnp.concatenate([zeros_h, sin, zeros_r], axis=-1)
    sb_h = jnp.concatenate([-sin, zeros_h, zeros_r], axis=-1)
    pair = lambda a: jnp.concatenate([a, a], axis=-1).reshape(b * s, LANES)
    return pair(cos_h), pair(sa_h), pair(sb_h)


def _split_w_in(w):
    q1, q2, k1, k2 = (w[:, n * D_DIFF_QK:(n + 1) * D_DIFF_QK] for n in range(4))
    o = 4 * D_DIFF_QK
    v_d = w[:, o:o + D_DIFF_OUT]
    o += D_DIFF_OUT
    q_f, k_f, v_f = (w[:, o + n * D_FOX:o + (n + 1) * D_FOX] for n in range(3))
    o += 3 * D_FOX
    w_f = w[:, o:o + N_FOX_HEADS]
    main = jnp.concatenate([q1 * ATTN_SCALE, q2 * ATTN_SCALE, k1, k2, q_f * ATTN_SCALE, k_f, v_d, v_f],
                           axis=1).astype(BF16)
    w_f = jnp.pad(w_f, ((0, 0), (0, LANES - N_FOX_HEADS))).astype(BF16)
    return main, w_f


def kernel(x, mem, positions, norm_mix_g, w_in, lam_q1, lam_k1, lam_q2, lam_k2, diff_subln_g, fox_bias, w_out, norm_mem_g, mem_norm_g, w_mq, w_mkv, w_mo, norm_ffn_g, w_ff1, w_ff3, w_ff2, w_router, w_e1, w_e3, w_e2, final_norm_g):
    batch, seq, d = x.shape
    n_mem = mem.shape[1]
    depth = w_in.shape[0]
    t = batch * seq
    cos_t, sa_t, sb_t = _rope_tables(positions)
    h = x.reshape(t, d)
    mem2 = mem.reshape(batch * n_mem, d)
    row = lambda v: v.reshape(1, -1).astype(F32)

    for l in range(depth):
        lam_init = 0.8 - 0.6 * math.exp(-0.3 * l)
        lam = (jnp.exp(jnp.sum(lam_q1[l] * lam_k1[l])) - jnp.exp(jnp.sum(lam_q2[l] * lam_k2[l]))
               + lam_init).reshape(1).astype(F32)
        w_main, w_f = _split_w_in(w_in[l])
        f_bias = jnp.pad(fox_bias[l], (0, LANES - N_FOX_HEADS)).reshape(1, LANES)

        qk, vt, c = _proj_call(h, row(norm_mix_g[l]), w_main, w_f, f_bias, cos_t, sa_t, sb_t, batch, seq)
        qk3 = qk.reshape(batch, seq, qk.shape[1])
        c3 = c.reshape(batch, seq, LANES)
        ct = jnp.transpose(c3[:, :, :N_FOX_HEADS], (0, 2, 1))
        o_d = _diff_call(lam, qk3, vt, diff_subln_g[l].reshape(DIFF_V_DIM, 1), 1.0 - lam_init, batch, seq)
        o_f = _fox_call(qk3, vt, c3, ct, batch, seq)

        w_o = w_out[l].astype(BF16)
        kv = _norm_matmul_call(mem2, row(mem_norm_g[l]), w_mkv[l].astype(BF16), 512, 1024)
        h = _mix_mem_call(h, o_d.reshape(t, D_DIFF_OUT), o_f.reshape(t, D_FOX),
                          w_o[:D_DIFF_OUT], w_o[D_DIFF_OUT:], row(norm_mem_g[l]),
                          w_mq[l].astype(BF16), kv, w_mo[l].astype(BF16), seq, n_mem)

        i = l // 2
        if l % 2 == 0:
            h = _dense_ffn_call(h, row(norm_ffn_g[l]), w_ff1[i].astype(BF16), w_ff3[i].astype(BF16),
                                w_ff2[i].astype(BF16))
        else:
            w_r = jnp.pad(w_router[i], ((0, 0), (0, LANES - N_EXPERTS))).astype(BF16)
            h = _moe_ffn(h, row(norm_ffn_g[l]), w_r, w_e1[i].astype(BF16), w_e3[i].astype(BF16),
                         w_e2[i].astype(BF16))

    return _final_norm_call(h, row(final_norm_g)).reshape(batch, seq, d)
```

```python
import functools
import math

import jax
import jax.numpy as jnp
from jax import lax
from jax.experimental import pallas as pl
from jax.experimental.pallas import tpu as pltpu

F32 = jnp.float32
BF16 = jnp.bfloat16

HEAD_DIM = 64
N_DIFF_HEADS = 4
DIFF_V_DIM = 128
N_FOX_HEADS = 8
D_DIFF_QK = N_DIFF_HEADS * HEAD_DIM
D_DIFF_OUT = N_DIFF_HEADS * DIFF_V_DIM
D_FOX = N_FOX_HEADS * HEAD_DIM
ATTN_SCALE = 1.0 / math.sqrt(HEAD_DIM)
ROPE_DIM = HEAD_DIM // 4
ROPE_HALF = ROPE_DIM // 2
ROPE_THETA = 500000.0
MEM_HEADS = 4
N_EXPERTS = 8
EPS = 1e-6
SUBLN_EPS = 1e-5
LOG2E = 1.0 / math.log(2.0)

LANES = 128
VMEM_LIMIT = 48 * 1024 * 1024

TM_PROJ = 1024
TN_PROJ = 512
TQ = 512
TK = 512
TM_MIX = 512
TM_FFN = 1024
TF_DENSE = 256
TF_MOE = 512
CUM_CHUNK = 256
V_ROWS = LANES + 8
N_SPLIT = 3


def _params(sem):
    return pltpu.CompilerParams(dimension_semantics=sem, vmem_limit_bytes=VMEM_LIMIT)


def _rms_scale(x, eps):
    return lax.rsqrt(jnp.mean(x * x, axis=-1, keepdims=True) + eps)


def _split3(v):
    hi = v.astype(BF16)
    r1 = v - hi.astype(F32)
    mid = r1.astype(BF16)
    lo = (r1 - mid.astype(F32)).astype(BF16)
    return hi, mid, lo


def _proj_kernel(x_ref, g_ref, w_ref, wf_ref, fb_ref, sel_ref, cos_ref, sa_ref, sb_ref,
                 qk_ref, vt_ref, c_ref, kx_ref, xn_ref, carry_ref, *, tiles_per_seq, n_rope, n_qk):
    i = pl.program_id(0)
    j = pl.program_id(1)
    tm = x_ref.shape[0]

    @pl.when(j == 0)
    def _():
        x = x_ref[...]
        xn = (x * _rms_scale(x, EPS)) * g_ref[...]
        xnb = xn.astype(BF16)
        xn_ref[...] = xnb
        logit = jnp.dot(xnb, wf_ref[...], preferred_element_type=F32) + fb_ref[...]
        logf = jnp.minimum(logit, 0.0) - jnp.log1p(jnp.exp(-jnp.abs(logit)))

        @pl.when(i % tiles_per_seq == 0)
        def _():
            carry_ref[...] = jnp.zeros_like(carry_ref)

        r = lax.broadcasted_iota(jnp.int32, (CUM_CHUNK, CUM_CHUNK), 0)
        cidx = lax.broadcasted_iota(jnp.int32, (CUM_CHUNK, CUM_CHUNK), 1)
        tri = (cidx <= r).astype(BF16)
        carry = carry_ref[...]
        for ch in range(tm // CUM_CHUNK):
            hi, mid, lo = _split3(logf[ch * CUM_CHUNK:(ch + 1) * CUM_CHUNK])
            cs = (jnp.dot(tri, hi, preferred_element_type=F32)
                  + jnp.dot(tri, mid, preferred_element_type=F32)
                  + jnp.dot(tri, lo, preferred_element_type=F32)) + carry
            c_ref[ch * CUM_CHUNK:(ch + 1) * CUM_CHUNK, :] = cs * LOG2E
            carry = cs[CUM_CHUNK - 1:CUM_CHUNK, :]
        carry_ref[...] = carry
        hi, mid, lo = _split3(-c_ref[...])
        kx = (jnp.dot(hi, sel_ref[0], preferred_element_type=F32)
              + jnp.dot(mid, sel_ref[1], preferred_element_type=F32)
              + jnp.dot(lo, sel_ref[2], preferred_element_type=F32))
        kx_ref[...] = kx.astype(kx_ref.dtype)

    acc = jnp.dot(xn_ref[...], w_ref[...], preferred_element_type=F32)

    @pl.when(j < n_rope)
    def _():
        cos = cos_ref[...]
        sa = sa_ref[...]
        sb = sb_ref[...]
        for s in range(TN_PROJ // LANES):
            t = acc[:, s * LANES:(s + 1) * LANES]
            rot = (t * cos + pltpu.roll(t, ROPE_HALF, 1) * sa
                   + pltpu.roll(t, LANES - ROPE_HALF, 1) * sb)
            qk_ref[:, s * LANES:(s + 1) * LANES] = rot.astype(qk_ref.dtype)

    @pl.when(jnp.logical_and(j >= n_rope, j < n_qk))
    def _():
        qk_ref[...] = acc.astype(qk_ref.dtype)

    @pl.when(j >= n_qk)
    def _():
        at = acc.T
        ones_row = (lax.broadcasted_iota(jnp.int32, (V_ROWS - LANES, TK), 0) == 0).astype(vt_ref.dtype)
        for hd in range(TN_PROJ // LANES):
            for kb in range(tm // TK):
                vt_ref[0, hd, kb, 0:LANES, :] = at[hd * LANES:(hd + 1) * LANES,
                                                   kb * TK:(kb + 1) * TK].astype(vt_ref.dtype)
                vt_ref[0, hd, kb, LANES:V_ROWS, :] = ones_row


def _proj_call(h, g, w_main, w_f, f_bias, sel, cos_t, sa_t, sb_t, batch, seq):
    t, d = h.shape
    tm = TM_PROJ
    tiles_per_seq = seq // tm
    n_col = w_main.shape[1] // TN_PROJ
    n_rope, n_qk = 2, 4
    n_vb = TN_PROJ // LANES
    kern = functools.partial(_proj_kernel, tiles_per_seq=tiles_per_seq, n_rope=n_rope, n_qk=n_qk)
    return pl.pallas_call(
        kern,
        grid=(t // tm, n_col),
        in_specs=[
            pl.BlockSpec((tm, d), lambda i, j: (i, 0)),
            pl.BlockSpec((1, d), lambda i, j: (0, 0)),
            pl.BlockSpec((d, TN_PROJ), lambda i, j: (0, j)),
            pl.BlockSpec((d, LANES), lambda i, j: (0, 0)),
            pl.BlockSpec((1, LANES), lambda i, j: (0, 0)),
            pl.BlockSpec((N_SPLIT, LANES, D_FOX), lambda i, j: (0, 0, 0)),
            pl.BlockSpec((tm, LANES), lambda i, j: (i, 0)),
            pl.BlockSpec((tm, LANES), lambda i, j: (i, 0)),
            pl.BlockSpec((tm, LANES), lambda i, j: (i, 0)),
        ],
        out_specs=[
            pl.BlockSpec((tm, TN_PROJ), lambda i, j: (i, jnp.minimum(j, n_qk - 1))),
            pl.BlockSpec((1, n_vb, tm // TK, V_ROWS, TK),
                         lambda i, j: (i // tiles_per_seq, jnp.maximum(j - n_qk, 0),
                                       i % tiles_per_seq, 0, 0)),
            pl.BlockSpec((tm, LANES), lambda i, j: (i, 0)),
            pl.BlockSpec((tm, D_FOX), lambda i, j: (i, 0)),
        ],
        out_shape=[
            jax.ShapeDtypeStruct((t, n_qk * TN_PROJ), BF16),
            jax.ShapeDtypeStruct((batch, (n_col - n_qk) * n_vb, seq // TK, V_ROWS, TK), BF16),
            jax.ShapeDtypeStruct((t, LANES), F32),
            jax.ShapeDtypeStruct((t, D_FOX), BF16),
        ],
        scratch_shapes=[pltpu.VMEM((tm, d), BF16), pltpu.VMEM((1, LANES), F32)],
        compiler_params=_params(("arbitrary", "arbitrary")),
    )(h, g, w_main, w_f, f_bias, sel, cos_t, sa_t, sb_t)


def _head_mask(q, half):
    lane = lax.broadcasted_iota(jnp.int32, q.shape, 1)
    return jnp.where(lane // HEAD_DIM == half, q, jnp.zeros_like(q))


def _causal_mask(jj):
    kpos = lax.broadcasted_iota(jnp.int32, (TK, TQ), 0) + jj * TK
    qpos = lax.broadcasted_iota(jnp.int32, (TK, TQ), 1)
    return kpos <= qpos


def _scores_t(k, q):
    return lax.dot_general(k, q, (((1,), (1,)), ((), ())), preferred_element_type=F32)


def _online_update(u, vt, shift, acc_ref, m_ref, slot, first):
    mu = jnp.max(u, axis=0, keepdims=True) + shift
    if first:
        m_new = mu
    else:
        m_old = m_ref[slot]
        m_new = jnp.maximum(m_old, mu)
    p = jnp.exp2(u - (m_new - shift)).astype(BF16)
    pv = jnp.dot(vt, p, preferred_element_type=F32)
    if first:
        acc_ref[slot] = pv
    else:
        acc_ref[slot] = jnp.exp2(m_old - m_new) * acc_ref[slot] + pv
    m_ref[slot] = m_new


def _normalised(acc_ref, slot):
    acc = acc_ref[slot]
    return acc[:LANES] * (1.0 / acc[LANES:LANES + 1])


def _attend(n_full, first_block, tiles, acc_ref, m_ref):
    ratio = TQ // TK

    def block(j, mask, first):
        specs = tiles(j)
        us = [fn() for fn, _, _ in specs]
        for slot, (u, (_, vt, shift)) in enumerate(zip(us, specs)):
            if mask is not None:
                u = jnp.where(mask, u, -jnp.inf)
            _online_update(u, vt, shift, acc_ref, m_ref, slot, first)

    for jj in range(ratio):
        block(first_block + jj, _causal_mask(jj), jj == 0)

    def body(j, carry):
        block(j, None, False)
        return carry

    lax.fori_loop(0, n_full, body, 0)


def _fox_kernel(q_ref, k_ref, kx_ref, vt_ref, ct_ref, o_ref, acc_ref, m_ref):
    grp = pl.program_id(1)
    qi = pl.program_id(2)
    ratio = TQ // TK
    lane = lax.broadcasted_iota(jnp.int32, (TQ, LANES), 1)
    qas, cqs = [], []
    for pp in range(2):
        q = q_ref[0, :, pp * LANES:(pp + 1) * LANES]
        for hh in range(2):
            pick = jnp.logical_and(lane >= N_SPLIT * hh, lane < N_SPLIT * (hh + 1))
            qas.append(jnp.concatenate([_head_mask(q, hh), pick.astype(q.dtype)], axis=1))
            cqs.append(ct_ref[0, pl.ds(4 * grp + 2 * pp + hh, 1), :])

    def tiles(j):
        start = pl.multiple_of(j * TK, TK)
        specs = []
        for pp in range(2):
            ka = jnp.concatenate([k_ref[0, pl.ds(start, TK), pp * LANES:(pp + 1) * LANES],
                                  kx_ref[0, pl.ds(start, TK), pp * LANES:(pp + 1) * LANES]], axis=1)
            vt = vt_ref[0, pp, j]
            for hh in range(2):
                specs.append((functools.partial(_scores_t, ka, qas[2 * pp + hh]), vt, cqs[2 * pp + hh]))
        return specs

    _attend(qi * ratio, qi * ratio, tiles, acc_ref, m_ref)
    for pp in range(2):
        ot = jnp.concatenate([_normalised(acc_ref, 2 * pp)[:HEAD_DIM],
                              _normalised(acc_ref, 2 * pp + 1)[HEAD_DIM:]], axis=0)
        o_ref[0, :, pp * LANES:(pp + 1) * LANES] = ot.T.astype(o_ref.dtype)


def _fox_call(qk, kx, vt, ct, batch, seq):
    nkb = seq // TK
    w2 = 2 * LANES
    return pl.pallas_call(
        _fox_kernel,
        grid=(batch, 2, seq // TQ),
        in_specs=[
            pl.BlockSpec((1, TQ, w2), lambda b, g, qi: (b, qi, 4 + g)),
            pl.BlockSpec((1, seq, w2), lambda b, g, qi: (b, 0, 6 + g)),
            pl.BlockSpec((1, seq, w2), lambda b, g, qi: (b, 0, g)),
            pl.BlockSpec((1, 2, nkb, V_ROWS, TK), lambda b, g, qi: (b, N_DIFF_HEADS // 2 + g, 0, 0, 0)),
            pl.BlockSpec((1, N_FOX_HEADS, TQ), lambda b, g, qi: (b, 0, qi)),
        ],
        out_specs=pl.BlockSpec((1, TQ, w2), lambda b, g, qi: (b, qi, g)),
        out_shape=jax.ShapeDtypeStruct((batch, seq, D_FOX), BF16),
        scratch_shapes=[pltpu.VMEM((4, V_ROWS, TQ), F32), pltpu.VMEM((4, 1, TQ), F32)],
        compiler_params=_params(("arbitrary", "arbitrary", "arbitrary")),
    )(qk, qk, kx, vt, ct)


def _diff_kernel(lam_ref, q1_ref, q2_ref, k1_ref, k2_ref, vt_ref, g_ref, o_ref,
                 acc_ref, m_ref, *, out_scale):
    qi = pl.program_id(2)
    ratio = TQ // TK
    q_refs = [q1_ref, q2_ref]
    k_refs = [k1_ref, k2_ref]
    qms = [_head_mask(q_refs[br][0], hh) for hh in range(2) for br in range(2)]

    def tiles(j):
        start = pl.multiple_of(j * TK, TK)
        ks = [k_refs[br][0, pl.ds(start, TK), :] for br in range(2)]
        specs = []
        for hh in range(2):
            vt = vt_ref[0, hh, j]
            for br in range(2):
                specs.append((functools.partial(_scores_t, ks[br], qms[2 * hh + br]), vt, 0.0))
        return specs

    _attend(qi * ratio, qi * ratio, tiles, acc_ref, m_ref)
    for hh in range(2):
        ot = _normalised(acc_ref, 2 * hh) - lam_ref[0] * _normalised(acc_ref, 2 * hh + 1)
        y = ot * lax.rsqrt(jnp.mean(ot * ot, axis=0, keepdims=True) + SUBLN_EPS)
        y = (y * g_ref[...]) * out_scale
        o_ref[0, :, hh * LANES:(hh + 1) * LANES] = y.T.astype(o_ref.dtype)


def _diff_call(lam, qk, vt, g_col, out_scale, batch, seq):
    nkb = seq // TK
    n_pairs = N_DIFF_HEADS // 2
    kern = functools.partial(_diff_kernel, out_scale=out_scale)
    grid_spec = pltpu.PrefetchScalarGridSpec(
        num_scalar_prefetch=1,
        grid=(batch, n_pairs, seq // TQ),
        in_specs=[
            pl.BlockSpec((1, TQ, LANES), lambda b, p, qi, lam: (b, qi, p)),
            pl.BlockSpec((1, TQ, LANES), lambda b, p, qi, lam: (b, qi, 2 + p)),
            pl.BlockSpec((1, seq, LANES), lambda b, p, qi, lam: (b, 0, 4 + p)),
            pl.BlockSpec((1, seq, LANES), lambda b, p, qi, lam: (b, 0, 6 + p)),
            pl.BlockSpec((1, 2, nkb, V_ROWS, TK), lambda b, p, qi, lam: (b, p, 0, 0, 0)),
            pl.BlockSpec((DIFF_V_DIM, 1), lambda b, p, qi, lam: (0, 0)),
        ],
        out_specs=pl.BlockSpec((1, TQ, 2 * LANES), lambda b, p, qi, lam: (b, qi, p)),
        scratch_shapes=[pltpu.VMEM((4, V_ROWS, TQ), F32), pltpu.VMEM((4, 1, TQ), F32)],
    )
    return pl.pallas_call(
        kern,
        grid_spec=grid_spec,
        out_shape=jax.ShapeDtypeStruct((batch, seq, D_DIFF_OUT), BF16),
        compiler_params=_params(("arbitrary", "arbitrary", "arbitrary")),
    )(lam, qk, qk, qk, qk, vt, g_col)


def _norm_matmul_kernel(x_ref, g_ref, w_ref, o_ref):
    x = x_ref[...]
    xn = ((x * _rms_scale(x, EPS)) * g_ref[...]).astype(BF16)
    o_ref[...] = jnp.dot(xn, w_ref[...], preferred_element_type=F32).astype(o_ref.dtype)


def _norm_matmul_call(x, g, w, tm, tn):
    t, d = x.shape
    n = w.shape[1]
    return pl.pallas_call(
        _norm_matmul_kernel,
        grid=(t // tm, n // tn),
        in_specs=[pl.BlockSpec((tm, d), lambda i, j: (i, 0)),
                  pl.BlockSpec((1, d), lambda i, j: (0, 0)),
                  pl.BlockSpec((d, tn), lambda i, j: (0, j))],
        out_specs=pl.BlockSpec((tm, tn), lambda i, j: (i, j)),
        out_shape=jax.ShapeDtypeStruct((t, n), BF16),
        compiler_params=_params(("arbitrary", "arbitrary")),
    )(x, g, w)


def _mix_mem_kernel(h_ref, od_ref, of_ref, wod_ref, wof_ref, g_ref, wq_ref, k_ref, v_ref, wo_ref,
                    o_ref, *, mem_scale):
    h1 = (h_ref[...]
          + jnp.dot(od_ref[...], wod_ref[...], preferred_element_type=F32)
          + jnp.dot(of_ref[...], wof_ref[...], preferred_element_type=F32))
    xn = ((h1 * _rms_scale(h1, EPS)) * g_ref[...]).astype(BF16)
    q = jnp.dot(xn, wq_ref[...], preferred_element_type=F32).astype(BF16)
    d = q.shape[1]
    hd = d // MEM_HEADS
    outs = []
    for hh in range(MEM_HEADS):
        qh = q[:, hh * hd:(hh + 1) * hd]
        kh = k_ref[:, hh * hd:(hh + 1) * hd]
        vh = v_ref[:, hh * hd:(hh + 1) * hd]
        s = lax.dot_general(qh, kh, (((1,), (1,)), ((), ())),
                            preferred_element_type=F32) * mem_scale
        m = jnp.max(s, axis=-1, keepdims=True)
        e = jnp.exp(s - m)
        p = e * (1.0 / jnp.sum(e, axis=-1, keepdims=True))
        outs.append(jnp.dot(p.astype(BF16), vh, preferred_element_type=F32).astype(BF16))
    o = jnp.concatenate(outs, axis=1)
    o_ref[...] = h1 + jnp.dot(o, wo_ref[...], preferred_element_type=F32)


def _mix_mem_call(h, o_d, o_f, w_od, w_of, g, w_q, kv, w_o, seq, n_mem):
    t, d = h.shape
    tm = TM_MIX
    tiles_per_seq = seq // tm
    kern = functools.partial(_mix_mem_kernel, mem_scale=1.0 / math.sqrt(d // MEM_HEADS))
    const = lambda i: (0, 0)
    return pl.pallas_call(
        kern,
        grid=(t // tm,),
        in_specs=[
            pl.BlockSpec((tm, d), lambda i: (i, 0)),
            pl.BlockSpec((tm, D_DIFF_OUT), lambda i: (i, 0)),
            pl.BlockSpec((tm, D_FOX), lambda i: (i, 0)),
            pl.BlockSpec((D_DIFF_OUT, d), const),
            pl.BlockSpec((D_FOX, d), const),
            pl.BlockSpec((1, d), const),
            pl.BlockSpec((d, d), const),
            pl.BlockSpec((n_mem, d), lambda i: (i // tiles_per_seq, 0)),
            pl.BlockSpec((n_mem, d), lambda i: (i // tiles_per_seq, 1)),
            pl.BlockSpec((d, d), const),
        ],
        out_specs=pl.BlockSpec((tm, d), lambda i: (i, 0)),
        out_shape=jax.ShapeDtypeStruct((t, d), F32),
        compiler_params=_params(("arbitrary",)),
    )(h, o_d, o_f, w_od, w_of, g, w_q, kv, kv, w_o)


def _swiglu_chunk(x, w1, w3, w2):
    a = jnp.dot(x, w1, preferred_element_type=F32)
    b = jnp.dot(x, w3, preferred_element_type=F32)
    mid = (a * (1.0 / (1.0 + jnp.exp(-a)))) * b
    return jnp.dot(mid.astype(BF16), w2, preferred_element_type=F32)


def _dense_ffn_kernel(h_ref, g_ref, w1_ref, w3_ref, w2_ref, o_ref, xn_ref):
    f = pl.program_id(1)

    @pl.when(f == 0)
    def _():
        x = h_ref[...]
        xn_ref[...] = ((x * _rms_scale(x, EPS)) * g_ref[...]).astype(BF16)
        o_ref[...] = x

    o_ref[...] += _swiglu_chunk(xn_ref[...], w1_ref[...], w3_ref[...], w2_ref[...])


def _dense_ffn_call(h, g, w1, w3, w2):
    t, d = h.shape
    tm, tf = TM_FFN, TF_DENSE
    dff = w1.shape[1]
    return pl.pallas_call(
        _dense_ffn_kernel,
        grid=(t // tm, dff // tf),
        in_specs=[pl.BlockSpec((tm, d), lambda i, f: (i, 0)),
                  pl.BlockSpec((1, d), lambda i, f: (0, 0)),
                  pl.BlockSpec((d, tf), lambda i, f: (0, f)),
                  pl.BlockSpec((d, tf), lambda i, f: (0, f)),
                  pl.BlockSpec((tf, d), lambda i, f: (f, 0))],
        out_specs=pl.BlockSpec((tm, d), lambda i, f: (i, 0)),
        out_shape=jax.ShapeDtypeStruct((t, d), F32),
        scratch_shapes=[pltpu.VMEM((tm, d), BF16)],
        compiler_params=_params(("arbitrary", "arbitrary")),
    )(h, g, w1, w3, w2)


def _router_kernel(h_ref, g_ref, wr_ref, hn_ref, idx_ref, gate_ref):
    x = h_ref[...]
    xn = ((x * _rms_scale(x, EPS)) * g_ref[...]).astype(BF16)
    hn_ref[...] = xn
    logits = jnp.dot(xn, wr_ref[...], preferred_element_type=F32)
    lane = lax.broadcasted_iota(jnp.int32, logits.shape, 1)
    logits = jnp.where(lane < N_EXPERTS, logits, -jnp.inf)
    v1 = jnp.max(logits, axis=-1, keepdims=True)
    i1 = jnp.min(jnp.where(logits == v1, lane, LANES), axis=-1, keepdims=True)
    rest = jnp.where(lane == i1, -jnp.inf, logits)
    v2 = jnp.max(rest, axis=-1, keepdims=True)
    i2 = jnp.min(jnp.where(rest == v2, lane, LANES), axis=-1, keepdims=True)
    e = jnp.exp(v2 - v1)
    inv = 1.0 / (1.0 + e)
    idx_ref[...] = jnp.where(lane == 0, i1, jnp.where(lane == 1, i2, 0))
    gate_ref[...] = jnp.where(lane == 0, inv, jnp.where(lane == 1, e * inv, 0.0))


def _router_call(h, g, w_r):
    t, d = h.shape
    tm = TM_FFN
    return pl.pallas_call(
        _router_kernel,
        grid=(t // tm,),
        in_specs=[pl.BlockSpec((tm, d), lambda i: (i, 0)),
                  pl.BlockSpec((1, d), lambda i: (0, 0)),
                  pl.BlockSpec((d, LANES), lambda i: (0, 0))],
        out_specs=[pl.BlockSpec((tm, d), lambda i: (i, 0)),
                   pl.BlockSpec((tm, LANES), lambda i: (i, 0)),
                   pl.BlockSpec((tm, LANES), lambda i: (i, 0))],
        out_shape=[jax.ShapeDtypeStruct((t, d), BF16),
                   jax.ShapeDtypeStruct((t, LANES), jnp.int32),
                   jax.ShapeDtypeStruct((t, LANES), F32)],
        compiler_params=_params(("arbitrary",)),
    )(h, g, w_r)


def _expert_ffn_kernel(te_ref, nu_ref, x_ref, w1_ref, w3_ref, w2_ref, o_ref, acc_ref, *, n_chunks):
    i = pl.program_id(0)
    f = pl.program_id(1)

    @pl.when(i < nu_ref[0])
    def _():
        @pl.when(f == 0)
        def _():
            acc_ref[...] = jnp.zeros_like(acc_ref)

        acc_ref[...] += _swiglu_chunk(x_ref[...], w1_ref[0], w3_ref[0], w2_ref[0])

        @pl.when(f == n_chunks - 1)
        def _():
            o_ref[...] = acc_ref[...].astype(o_ref.dtype)


def _expert_ffn_call(tile_expert, n_used, x_sorted, w1, w3, w2):
    p_rows, d = x_sorted.shape
    tm, tf = TM_FFN, TF_MOE
    dff = w1.shape[2]
    n_chunks = dff // tf
    kern = functools.partial(_expert_ffn_kernel, n_chunks=n_chunks)

    def row_map(i, f, te, nu):
        return (jnp.minimum(i, nu[0] - 1), 0)

    def chunk(i, f, nu):
        return jnp.where(i < nu[0], f, n_chunks - 1)

    grid_spec = pltpu.PrefetchScalarGridSpec(
        num_scalar_prefetch=2,
        grid=(p_rows // tm, n_chunks),
        in_specs=[
            pl.BlockSpec((tm, d), row_map),
            pl.BlockSpec((1, d, tf), lambda i, f, te, nu: (te[i], 0, chunk(i, f, nu))),
            pl.BlockSpec((1, d, tf), lambda i, f, te, nu: (te[i], 0, chunk(i, f, nu))),
            pl.BlockSpec((1, tf, d), lambda i, f, te, nu: (te[i], chunk(i, f, nu), 0)),
        ],
        out_specs=pl.BlockSpec((tm, d), row_map),
        scratch_shapes=[pltpu.VMEM((tm, d), F32)],
    )
    return pl.pallas_call(
        kern,
        grid_spec=grid_spec,
        out_shape=jax.ShapeDtypeStruct((p_rows, d), BF16),
        compiler_params=_params(("arbitrary", "arbitrary")),
    )(tile_expert, n_used, x_sorted, w1, w3, w2)


def _route(idx, tm):
    t = idx.shape[0]
    n_assign = 2 * t
    p_rows = n_assign + N_EXPERTS * tm
    e_flat = idx.reshape(-1)
    onehot = (e_flat[:, None] == jnp.arange(N_EXPERTS, dtype=jnp.int32)[None, :]).astype(jnp.int32)
    running = jnp.cumsum(onehot, axis=0)
    counts = running[-1]
    rank = jnp.sum(running * onehot, axis=1) - 1
    padded = ((counts + tm - 1) // tm) * tm
    pend = jnp.cumsum(padded)
    pstart = pend - padded
    gstart = jnp.cumsum(counts) - counts
    dest = pstart[e_flat] + rank
    n_tiles = p_rows // tm
    n_used = (pend[-1] // tm).astype(jnp.int32)
    tile_start = jnp.arange(n_tiles, dtype=jnp.int32) * tm
    tile_expert = jnp.sum((tile_start[:, None] >= pend[None, :]).astype(jnp.int32), axis=1)
    last_expert = jnp.sum((((n_used - 1) * tm) >= pend).astype(jnp.int32))
    tile_expert = jnp.where(tile_start < pend[-1], tile_expert, last_expert).astype(jnp.int32)
    order = jnp.sort(e_flat * n_assign + jnp.arange(n_assign, dtype=jnp.int32)) % n_assign
    row_expert = jnp.repeat(tile_expert, tm)
    within = jnp.arange(p_rows, dtype=jnp.int32) - pstart[row_expert]
    src = jnp.clip(gstart[row_expert] + within, 0, n_assign - 1)
    row_token = jnp.where(within < counts[row_expert], order[src] // 2, 0)
    return row_token, dest.reshape(t, 2), tile_expert, n_used.reshape(1)


def _moe_ffn(h, g, w_r, w1, w3, w2):
    hn, idx_l, gate_l = _router_call(h, g, w_r)
    row_token, slot, tile_expert, n_used = _route(idx_l[:, :2], TM_FFN)
    x_sorted = jnp.take(hn, row_token, axis=0)
    y = _expert_ffn_call(tile_expert, n_used, x_sorted, w1, w3, w2)
    return (h + gate_l[:, 0:1] * jnp.take(y, slot[:, 0], axis=0).astype(F32)
            + gate_l[:, 1:2] * jnp.take(y, slot[:, 1], axis=0).astype(F32))


def _final_norm_kernel(x_ref, g_ref, o_ref):
    x = x_ref[...]
    o_ref[...] = (x * _rms_scale(x, EPS)) * g_ref[...]


def _final_norm_call(h, g):
    t, d = h.shape
    tm = TM_FFN
    return pl.pallas_call(
        _final_norm_kernel,
        grid=(t // tm,),
        in_specs=[pl.BlockSpec((tm, d), lambda i: (i, 0)), pl.BlockSpec((1, d), lambda i: (0, 0))],
        out_specs=pl.BlockSpec((tm, d), lambda i: (i, 0)),
        out_shape=jax.ShapeDtypeStruct((t, d), F32),
        compiler_params=_params(("arbitrary",)),
    )(h, g)


def _rope_tables(positions):
    inv_freq = ROPE_THETA ** (-jnp.arange(0, ROPE_DIM, 2, dtype=F32) / ROPE_DIM)
    ang = positions.astype(F32)[..., None] * inv_freq
    cos, sin = jnp.cos(ang), jnp.sin(ang)
    b, s = positions.shape
    ones = jnp.ones((b, s, HEAD_DIM - ROPE_DIM), F32)
    zeros_h = jnp.zeros((b, s, ROPE_HALF), F32)
    zeros_r = jnp.zeros((b, s, HEAD_DIM - ROPE_DIM), F32)
    cos_h = jnp.concatenate([cos, cos, ones], axis=-1)
    sa_h = jnp.concatenate([zeros_h, sin, zeros_r], axis=-1)
    sb_h = jnp.concatenate([-sin, zeros_h, zeros_r], axis=-1)
    pair = lambda a: jnp.concatenate([a, a], axis=-1).reshape(b * s, LANES)
    return pair(cos_h), pair(sa_h), pair(sb_h)


def _decay_selectors():
    src = jnp.arange(LANES, dtype=jnp.int32)[:, None]
    dst = jnp.arange(D_FOX, dtype=jnp.int32)[None, :]
    sels = []
    for s in range(N_SPLIT):
        hit = jnp.logical_and(src < N_FOX_HEADS,
                              dst == (src // 2) * LANES + N_SPLIT * (src % 2) + s)
        sels.append(hit.astype(BF16))
    return jnp.stack(sels)


def _split_w_in(w):
    q1, q2, k1, k2 = (w[:, n * D_DIFF_QK:(n + 1) * D_DIFF_QK] for n in range(4))
    o = 4 * D_DIFF_QK
    v_d = w[:, o:o + D_DIFF_OUT]
    o += D_DIFF_OUT
    q_f, k_f, v_f = (w[:, o + n * D_FOX:o + (n + 1) * D_FOX] for n in range(3))
    o += 3 * D_FOX
    w_f = w[:, o:o + N_FOX_HEADS]
    qs = ATTN_SCALE * LOG2E
    main = jnp.concatenate([q1 * qs, q2 * qs, k1, k2, q_f * qs, k_f, v_d, v_f], axis=1).astype(BF16)
    w_f = jnp.pad(w_f, ((0, 0), (0, LANES - N_FOX_HEADS))).astype(BF16)
    return main, w_f


def kernel(x, mem, positions, norm_mix_g, w_in, lam_q1, lam_k1, lam_q2, lam_k2, diff_subln_g, fox_bias, w_out, norm_mem_g, mem_norm_g, w_mq, w_mkv, w_mo, norm_ffn_g, w_ff1, w_ff3, w_ff2, w_router, w_e1, w_e3, w_e2, final_norm_g):
    batch, seq, d = x.shape
    n_mem = mem.shape[1]
    depth = w_in.shape[0]
    t = batch * seq
    cos_t, sa_t, sb_t = _rope_tables(positions)
    sel = _decay_selectors()
    h = x.reshape(t, d)
    mem2 = mem.reshape(batch * n_mem, d)
    row = lambda v: v.reshape(1, -1).astype(F32)

    for l in range(depth):
        lam_init = 0.8 - 0.6 * math.exp(-0.3 * l)
        lam = (jnp.exp(jnp.sum(lam_q1[l] * lam_k1[l])) - jnp.exp(jnp.sum(lam_q2[l] * lam_k2[l]))
               + lam_init).reshape(1).astype(F32)
        w_main, w_f = _split_w_in(w_in[l])
        f_bias = jnp.pad(fox_bias[l], (0, LANES - N_FOX_HEADS)).reshape(1, LANES)

        qk, vt, c, kx = _proj_call(h, row(norm_mix_g[l]), w_main, w_f, f_bias, sel, cos_t, sa_t, sb_t,
                                   batch, seq)
        qk3 = qk.reshape(batch, seq, qk.shape[1])
        ct = jnp.transpose(c.reshape(batch, seq, LANES)[:, :, :N_FOX_HEADS], (0, 2, 1))
        o_d = _diff_call(lam, qk3, vt, diff_subln_g[l].reshape(DIFF_V_DIM, 1), 1.0 - lam_init, batch, seq)
        o_f = _fox_call(qk3, kx.reshape(batch, seq, D_FOX), vt, ct, batch, seq)

        w_o = w_out[l].astype(BF16)
        kv = _norm_matmul_call(mem2, row(mem_norm_g[l]), w_mkv[l].astype(BF16), 512, 1024)
        h = _mix_mem_call(h, o_d.reshape(t, D_DIFF_OUT), o_f.reshape(t, D_FOX),
                          w_o[:D_DIFF_OUT], w_o[D_DIFF_OUT:], row(norm_mem_g[l]),
                          w_mq[l].astype(BF16), kv, w_mo[l].astype(BF16), seq, n_mem)

        i = l // 2
        if l % 2 == 0:
            h = _dense_ffn_call(h, row(norm_ffn_g[l]), w_ff1[i].astype(BF16), w_ff3[i].astype(BF16),
                                w_ff2[i].astype(BF16))
        else:
            w_r = jnp.pad(w_router[i], ((0, 0), (0, LANES - N_EXPERTS))).astype(BF16)
            h = _moe_ffn(h, row(norm_ffn_g[l]), w_r, w_e1[i].astype(BF16), w_e3[i].astype(BF16),
                         w_e2[i].astype(BF16))

    return _final_norm_call(h, row(final_norm_g)).reshape(batch, seq, d)
```

```python
import functools
import math

import jax
import jax.numpy as jnp
from jax import lax
from jax.experimental import pallas as pl
from jax.experimental.pallas import tpu as pltpu

F32 = jnp.float32
BF16 = jnp.bfloat16

HEAD_DIM = 64
N_DIFF_HEADS = 4
DIFF_V_DIM = 128
N_FOX_HEADS = 8
D_DIFF_QK = N_DIFF_HEADS * HEAD_DIM
D_DIFF_OUT = N_DIFF_HEADS * DIFF_V_DIM
D_FOX = N_FOX_HEADS * HEAD_DIM
ATTN_SCALE = 1.0 / math.sqrt(HEAD_DIM)
ROPE_DIM = HEAD_DIM // 4
ROPE_HALF = ROPE_DIM // 2
ROPE_THETA = 500000.0
MEM_HEADS = 4
N_EXPERTS = 8
EPS = 1e-6
SUBLN_EPS = 1e-5
LOG2E = 1.0 / math.log(2.0)

LANES = 128
VMEM_LIMIT = 48 * 1024 * 1024

TM_PROJ = 512
TN_PROJ = 512
N_ROPE_BLK = 2
N_QK_BLK = 4
TQ = 512
TK = 512
TM_MIX = 512
TM_FFN = 1024
TF_DENSE = 256
TF_MOE = 512
CUM_CHUNK = 256
V_ROWS = LANES + 8
N_SPLIT = 3


def _params(sem):
    return pltpu.CompilerParams(dimension_semantics=sem, vmem_limit_bytes=VMEM_LIMIT)


def _rms_scale(x, eps):
    return lax.rsqrt(jnp.mean(x * x, axis=-1, keepdims=True) + eps)


def _split3(v):
    hi = v.astype(BF16)
    r1 = v - hi.astype(F32)
    mid = r1.astype(BF16)
    lo = (r1 - mid.astype(F32)).astype(BF16)
    return hi, mid, lo


def _proj_kernel(x_ref, g_ref, w_ref, wf_ref, fb_ref, sel_ref, cos_ref, sa_ref, sb_ref,
                 qk_ref, vt_ref, c_ref, kx_ref, carry_ref, *, tiles_per_seq):
    i = pl.program_id(0)
    tm = x_ref.shape[0]
    x = x_ref[...]
    xnb = ((x * _rms_scale(x, EPS)) * g_ref[...]).astype(BF16)

    logit = jnp.dot(xnb, wf_ref[...], preferred_element_type=F32) + fb_ref[...]
    logf = jnp.minimum(logit, 0.0) - jnp.log1p(jnp.exp(-jnp.abs(logit)))

    @pl.when(i % tiles_per_seq == 0)
    def _():
        carry_ref[...] = jnp.zeros_like(carry_ref)

    r = lax.broadcasted_iota(jnp.int32, (CUM_CHUNK, CUM_CHUNK), 0)
    cidx = lax.broadcasted_iota(jnp.int32, (CUM_CHUNK, CUM_CHUNK), 1)
    tri = (cidx <= r).astype(BF16)
    carry = carry_ref[...]
    chunks = []
    for ch in range(tm // CUM_CHUNK):
        hi, mid, lo = _split3(logf[ch * CUM_CHUNK:(ch + 1) * CUM_CHUNK])
        cs = (jnp.dot(tri, hi, preferred_element_type=F32)
              + jnp.dot(tri, mid, preferred_element_type=F32)
              + jnp.dot(tri, lo, preferred_element_type=F32)) + carry
        chunks.append(cs * LOG2E)
        carry = cs[CUM_CHUNK - 1:CUM_CHUNK, :]
    carry_ref[...] = carry
    c2 = jnp.concatenate(chunks, axis=0)
    c_ref[...] = c2
    hi, mid, lo = _split3(-c2)
    kx = (jnp.dot(hi, sel_ref[0], preferred_element_type=F32)
          + jnp.dot(mid, sel_ref[1], preferred_element_type=F32)
          + jnp.dot(lo, sel_ref[2], preferred_element_type=F32))
    kx_ref[...] = kx.astype(kx_ref.dtype)

    cos = cos_ref[...]
    sa = sa_ref[...]
    sb = sb_ref[...]
    ones_row = (lax.broadcasted_iota(jnp.int32, (V_ROWS - LANES, TK), 0) == 0).astype(vt_ref.dtype)
    n_blk = w_ref.shape[1] // TN_PROJ
    for j in range(n_blk):
        acc = jnp.dot(xnb, w_ref[:, j * TN_PROJ:(j + 1) * TN_PROJ], preferred_element_type=F32)
        if j < N_ROPE_BLK:
            for s in range(TN_PROJ // LANES):
                t = acc[:, s * LANES:(s + 1) * LANES]
                rot = (t * cos + pltpu.roll(t, ROPE_HALF, 1) * sa
                       + pltpu.roll(t, LANES - ROPE_HALF, 1) * sb)
                qk_ref[:, j * TN_PROJ + s * LANES:j * TN_PROJ + (s + 1) * LANES] = rot.astype(qk_ref.dtype)
        elif j < N_QK_BLK:
            qk_ref[:, j * TN_PROJ:(j + 1) * TN_PROJ] = acc.astype(qk_ref.dtype)
        else:
            at = acc.T
            for hd in range(TN_PROJ // LANES):
                head = (j - N_QK_BLK) * (TN_PROJ // LANES) + hd
                vt_ref[0, head, 0, 0:LANES, :] = at[hd * LANES:(hd + 1) * LANES, :].astype(vt_ref.dtype)
                vt_ref[0, head, 0, LANES:V_ROWS, :] = ones_row


def _proj_call(h, g, w_main, w_f, f_bias, sel, cos_t, sa_t, sb_t, batch, seq):
    t, d = h.shape
    tm = TM_PROJ
    assert tm == TK
    tiles_per_seq = seq // tm
    n_col = w_main.shape[1]
    n_vh = (n_col // TN_PROJ - N_QK_BLK) * (TN_PROJ // LANES)
    kern = functools.partial(_proj_kernel, tiles_per_seq=tiles_per_seq)
    const2 = lambda i: (0, 0)
    return pl.pallas_call(
        kern,
        grid=(t // tm,),
        in_specs=[
            pl.BlockSpec((tm, d), lambda i: (i, 0)),
            pl.BlockSpec((1, d), const2),
            pl.BlockSpec((d, n_col), const2),
            pl.BlockSpec((d, LANES), const2),
            pl.BlockSpec((1, LANES), const2),
            pl.BlockSpec((N_SPLIT, LANES, LANES), lambda i: (0, 0, 0)),
            pl.BlockSpec((tm, LANES), lambda i: (i, 0)),
            pl.BlockSpec((tm, LANES), lambda i: (i, 0)),
            pl.BlockSpec((tm, LANES), lambda i: (i, 0)),
        ],
        out_specs=[
            pl.BlockSpec((tm, N_QK_BLK * TN_PROJ), lambda i: (i, 0)),
            pl.BlockSpec((1, n_vh, 1, V_ROWS, TK), lambda i: (i // tiles_per_seq, 0, i % tiles_per_seq, 0, 0)),
            pl.BlockSpec((tm, LANES), lambda i: (i, 0)),
            pl.BlockSpec((tm, LANES), lambda i: (i, 0)),
        ],
        out_shape=[
            jax.ShapeDtypeStruct((t, N_QK_BLK * TN_PROJ), BF16),
            jax.ShapeDtypeStruct((batch, n_vh, seq // TK, V_ROWS, TK), BF16),
            jax.ShapeDtypeStruct((t, LANES), F32),
            jax.ShapeDtypeStruct((t, LANES), BF16),
        ],
        scratch_shapes=[pltpu.VMEM((1, LANES), F32)],
        compiler_params=_params(("arbitrary",)),
    )(h, g, w_main, w_f, f_bias, sel, cos_t, sa_t, sb_t)


def _head_mask(q, half):
    lane = lax.broadcasted_iota(jnp.int32, q.shape, 1)
    return jnp.where(lane // HEAD_DIM == half, q, jnp.zeros_like(q))


def _causal_mask():
    kpos = lax.broadcasted_iota(jnp.int32, (TK, TQ), 0)
    qpos = lax.broadcasted_iota(jnp.int32, (TK, TQ), 1)
    return kpos <= qpos


def _scores_t(k, q):
    return lax.dot_general(k, q, (((1,), (1,)), ((), ())), preferred_element_type=F32)


def _online_update(u, vt, shift, acc_ref, m_ref):
    mu = jnp.max(u, axis=0, keepdims=True) + shift
    m_old = m_ref[...]
    m_new = jnp.maximum(m_old, mu)
    p = jnp.exp2(u - (m_new - shift)).astype(BF16)
    pv = jnp.dot(vt, p, preferred_element_type=F32)
    acc_ref[...] = jnp.exp2(m_old - m_new) * acc_ref[...] + pv
    m_ref[...] = m_new


def _normalised(acc_ref):
    acc = acc_ref[...]
    return acc[:LANES] * (1.0 / acc[LANES:LANES + 1])


N_TILES = 4


def _attn_scratch():
    return ([pltpu.VMEM((TK, TQ), F32)] * (2 * N_TILES) + [pltpu.VMEM((V_ROWS, TQ), F32)] * N_TILES
            + [pltpu.VMEM((1, TQ), F32)] * N_TILES)


def _attend(qi, qk_fn, vt_fn, shifts, scratch):
    assert TQ == TK
    s_a, s_b = scratch[:N_TILES], scratch[N_TILES:2 * N_TILES]
    acc = scratch[2 * N_TILES:3 * N_TILES]
    m = scratch[3 * N_TILES:]
    for t in range(N_TILES):
        m[t][...] = jnp.full(m[t].shape, -jnp.inf, F32)
        acc[t][...] = jnp.zeros(acc[t].shape, F32)
        s_a[t][...] = qk_fn(0, t)

    def stage(j, cur, nxt, masked):
        for t in range(N_TILES):
            if nxt is not None:
                nxt[t][...] = qk_fn(j + 1, t)
            u = cur[t][...]
            if masked:
                u = jnp.where(_causal_mask(), u, -jnp.inf)
            _online_update(u, vt_fn(j, t), shifts[t], acc[t], m[t])

    def body(i, carry):
        stage(2 * i, s_a, s_b, False)
        stage(2 * i + 1, s_b, s_a, False)
        return carry

    lax.fori_loop(0, qi // 2, body, 0)

    @pl.when(qi % 2 == 0)
    def _():
        stage(qi, s_a, None, True)

    @pl.when(qi % 2 == 1)
    def _():
        stage(qi - 1, s_a, s_b, False)
        stage(qi, s_b, None, True)

    return acc


def _fox_kernel(q_ref, k_ref, kx_ref, vt_ref, ct_ref, o_ref, *scratch):
    grp = pl.program_id(1)
    qi = pl.program_id(2)
    lane = lax.broadcasted_iota(jnp.int32, (TQ, LANES), 1)
    qas, cqs = [], []
    for pp in range(2):
        q = q_ref[0, :, pp * LANES:(pp + 1) * LANES]
        for hh in range(2):
            head = 4 * grp + 2 * pp + hh
            pick = jnp.logical_and(lane >= N_SPLIT * head, lane < N_SPLIT * (head + 1))
            qas.append(jnp.concatenate([_head_mask(q, hh), pick.astype(q.dtype)], axis=1))
            cqs.append(ct_ref[0, pl.ds(head, 1), :])

    def qk_fn(j, t):
        pp = t // 2
        start = pl.multiple_of(j * TK, TK)
        ka = jnp.concatenate([k_ref[0, pl.ds(start, TK), pp * LANES:(pp + 1) * LANES],
                              kx_ref[0, pl.ds(start, TK), :]], axis=1)
        return _scores_t(ka, qas[t])

    def vt_fn(j, t):
        return vt_ref[0, t // 2, j]

    acc = _attend(qi, qk_fn, vt_fn, cqs, scratch)
    for pp in range(2):
        ot = jnp.concatenate([_normalised(acc[2 * pp])[:HEAD_DIM],
                              _normalised(acc[2 * pp + 1])[HEAD_DIM:]], axis=0)
        o_ref[0, :, pp * LANES:(pp + 1) * LANES] = ot.T.astype(o_ref.dtype)


def _fox_call(qk, kx, vt, ct, batch, seq):
    nkb = seq // TK
    w2 = 2 * LANES
    return pl.pallas_call(
        _fox_kernel,
        grid=(batch, 2, seq // TQ),
        in_specs=[
            pl.BlockSpec((1, TQ, w2), lambda b, g, qi: (b, qi, 4 + g)),
            pl.BlockSpec((1, seq, w2), lambda b, g, qi: (b, 0, 6 + g)),
            pl.BlockSpec((1, seq, LANES), lambda b, g, qi: (b, 0, 0)),
            pl.BlockSpec((1, 2, nkb, V_ROWS, TK), lambda b, g, qi: (b, N_DIFF_HEADS // 2 + g, 0, 0, 0)),
            pl.BlockSpec((1, N_FOX_HEADS, TQ), lambda b, g, qi: (b, 0, qi)),
        ],
        out_specs=pl.BlockSpec((1, TQ, w2), lambda b, g, qi: (b, qi, g)),
        out_shape=jax.ShapeDtypeStruct((batch, seq, D_FOX), BF16),
        scratch_shapes=_attn_scratch(),
        compiler_params=_params(("arbitrary", "arbitrary", "arbitrary")),
    )(qk, qk, kx, vt, ct)


def _diff_kernel(lam_ref, q1_ref, q2_ref, k1_ref, k2_ref, vt_ref, g_ref, o_ref, *scratch, out_scale):
    qi = pl.program_id(2)
    q_refs = [q1_ref, q2_ref]
    k_refs = [k1_ref, k2_ref]
    qms = [_head_mask(q_refs[br][0], hh) for hh in range(2) for br in range(2)]

    def qk_fn(j, t):
        start = pl.multiple_of(j * TK, TK)
        return _scores_t(k_refs[t % 2][0, pl.ds(start, TK), :], qms[t])

    def vt_fn(j, t):
        return vt_ref[0, t // 2, j]

    acc = _attend(qi, qk_fn, vt_fn, [0.0] * N_TILES, scratch)
    for hh in range(2):
        ot = _normalised(acc[2 * hh]) - lam_ref[0] * _normalised(acc[2 * hh + 1])
        y = ot * lax.rsqrt(jnp.mean(ot * ot, axis=0, keepdims=True) + SUBLN_EPS)
        y = (y * g_ref[...]) * out_scale
        o_ref[0, :, hh * LANES:(hh + 1) * LANES] = y.T.astype(o_ref.dtype)


def _diff_call(lam, qk, vt, g_col, out_scale, batch, seq):
    nkb = seq // TK
    n_pairs = N_DIFF_HEADS // 2
    kern = functools.partial(_diff_kernel, out_scale=out_scale)
    grid_spec = pltpu.PrefetchScalarGridSpec(
        num_scalar_prefetch=1,
        grid=(batch, n_pairs, seq // TQ),
        in_specs=[
            pl.BlockSpec((1, TQ, LANES), lambda b, p, qi, lam: (b, qi, p)),
            pl.BlockSpec((1, TQ, LANES), lambda b, p, qi, lam: (b, qi, 2 + p)),
            pl.BlockSpec((1, seq, LANES), lambda b, p, qi, lam: (b, 0, 4 + p)),
            pl.BlockSpec((1, seq, LANES), lambda b, p, qi, lam: (b, 0, 6 + p)),
            pl.BlockSpec((1, 2, nkb, V_ROWS, TK), lambda b, p, qi, lam: (b, p, 0, 0, 0)),
            pl.BlockSpec((DIFF_V_DIM, 1), lambda b, p, qi, lam: (0, 0)),
        ],
        out_specs=pl.BlockSpec((1, TQ, 2 * LANES), lambda b, p, qi, lam: (b, qi, p)),
        scratch_shapes=_attn_scratch(),
    )
    return pl.pallas_call(
        kern,
        grid_spec=grid_spec,
        out_shape=jax.ShapeDtypeStruct((batch, seq, D_DIFF_OUT), BF16),
        compiler_params=_params(("arbitrary", "arbitrary", "arbitrary")),
    )(lam, qk, qk, qk, qk, vt, g_col)


def _norm_matmul_kernel(x_ref, g_ref, w_ref, o_ref):
    x = x_ref[...]
    xn = ((x * _rms_scale(x, EPS)) * g_ref[...]).astype(BF16)
    o_ref[...] = jnp.dot(xn, w_ref[...], preferred_element_type=F32).astype(o_ref.dtype)


def _norm_matmul_call(x, g, w, tm, tn):
    t, d = x.shape
    n = w.shape[1]
    return pl.pallas_call(
        _norm_matmul_kernel,
        grid=(t // tm, n // tn),
        in_specs=[pl.BlockSpec((tm, d), lambda i, j: (i, 0)),
                  pl.BlockSpec((1, d), lambda i, j: (0, 0)),
                  pl.BlockSpec((d, tn), lambda i, j: (0, j))],
        out_specs=pl.BlockSpec((tm, tn), lambda i, j: (i, j)),
        out_shape=jax.ShapeDtypeStruct((t, n), BF16),
        compiler_params=_params(("arbitrary", "arbitrary")),
    )(x, g, w)


def _mix_mem_kernel(h_ref, od_ref, of_ref, wod_ref, wof_ref, g_ref, wq_ref, k_ref, v_ref, wo_ref,
                    o_ref, *, mem_scale):
    h1 = (h_ref[...]
          + jnp.dot(od_ref[...], wod_ref[...], preferred_element_type=F32)
          + jnp.dot(of_ref[...], wof_ref[...], preferred_element_type=F32))
    xn = ((h1 * _rms_scale(h1, EPS)) * g_ref[...]).astype(BF16)
    q = jnp.dot(xn, wq_ref[...], preferred_element_type=F32).astype(BF16)
    d = q.shape[1]
    hd = d // MEM_HEADS
    outs = []
    for hh in range(MEM_HEADS):
        qh = q[:, hh * hd:(hh + 1) * hd]
        kh = k_ref[:, hh * hd:(hh + 1) * hd]
        vh = v_ref[:, hh * hd:(hh + 1) * hd]
        s = lax.dot_general(qh, kh, (((1,), (1,)), ((), ())),
                            preferred_element_type=F32) * mem_scale
        m = jnp.max(s, axis=-1, keepdims=True)
        e = jnp.exp(s - m)
        p = e * (1.0 / jnp.sum(e, axis=-1, keepdims=True))
        outs.append(jnp.dot(p.astype(BF16), vh, preferred_element_type=F32).astype(BF16))
    o = jnp.concatenate(outs, axis=1)
    o_ref[...] = h1 + jnp.dot(o, wo_ref[...], preferred_element_type=F32)


def _mix_mem_call(h, o_d, o_f, w_od, w_of, g, w_q, kv, w_o, seq, n_mem):
    t, d = h.shape
    tm = TM_MIX
    tiles_per_seq = seq // tm
    kern = functools.partial(_mix_mem_kernel, mem_scale=1.0 / math.sqrt(d // MEM_HEADS))
    const = lambda i: (0, 0)
    return pl.pallas_call(
        kern,
        grid=(t // tm,),
        in_specs=[
            pl.BlockSpec((tm, d), lambda i: (i, 0)),
            pl.BlockSpec((tm, D_DIFF_OUT), lambda i: (i, 0)),
            pl.BlockSpec((tm, D_FOX), lambda i: (i, 0)),
            pl.BlockSpec((D_DIFF_OUT, d), const),
            pl.BlockSpec((D_FOX, d), const),
            pl.BlockSpec((1, d), const),
            pl.BlockSpec((d, d), const),
            pl.BlockSpec((n_mem, d), lambda i: (i // tiles_per_seq, 0)),
            pl.BlockSpec((n_mem, d), lambda i: (i // tiles_per_seq, 1)),
            pl.BlockSpec((d, d), const),
        ],
        out_specs=pl.BlockSpec((tm, d), lambda i: (i, 0)),
        out_shape=jax.ShapeDtypeStruct((t, d), F32),
        compiler_params=_params(("arbitrary",)),
    )(h, o_d, o_f, w_od, w_of, g, w_q, kv, kv, w_o)


def _swiglu_chunk(x, w1, w3, w2):
    a = jnp.dot(x, w1, preferred_element_type=F32)
    b = jnp.dot(x, w3, preferred_element_type=F32)
    mid = (a * (1.0 / (1.0 + jnp.exp(-a)))) * b
    return jnp.dot(mid.astype(BF16), w2, preferred_element_type=F32)


def _dense_ffn_kernel(h_ref, g_ref, w1_ref, w3_ref, w2_ref, o_ref, xn_ref):
    f = pl.program_id(1)

    @pl.when(f == 0)
    def _():
        x = h_ref[...]
        xn_ref[...] = ((x * _rms_scale(x, EPS)) * g_ref[...]).astype(BF16)
        o_ref[...] = x

    o_ref[...] += _swiglu_chunk(xn_ref[...], w1_ref[...], w3_ref[...], w2_ref[...])


def _dense_ffn_call(h, g, w1, w3, w2):
    t, d = h.shape
    tm, tf = TM_FFN, TF_DENSE
    dff = w1.shape[1]
    return pl.pallas_call(
        _dense_ffn_kernel,
        grid=(t // tm, dff // tf),
        in_specs=[pl.BlockSpec((tm, d), lambda i, f: (i, 0)),
                  pl.BlockSpec((1, d), lambda i, f: (0, 0)),
                  pl.BlockSpec((d, tf), lambda i, f: (0, f)),
                  pl.BlockSpec((d, tf), lambda i, f: (0, f)),
                  pl.BlockSpec((tf, d), lambda i, f: (f, 0))],
        out_specs=pl.BlockSpec((tm, d), lambda i, f: (i, 0)),
        out_shape=jax.ShapeDtypeStruct((t, d), F32),
        scratch_shapes=[pltpu.VMEM((tm, d), BF16)],
        compiler_params=_params(("arbitrary", "arbitrary")),
    )(h, g, w1, w3, w2)


def _router_kernel(h_ref, g_ref, wr_ref, hn_ref, idx_ref, gate_ref):
    x = h_ref[...]
    xn = ((x * _rms_scale(x, EPS)) * g_ref[...]).astype(BF16)
    hn_ref[...] = xn
    logits = jnp.dot(xn, wr_ref[...], preferred_element_type=F32)
    lane = lax.broadcasted_iota(jnp.int32, logits.shape, 1)
    logits = jnp.where(lane < N_EXPERTS, logits, -jnp.inf)
    v1 = jnp.max(logits, axis=-1, keepdims=True)
    i1 = jnp.min(jnp.where(logits == v1, lane, LANES), axis=-1, keepdims=True)
    rest = jnp.where(lane == i1, -jnp.inf, logits)
    v2 = jnp.max(rest, axis=-1, keepdims=True)
    i2 = jnp.min(jnp.where(rest == v2, lane, LANES), axis=-1, keepdims=True)
    e = jnp.exp(v2 - v1)
    inv = 1.0 / (1.0 + e)
    idx_ref[...] = jnp.where(lane == 0, i1, jnp.where(lane == 1, i2, 0))
    gate_ref[...] = jnp.where(lane == 0, inv, jnp.where(lane == 1, e * inv, 0.0))


def _router_call(h, g, w_r):
    t, d = h.shape
    tm = TM_FFN
    return pl.pallas_call(
        _router_kernel,
        grid=(t // tm,),
        in_specs=[pl.BlockSpec((tm, d), lambda i: (i, 0)),
                  pl.BlockSpec((1, d), lambda i: (0, 0)),
                  pl.BlockSpec((d, LANES), lambda i: (0, 0))],
        out_specs=[pl.BlockSpec((tm, d), lambda i: (i, 0)),
                   pl.BlockSpec((tm, LANES), lambda i: (i, 0)),
                   pl.BlockSpec((tm, LANES), lambda i: (i, 0))],
        out_shape=[jax.ShapeDtypeStruct((t, d), BF16),
                   jax.ShapeDtypeStruct((t, LANES), jnp.int32),
                   jax.ShapeDtypeStruct((t, LANES), F32)],
        compiler_params=_params(("arbitrary",)),
    )(h, g, w_r)


def _expert_ffn_kernel(te_ref, nu_ref, x_ref, w1_ref, w3_ref, w2_ref, o_ref, acc_ref, *, n_chunks):
    i = pl.program_id(0)
    f = pl.program_id(1)

    @pl.when(i < nu_ref[0])
    def _():
        @pl.when(f == 0)
        def _():
            acc_ref[...] = jnp.zeros_like(acc_ref)

        acc_ref[...] += _swiglu_chunk(x_ref[...], w1_ref[0], w3_ref[0], w2_ref[0])

        @pl.when(f == n_chunks - 1)
        def _():
            o_ref[...] = acc_ref[...].astype(o_ref.dtype)


def _expert_ffn_call(tile_expert, n_used, x_sorted, w1, w3, w2):
    p_rows, d = x_sorted.shape
    tm, tf = TM_FFN, TF_MOE
    dff = w1.shape[2]
    n_chunks = dff // tf
    kern = functools.partial(_expert_ffn_kernel, n_chunks=n_chunks)

    def row_map(i, f, te, nu):
        return (jnp.minimum(i, nu[0] - 1), 0)

    def chunk(i, f, nu):
        return jnp.where(i < nu[0], f, n_chunks - 1)

    grid_spec = pltpu.PrefetchScalarGridSpec(
        num_scalar_prefetch=2,
        grid=(p_rows // tm, n_chunks),
        in_specs=[
            pl.BlockSpec((tm, d), row_map),
            pl.BlockSpec((1, d, tf), lambda i, f, te, nu: (te[i], 0, chunk(i, f, nu))),
            pl.BlockSpec((1, d, tf), lambda i, f, te, nu: (te[i], 0, chunk(i, f, nu))),
            pl.BlockSpec((1, tf, d), lambda i, f, te, nu: (te[i], chunk(i, f, nu), 0)),
        ],
        out_specs=pl.BlockSpec((tm, d), row_map),
        scratch_shapes=[pltpu.VMEM((tm, d), F32)],
    )
    return pl.pallas_call(
        kern,
        grid_spec=grid_spec,
        out_shape=jax.ShapeDtypeStruct((p_rows, d), BF16),
        compiler_params=_params(("arbitrary", "arbitrary")),
    )(tile_expert, n_used, x_sorted, w1, w3, w2)


def _route(idx, tm):
    t = idx.shape[0]
    n_assign = 2 * t
    p_rows = n_assign + N_EXPERTS * tm
    e_flat = idx.reshape(-1)
    onehot = (e_flat[:, None] == jnp.arange(N_EXPERTS, dtype=jnp.int32)[None, :]).astype(jnp.int32)
    running = jnp.cumsum(onehot, axis=0)
    counts = running[-1]
    rank = jnp.sum(running * onehot, axis=1) - 1
    padded = ((counts + tm - 1) // tm) * tm
    pend = jnp.cumsum(padded)
    pstart = pend - padded
    gstart = jnp.cumsum(counts) - counts
    dest = pstart[e_flat] + rank
    n_tiles = p_rows // tm
    n_used = (pend[-1] // tm).astype(jnp.int32)
    tile_start = jnp.arange(n_tiles, dtype=jnp.int32) * tm
    tile_expert = jnp.sum((tile_start[:, None] >= pend[None, :]).astype(jnp.int32), axis=1)
    last_expert = jnp.sum((((n_used - 1) * tm) >= pend).astype(jnp.int32))
    tile_expert = jnp.where(tile_start < pend[-1], tile_expert, last_expert).astype(jnp.int32)
    order = jnp.sort(e_flat * n_assign + jnp.arange(n_assign, dtype=jnp.int32)) % n_assign
    row_expert = jnp.repeat(tile_expert, tm)
    within = jnp.arange(p_rows, dtype=jnp.int32) - pstart[row_expert]
    src = jnp.clip(gstart[row_expert] + within, 0, n_assign - 1)
    row_token = jnp.where(within < counts[row_expert], order[src] // 2, 0)
    return row_token, dest.reshape(t, 2), tile_expert, n_used.reshape(1)


def _moe_ffn(h, g, w_r, w1, w3, w2):
    hn, idx_l, gate_l = _router_call(h, g, w_r)
    row_token, slot, tile_expert, n_used = _route(idx_l[:, :2], TM_FFN)
    x_sorted = jnp.take(hn, row_token, axis=0)
    y = _expert_ffn_call(tile_expert, n_used, x_sorted, w1, w3, w2)
    return (h + gate_l[:, 0:1] * jnp.take(y, slot[:, 0], axis=0).astype(F32)
            + gate_l[:, 1:2] * jnp.take(y, slot[:, 1], axis=0).astype(F32))


def _final_norm_kernel(x_ref, g_ref, o_ref):
    x = x_ref[...]
    o_ref[...] = (x * _rms_scale(x, EPS)) * g_ref[...]


def _final_norm_call(h, g):
    t, d = h.shape
    tm = TM_FFN
    return pl.pallas_call(
        _final_norm_kernel,
        grid=(t // tm,),
        in_specs=[pl.BlockSpec((tm, d), lambda i: (i, 0)), pl.BlockSpec((1, d), lambda i: (0, 0))],
        out_specs=pl.BlockSpec((tm, d), lambda i: (i, 0)),
        out_shape=jax.ShapeDtypeStruct((t, d), F32),
        compiler_params=_params(("arbitrary",)),
    )(h, g)


def _rope_tables(positions):
    inv_freq = ROPE_THETA ** (-jnp.arange(0, ROPE_DIM, 2, dtype=F32) / ROPE_DIM)
    ang = positions.astype(F32)[..., None] * inv_freq
    cos, sin = jnp.cos(ang), jnp.sin(ang)
    b, s = positions.shape
    ones = jnp.ones((b, s, HEAD_DIM - ROPE_DIM), F32)
    zeros_h = jnp.zeros((b, s, ROPE_HALF), F32)
    zeros_r = jnp.zeros((b, s, HEAD_DIM - ROPE_DIM), F32)
    cos_h = jnp.concatenate([cos, cos, ones], axis=-1)
    sa_h = jnp.concatenate([zeros_h, sin, zeros_r], axis=-1)
    sb_h = jnp.concatenate([-sin, zeros_h, zeros_r], axis=-1)
    pair = lambda a: jnp.concatenate([a, a], axis=-1).reshape(b * s, LANES)
    return pair(cos_h), pair(sa_h), pair(sb_h)


def _decay_selectors():
    src = jnp.arange(LANES, dtype=jnp.int32)[:, None]
    dst = jnp.arange(LANES, dtype=jnp.int32)[None, :]
    sels = []
    for s in range(N_SPLIT):
        hit = jnp.logical_and(src < N_FOX_HEADS, dst == N_SPLIT * src + s)
        sels.append(hit.astype(BF16))
    return jnp.stack(sels)


def _split_w_in(w):
    q1, q2, k1, k2 = (w[:, n * D_DIFF_QK:(n + 1) * D_DIFF_QK] for n in range(4))
    o = 4 * D_DIFF_QK
    v_d = w[:, o:o + D_DIFF_OUT]
    o += D_DIFF_OUT
    q_f, k_f, v_f = (w[:, o + n * D_FOX:o + (n + 1) * D_FOX] for n in range(3))
    o += 3 * D_FOX
    w_f = w[:, o:o + N_FOX_HEADS]
    qs = ATTN_SCALE * LOG2E
    main = jnp.concatenate([q1 * qs, q2 * qs, k1, k2, q_f * qs, k_f, v_d, v_f], axis=1).astype(BF16)
    w_f = jnp.pad(w_f, ((0, 0), (0, LANES - N_FOX_HEADS))).astype(BF16)
    return main, w_f


def kernel(x, mem, positions, norm_mix_g, w_in, lam_q1, lam_k1, lam_q2, lam_k2, diff_subln_g, fox_bias, w_out, norm_mem_g, mem_norm_g, w_mq, w_mkv, w_mo, norm_ffn_g, w_ff1, w_ff3, w_ff2, w_router, w_e1, w_e3, w_e2, final_norm_g):
    batch, seq, d = x.shape
    n_mem = mem.shape[1]
    depth = w_in.shape[0]
    t = batch * seq
    cos_t, sa_t, sb_t = _rope_tables(positions)
    sel = _decay_selectors()
    h = x.reshape(t, d)
    mem2 = mem.reshape(batch * n_mem, d)
    row = lambda v: v.reshape(1, -1).astype(F32)

    for l in range(depth):
        lam_init = 0.8 - 0.6 * math.exp(-0.3 * l)
        lam = (jnp.exp(jnp.sum(lam_q1[l] * lam_k1[l])) - jnp.exp(jnp.sum(lam_q2[l] * lam_k2[l]))
               + lam_init).reshape(1).astype(F32)
        w_main, w_f = _split_w_in(w_in[l])
        f_bias = jnp.pad(fox_bias[l], (0, LANES - N_FOX_HEADS)).reshape(1, LANES)

        qk, vt, c, kx = _proj_call(h, row(norm_mix_g[l]), w_main, w_f, f_bias, sel, cos_t, sa_t, sb_t,
                                   batch, seq)
        qk3 = qk.reshape(batch, seq, qk.shape[1])
        ct = jnp.transpose(c.reshape(batch, seq, LANES)[:, :, :N_FOX_HEADS], (0, 2, 1))
        o_d = _diff_call(lam, qk3, vt, diff_subln_g[l].reshape(DIFF_V_DIM, 1), 1.0 - lam_init, batch, seq)
        o_f = _fox_call(qk3, kx.reshape(batch, seq, LANES), vt, ct, batch, seq)

        w_o = w_out[l].astype(BF16)
        kv = _norm_matmul_call(mem2, row(mem_norm_g[l]), w_mkv[l].astype(BF16), 512, 1024)
        h = _mix_mem_call(h, o_d.reshape(t, D_DIFF_OUT), o_f.reshape(t, D_FOX),
                          w_o[:D_DIFF_OUT], w_o[D_DIFF_OUT:], row(norm_mem_g[l]),
                          w_mq[l].astype(BF16), kv, w_mo[l].astype(BF16), seq, n_mem)

        i = l // 2
        if l % 2 == 0:
            h = _dense_ffn_call(h, row(norm_ffn_g[l]), w_ff1[i].astype(BF16), w_ff3[i].astype(BF16),
                                w_ff2[i].astype(BF16))
        else:
            w_r = jnp.pad(w_router[i], ((0, 0), (0, LANES - N_EXPERTS))).astype(BF16)
            h = _moe_ffn(h, row(norm_ffn_g[l]), w_r, w_e1[i].astype(BF16), w_e3[i].astype(BF16),
                         w_e2[i].astype(BF16))

    return _final_norm_call(h, row(final_norm_g)).reshape(batch, seq, d)
```

```python
import functools
import math

import jax
import jax.numpy as jnp
from jax import lax
from jax.experimental import pallas as pl
from jax.experimental.pallas import tpu as pltpu

F32 = jnp.float32
BF16 = jnp.bfloat16

HEAD_DIM = 64
N_DIFF_HEADS = 4
DIFF_V_DIM = 128
N_FOX_HEADS = 8
D_DIFF_QK = N_DIFF_HEADS * HEAD_DIM
D_DIFF_OUT = N_DIFF_HEADS * DIFF_V_DIM
D_FOX = N_FOX_HEADS * HEAD_DIM
ATTN_SCALE = 1.0 / math.sqrt(HEAD_DIM)
ROPE_DIM = HEAD_DIM // 4
ROPE_HALF = ROPE_DIM // 2
ROPE_THETA = 500000.0
MEM_HEADS = 4
N_EXPERTS = 8
EPS = 1e-6
SUBLN_EPS = 1e-5
LOG2E = 1.0 / math.log(2.0)

LANES = 128
VMEM_LIMIT = 48 * 1024 * 1024

TM_PROJ = 512
TN_PROJ = 512
N_QK_BLK = 4
TQ = 512
TK = 512
TM_MIX = 512
TM_FFN = 1024
TF_DENSE = 256
TF_MOE = 512
CUM_CHUNK = 256
V_ROWS = LANES + 8
N_SPLIT = 3


def _params(sem):
    return pltpu.CompilerParams(dimension_semantics=sem, vmem_limit_bytes=VMEM_LIMIT)


def _rms_scale(x, eps):
    return lax.rsqrt(jnp.mean(x * x, axis=-1, keepdims=True) + eps)


def _split3(v):
    hi = v.astype(BF16)
    r1 = v - hi.astype(F32)
    mid = r1.astype(BF16)
    lo = (r1 - mid.astype(F32)).astype(BF16)
    return hi, mid, lo


O_VD = 4 * D_DIFF_QK
O_QF = O_VD + D_DIFF_OUT
O_KF = O_QF + D_FOX
O_VF = O_KF + D_FOX
O_F = O_VF + D_FOX
Q_SCALE = ATTN_SCALE * LOG2E
_PROJ_BLOCKS = (
    (0, "rope", 0, Q_SCALE),
    (2 * D_DIFF_QK, "rope", TN_PROJ, 1.0),
    (O_QF, "plain", 2 * TN_PROJ, Q_SCALE),
    (O_KF, "plain", 3 * TN_PROJ, 1.0),
    (O_VD, "vt", 0, 1.0),
    (O_VF, "vt", N_DIFF_HEADS, 1.0),
)


def _proj_kernel(x_ref, g_ref, w_ref, wf_ref, fb_ref, sel_ref, cos_ref, sa_ref, sb_ref,
                 qk_ref, vt_ref, c_ref, kx_ref, carry_ref, *, tiles_per_seq):
    i = pl.program_id(0)
    tm = x_ref.shape[0]
    x = x_ref[...]
    xnb = ((x * _rms_scale(x, EPS)) * g_ref[...]).astype(BF16)

    logit = jnp.dot(xnb, wf_ref[...], preferred_element_type=F32) + fb_ref[...]
    logf = jnp.minimum(logit, 0.0) - jnp.log1p(jnp.exp(-jnp.abs(logit)))

    @pl.when(i % tiles_per_seq == 0)
    def _():
        carry_ref[...] = jnp.zeros_like(carry_ref)

    r = lax.broadcasted_iota(jnp.int32, (CUM_CHUNK, CUM_CHUNK), 0)
    cidx = lax.broadcasted_iota(jnp.int32, (CUM_CHUNK, CUM_CHUNK), 1)
    tri = (cidx <= r).astype(BF16)
    carry = carry_ref[...]
    chunks = []
    for ch in range(tm // CUM_CHUNK):
        hi, mid, lo = _split3(logf[ch * CUM_CHUNK:(ch + 1) * CUM_CHUNK])
        cs = (jnp.dot(tri, hi, preferred_element_type=F32)
              + jnp.dot(tri, mid, preferred_element_type=F32)
              + jnp.dot(tri, lo, preferred_element_type=F32)) + carry
        chunks.append(cs * LOG2E)
        carry = cs[CUM_CHUNK - 1:CUM_CHUNK, :]
    carry_ref[...] = carry
    c2 = jnp.concatenate(chunks, axis=0)
    c_ref[...] = c2
    hi, mid, lo = _split3(-c2)
    kx = (jnp.dot(hi, sel_ref[0], preferred_element_type=F32)
          + jnp.dot(mid, sel_ref[1], preferred_element_type=F32)
          + jnp.dot(lo, sel_ref[2], preferred_element_type=F32))
    kx_ref[...] = kx.astype(kx_ref.dtype)

    cos = cos_ref[...]
    sa = sa_ref[...]
    sb = sb_ref[...]
    ones_row = (lax.broadcasted_iota(jnp.int32, (V_ROWS - LANES, TK), 0) == 0).astype(vt_ref.dtype)
    for src, kind, dst, scale in _PROJ_BLOCKS:
        acc = jnp.dot(xnb, w_ref[:, src:src + TN_PROJ], preferred_element_type=F32)
        if scale != 1.0:
            acc = acc * scale
        if kind == "rope":
            for s in range(TN_PROJ // LANES):
                t = acc[:, s * LANES:(s + 1) * LANES]
                rot = (t * cos + pltpu.roll(t, ROPE_HALF, 1) * sa
                       + pltpu.roll(t, LANES - ROPE_HALF, 1) * sb)
                qk_ref[:, dst + s * LANES:dst + (s + 1) * LANES] = rot.astype(qk_ref.dtype)
        elif kind == "plain":
            qk_ref[:, dst:dst + TN_PROJ] = acc.astype(qk_ref.dtype)
        else:
            at = acc.T
            for hd in range(TN_PROJ // LANES):
                vt_ref[0, dst + hd, 0, 0:LANES, :] = at[hd * LANES:(hd + 1) * LANES, :].astype(vt_ref.dtype)
                vt_ref[0, dst + hd, 0, LANES:V_ROWS, :] = ones_row


def _proj_call(h, g, w_main, w_f, f_bias, sel, cos_t, sa_t, sb_t, batch, seq):
    t, d = h.shape
    tm = TM_PROJ
    assert tm == TK
    tiles_per_seq = seq // tm
    n_col = w_main.shape[1]
    n_vh = (len(_PROJ_BLOCKS) - N_QK_BLK) * (TN_PROJ // LANES)
    kern = functools.partial(_proj_kernel, tiles_per_seq=tiles_per_seq)
    const2 = lambda i: (0, 0)
    return pl.pallas_call(
        kern,
        grid=(t // tm,),
        in_specs=[
            pl.BlockSpec((tm, d), lambda i: (i, 0)),
            pl.BlockSpec((1, d), const2),
            pl.BlockSpec((d, n_col), const2),
            pl.BlockSpec((d, LANES), const2),
            pl.BlockSpec((1, LANES), const2),
            pl.BlockSpec((N_SPLIT, LANES, LANES), lambda i: (0, 0, 0)),
            pl.BlockSpec((tm, LANES), lambda i: (i, 0)),
            pl.BlockSpec((tm, LANES), lambda i: (i, 0)),
            pl.BlockSpec((tm, LANES), lambda i: (i, 0)),
        ],
        out_specs=[
            pl.BlockSpec((tm, N_QK_BLK * TN_PROJ), lambda i: (i, 0)),
            pl.BlockSpec((1, n_vh, 1, V_ROWS, TK), lambda i: (i // tiles_per_seq, 0, i % tiles_per_seq, 0, 0)),
            pl.BlockSpec((tm, LANES), lambda i: (i, 0)),
            pl.BlockSpec((tm, LANES), lambda i: (i, 0)),
        ],
        out_shape=[
            jax.ShapeDtypeStruct((t, N_QK_BLK * TN_PROJ), BF16),
            jax.ShapeDtypeStruct((batch, n_vh, seq // TK, V_ROWS, TK), BF16),
            jax.ShapeDtypeStruct((t, LANES), F32),
            jax.ShapeDtypeStruct((t, LANES), BF16),
        ],
        scratch_shapes=[pltpu.VMEM((1, LANES), F32)],
        compiler_params=_params(("arbitrary",)),
    )(h, g, w_main, w_f, f_bias, sel, cos_t, sa_t, sb_t)


def _head_mask(q, half):
    lane = lax.broadcasted_iota(jnp.int32, q.shape, 1)
    return jnp.where(lane // HEAD_DIM == half, q, jnp.zeros_like(q))


def _causal_mask():
    kpos = lax.broadcasted_iota(jnp.int32, (TK, TQ), 0)
    qpos = lax.broadcasted_iota(jnp.int32, (TK, TQ), 1)
    return kpos <= qpos


def _scores_t(k, q):
    return lax.dot_general(k, q, (((1,), (1,)), ((), ())), preferred_element_type=F32)


def _online_update(u, vt, shift, acc_ref, m_ref):
    mu = jnp.max(u, axis=0, keepdims=True) + shift
    m_old = m_ref[...]
    m_new = jnp.maximum(m_old, mu)
    p = jnp.exp2(u - (m_new - shift)).astype(BF16)
    pv = jnp.dot(vt, p, preferred_element_type=F32)
    acc_ref[...] = jnp.exp2(m_old - m_new) * acc_ref[...] + pv
    m_ref[...] = m_new


def _normalised(acc_ref):
    acc = acc_ref[...]
    return acc[:LANES] * (1.0 / acc[LANES:LANES + 1])


N_TILES = 4


def _attn_scratch():
    return ([pltpu.VMEM((TK, TQ), F32)] * (3 * N_TILES) + [pltpu.VMEM((V_ROWS, TQ), F32)] * N_TILES
            + [pltpu.VMEM((1, TQ), F32)] * N_TILES)


def _attend(qi, qk_fn, qk_next_fn, vt_fn, shifts, scratch):
    assert TQ == TK
    s_a, s_b, s_c = (scratch[n * N_TILES:(n + 1) * N_TILES] for n in range(3))
    acc = scratch[3 * N_TILES:4 * N_TILES]
    m = scratch[4 * N_TILES:]
    for t in range(N_TILES):
        m[t][...] = jnp.full(m[t].shape, -jnp.inf, F32)
        acc[t][...] = jnp.zeros(acc[t].shape, F32)

    def stage(j, cur, nxt, masked):
        for t in range(N_TILES):
            nxt[t][...] = qk_next_fn(t) if masked else qk_fn(j + 1, t)
            u = cur[t][...]
            if masked:
                u = jnp.where(_causal_mask(), u, -jnp.inf)
            _online_update(u, vt_fn(j, t), shifts[t], acc[t], m[t])

    @pl.when(qi == 0)
    def _():
        for t in range(N_TILES):
            s_a[t][...] = qk_fn(0, t)
        stage(0, s_a, s_c, True)

    @pl.when(qi > 0)
    def _():
        stage(0, s_c, s_a, False)

        def body(i, carry):
            stage(2 * i + 1, s_a, s_b, False)
            stage(2 * i + 2, s_b, s_a, False)
            return carry

        lax.fori_loop(0, (qi - 1) // 2, body, 0)

        @pl.when(qi % 2 == 1)
        def _():
            stage(qi, s_a, s_c, True)

        @pl.when(qi % 2 == 0)
        def _():
            stage(qi - 1, s_a, s_b, False)
            stage(qi, s_b, s_c, True)

    return acc


def _fox_kernel(q_ref, qn_ref, k_ref, kx_ref, vt_ref, ct_ref, o_ref, *scratch):
    grp = pl.program_id(1)
    qi = pl.program_id(2)
    lane = lax.broadcasted_iota(jnp.int32, (TQ, LANES), 1)
    qas, qas_next, cqs = [], [], []
    for pp in range(2):
        for hh in range(2):
            head = 4 * grp + 2 * pp + hh
            pick = jnp.logical_and(lane >= N_SPLIT * head, lane < N_SPLIT * (head + 1)).astype(BF16)
            for src, dst in ((q_ref, qas), (qn_ref, qas_next)):
                q = src[0, :, pp * LANES:(pp + 1) * LANES]
                dst.append(jnp.concatenate([_head_mask(q, hh), pick], axis=1))
            cqs.append(ct_ref[0, pl.ds(head, 1), :])

    def scores(j, t, qa):
        pp = t // 2
        start = pl.multiple_of(j * TK, TK)
        ka = jnp.concatenate([k_ref[0, pl.ds(start, TK), pp * LANES:(pp + 1) * LANES],
                              kx_ref[0, pl.ds(start, TK), :]], axis=1)
        return _scores_t(ka, qa[t])

    def vt_fn(j, t):
        return vt_ref[0, t // 2, j]

    acc = _attend(qi, lambda j, t: scores(j, t, qas), lambda t: scores(0, t, qas_next), vt_fn, cqs, scratch)
    for pp in range(2):
        ot = jnp.concatenate([_normalised(acc[2 * pp])[:HEAD_DIM],
                              _normalised(acc[2 * pp + 1])[HEAD_DIM:]], axis=0)
        o_ref[0, :, pp * LANES:(pp + 1) * LANES] = ot.T.astype(o_ref.dtype)


def _fox_call(qk, kx, vt, ct, batch, seq):
    nkb = seq // TK
    nq = seq // TQ
    w2 = 2 * LANES
    return pl.pallas_call(
        _fox_kernel,
        grid=(batch, 2, nq),
        in_specs=[
            pl.BlockSpec((1, TQ, w2), lambda b, g, qi: (b, qi, 4 + g)),
            pl.BlockSpec((1, TQ, w2), lambda b, g, qi: (b, jnp.minimum(qi + 1, nq - 1), 4 + g)),
            pl.BlockSpec((1, seq, w2), lambda b, g, qi: (b, 0, 6 + g)),
            pl.BlockSpec((1, seq, LANES), lambda b, g, qi: (b, 0, 0)),
            pl.BlockSpec((1, 2, nkb, V_ROWS, TK), lambda b, g, qi: (b, N_DIFF_HEADS // 2 + g, 0, 0, 0)),
            pl.BlockSpec((1, N_FOX_HEADS, TQ), lambda b, g, qi: (b, 0, qi)),
        ],
        out_specs=pl.BlockSpec((1, TQ, w2), lambda b, g, qi: (b, qi, g)),
        out_shape=jax.ShapeDtypeStruct((batch, seq, D_FOX), BF16),
        scratch_shapes=_attn_scratch(),
        compiler_params=_params(("arbitrary", "arbitrary", "arbitrary")),
    )(qk, qk, qk, kx, vt, ct)


def _diff_kernel(lam_ref, q1_ref, q2_ref, q1n_ref, q2n_ref, k1_ref, k2_ref, vt_ref, g_ref, o_ref, *scratch,
                 out_scale):
    qi = pl.program_id(2)
    k_refs = [k1_ref, k2_ref]
    qms = [_head_mask(r[0], hh) for hh in range(2) for r in (q1_ref, q2_ref)]
    qms_next = [_head_mask(r[0], hh) for hh in range(2) for r in (q1n_ref, q2n_ref)]

    def scores(j, t, qm):
        start = pl.multiple_of(j * TK, TK)
        return _scores_t(k_refs[t % 2][0, pl.ds(start, TK), :], qm[t])

    def vt_fn(j, t):
        return vt_ref[0, t // 2, j]

    acc = _attend(qi, lambda j, t: scores(j, t, qms), lambda t: scores(0, t, qms_next), vt_fn,
                  [0.0] * N_TILES, scratch)
    for hh in range(2):
        ot = _normalised(acc[2 * hh]) - lam_ref[0] * _normalised(acc[2 * hh + 1])
        y = ot * lax.rsqrt(jnp.mean(ot * ot, axis=0, keepdims=True) + SUBLN_EPS)
        y = (y * g_ref[...]) * out_scale
        o_ref[0, :, hh * LANES:(hh + 1) * LANES] = y.T.astype(o_ref.dtype)


def _diff_call(lam, qk, vt, g_col, out_scale, batch, seq):
    nkb = seq // TK
    nq = seq // TQ
    n_pairs = N_DIFF_HEADS // 2
    kern = functools.partial(_diff_kernel, out_scale=out_scale)
    grid_spec = pltpu.PrefetchScalarGridSpec(
        num_scalar_prefetch=1,
        grid=(batch, n_pairs, nq),
        in_specs=[
            pl.BlockSpec((1, TQ, LANES), lambda b, p, qi, lam: (b, qi, p)),
            pl.BlockSpec((1, TQ, LANES), lambda b, p, qi, lam: (b, qi, 2 + p)),
            pl.BlockSpec((1, TQ, LANES), lambda b, p, qi, lam: (b, jnp.minimum(qi + 1, nq - 1), p)),
            pl.BlockSpec((1, TQ, LANES), lambda b, p, qi, lam: (b, jnp.minimum(qi + 1, nq - 1), 2 + p)),
            pl.BlockSpec((1, seq, LANES), lambda b, p, qi, lam: (b, 0, 4 + p)),
            pl.BlockSpec((1, seq, LANES), lambda b, p, qi, lam: (b, 0, 6 + p)),
            pl.BlockSpec((1, 2, nkb, V_ROWS, TK), lambda b, p, qi, lam: (b, p, 0, 0, 0)),
            pl.BlockSpec((DIFF_V_DIM, 1), lambda b, p, qi, lam: (0, 0)),
        ],
        out_specs=pl.BlockSpec((1, TQ, 2 * LANES), lambda b, p, qi, lam: (b, qi, p)),
        scratch_shapes=_attn_scratch(),
    )
    return pl.pallas_call(
        kern,
        grid_spec=grid_spec,
        out_shape=jax.ShapeDtypeStruct((batch, seq, D_DIFF_OUT), BF16),
        compiler_params=_params(("arbitrary", "arbitrary", "arbitrary")),
    )(lam, qk, qk, qk, qk, qk, qk, vt, g_col)


def _norm_matmul_kernel(x_ref, g_ref, w_ref, o_ref):
    x = x_ref[...]
    xn = ((x * _rms_scale(x, EPS)) * g_ref[...]).astype(BF16)
    o_ref[...] = jnp.dot(xn, w_ref[...], preferred_element_type=F32).astype(o_ref.dtype)


def _norm_matmul_call(x, g, w, tm, tn):
    t, d = x.shape
    n = w.shape[1]
    return pl.pallas_call(
        _norm_matmul_kernel,
        grid=(t // tm, n // tn),
        in_specs=[pl.BlockSpec((tm, d), lambda i, j: (i, 0)),
                  pl.BlockSpec((1, d), lambda i, j: (0, 0)),
                  pl.BlockSpec((d, tn), lambda i, j: (0, j))],
        out_specs=pl.BlockSpec((tm, tn), lambda i, j: (i, j)),
        out_shape=jax.ShapeDtypeStruct((t, n), BF16),
        compiler_params=_params(("arbitrary", "arbitrary")),
    )(x, g, w)


def _mix_mem_kernel(h_ref, od_ref, of_ref, wod_ref, wof_ref, g_ref, wq_ref, k_ref, v_ref, wo_ref,
                    o_ref, *, mem_scale):
    h1 = (h_ref[...]
          + jnp.dot(od_ref[...], wod_ref[...], preferred_element_type=F32)
          + jnp.dot(of_ref[...], wof_ref[...], preferred_element_type=F32))
    xn = ((h1 * _rms_scale(h1, EPS)) * g_ref[...]).astype(BF16)
    q = jnp.dot(xn, wq_ref[...], preferred_element_type=F32).astype(BF16)
    d = q.shape[1]
    hd = d // MEM_HEADS
    outs = []
    for hh in range(MEM_HEADS):
        qh = q[:, hh * hd:(hh + 1) * hd]
        kh = k_ref[:, hh * hd:(hh + 1) * hd]
        vh = v_ref[:, hh * hd:(hh + 1) * hd]
        s = lax.dot_general(qh, kh, (((1,), (1,)), ((), ())),
                            preferred_element_type=F32) * mem_scale
        m = jnp.max(s, axis=-1, keepdims=True)
        e = jnp.exp(s - m)
        p = e * (1.0 / jnp.sum(e, axis=-1, keepdims=True))
        outs.append(jnp.dot(p.astype(BF16), vh, preferred_element_type=F32).astype(BF16))
    o = jnp.concatenate(outs, axis=1)
    o_ref[...] = h1 + jnp.dot(o, wo_ref[...], preferred_element_type=F32)


def _mix_mem_call(h, o_d, o_f, w_od, w_of, g, w_q, kv, w_o, seq, n_mem):
    t, d = h.shape
    tm = TM_MIX
    tiles_per_seq = seq // tm
    kern = functools.partial(_mix_mem_kernel, mem_scale=1.0 / math.sqrt(d // MEM_HEADS))
    const = lambda i: (0, 0)
    return pl.pallas_call(
        kern,
        grid=(t // tm,),
        in_specs=[
            pl.BlockSpec((tm, d), lambda i: (i, 0)),
            pl.BlockSpec((tm, D_DIFF_OUT), lambda i: (i, 0)),
            pl.BlockSpec((tm, D_FOX), lambda i: (i, 0)),
            pl.BlockSpec((D_DIFF_OUT, d), const),
            pl.BlockSpec((D_FOX, d), const),
            pl.BlockSpec((1, d), const),
            pl.BlockSpec((d, d), const),
            pl.BlockSpec((n_mem, d), lambda i: (i // tiles_per_seq, 0)),
            pl.BlockSpec((n_mem, d), lambda i: (i // tiles_per_seq, 1)),
            pl.BlockSpec((d, d), const),
        ],
        out_specs=pl.BlockSpec((tm, d), lambda i: (i, 0)),
        out_shape=jax.ShapeDtypeStruct((t, d), F32),
        compiler_params=_params(("arbitrary",)),
    )(h, o_d, o_f, w_od, w_of, g, w_q, kv, kv, w_o)


def _swiglu_chunk(x, w1, w3, w2):
    a = jnp.dot(x, w1, preferred_element_type=F32)
    b = jnp.dot(x, w3, preferred_element_type=F32)
    mid = (a * (1.0 / (1.0 + jnp.exp(-a)))) * b
    return jnp.dot(mid.astype(BF16), w2, preferred_element_type=F32)


def _dense_ffn_kernel(h_ref, g_ref, w1_ref, w3_ref, w2_ref, o_ref, xn_ref):
    f = pl.program_id(1)

    @pl.when(f == 0)
    def _():
        x = h_ref[...]
        xn_ref[...] = ((x * _rms_scale(x, EPS)) * g_ref[...]).astype(BF16)
        o_ref[...] = x

    o_ref[...] += _swiglu_chunk(xn_ref[...], w1_ref[...], w3_ref[...], w2_ref[...])


def _dense_ffn_call(h, g, w1, w3, w2):
    t, d = h.shape
    tm, tf = TM_FFN, TF_DENSE
    dff = w1.shape[1]
    return pl.pallas_call(
        _dense_ffn_kernel,
        grid=(t // tm, dff // tf),
        in_specs=[pl.BlockSpec((tm, d), lambda i, f: (i, 0)),
                  pl.BlockSpec((1, d), lambda i, f: (0, 0)),
                  pl.BlockSpec((d, tf), lambda i, f: (0, f)),
                  pl.BlockSpec((d, tf), lambda i, f: (0, f)),
                  pl.BlockSpec((tf, d), lambda i, f: (f, 0))],
        out_specs=pl.BlockSpec((tm, d), lambda i, f: (i, 0)),
        out_shape=jax.ShapeDtypeStruct((t, d), F32),
        scratch_shapes=[pltpu.VMEM((tm, d), BF16)],
        compiler_params=_params(("arbitrary", "arbitrary")),
    )(h, g, w1, w3, w2)


def _router_kernel(h_ref, g_ref, wr_ref, hn_ref, idx_ref, gate_ref):
    x = h_ref[...]
    xn = ((x * _rms_scale(x, EPS)) * g_ref[...]).astype(BF16)
    hn_ref[...] = xn
    logits = jnp.dot(xn, wr_ref[...], preferred_element_type=F32)
    lane = lax.broadcasted_iota(jnp.int32, logits.shape, 1)
    logits = jnp.where(lane < N_EXPERTS, logits, -jnp.inf)
    v1 = jnp.max(logits, axis=-1, keepdims=True)
    i1 = jnp.min(jnp.where(logits == v1, lane, LANES), axis=-1, keepdims=True)
    rest = jnp.where(lane == i1, -jnp.inf, logits)
    v2 = jnp.max(rest, axis=-1, keepdims=True)
    i2 = jnp.min(jnp.where(rest == v2, lane, LANES), axis=-1, keepdims=True)
    e = jnp.exp(v2 - v1)
    inv = 1.0 / (1.0 + e)
    idx_ref[...] = jnp.where(lane == 0, i1, jnp.where(lane == 1, i2, 0))
    gate_ref[...] = jnp.where(lane == 0, inv, jnp.where(lane == 1, e * inv, 0.0))


def _router_call(h, g, w_r):
    t, d = h.shape
    tm = TM_FFN
    return pl.pallas_call(
        _router_kernel,
        grid=(t // tm,),
        in_specs=[pl.BlockSpec((tm, d), lambda i: (i, 0)),
                  pl.BlockSpec((1, d), lambda i: (0, 0)),
                  pl.BlockSpec((d, LANES), lambda i: (0, 0))],
        out_specs=[pl.BlockSpec((tm, d), lambda i: (i, 0)),
                   pl.BlockSpec((tm, LANES), lambda i: (i, 0)),
                   pl.BlockSpec((tm, LANES), lambda i: (i, 0))],
        out_shape=[jax.ShapeDtypeStruct((t, d), BF16),
                   jax.ShapeDtypeStruct((t, LANES), jnp.int32),
                   jax.ShapeDtypeStruct((t, LANES), F32)],
        compiler_params=_params(("arbitrary",)),
    )(h, g, w_r)


def _expert_ffn_kernel(te_ref, nu_ref, x_ref, w1_ref, w3_ref, w2_ref, o_ref, acc_ref, *, n_chunks):
    i = pl.program_id(0)
    f = pl.program_id(1)

    @pl.when(i < nu_ref[0])
    def _():
        @pl.when(f == 0)
        def _():
            acc_ref[...] = jnp.zeros_like(acc_ref)

        acc_ref[...] += _swiglu_chunk(x_ref[...], w1_ref[0].astype(BF16), w3_ref[0].astype(BF16),
                                      w2_ref[0].astype(BF16))

        @pl.when(f == n_chunks - 1)
        def _():
            o_ref[...] = acc_ref[...].astype(o_ref.dtype)


def _expert_ffn_call(tile_expert, n_used, x_sorted, w1, w3, w2):
    p_rows, d = x_sorted.shape
    tm, tf = TM_FFN, TF_MOE
    dff = w1.shape[2]
    n_chunks = dff // tf
    kern = functools.partial(_expert_ffn_kernel, n_chunks=n_chunks)

    def row_map(i, f, te, nu):
        return (jnp.minimum(i, nu[0] - 1), 0)

    def chunk(i, f, nu):
        return jnp.where(i < nu[0], f, n_chunks - 1)

    grid_spec = pltpu.PrefetchScalarGridSpec(
        num_scalar_prefetch=2,
        grid=(p_rows // tm, n_chunks),
        in_specs=[
            pl.BlockSpec((tm, d), row_map),
            pl.BlockSpec((1, d, tf), lambda i, f, te, nu: (te[i], 0, chunk(i, f, nu))),
            pl.BlockSpec((1, d, tf), lambda i, f, te, nu: (te[i], 0, chunk(i, f, nu))),
            pl.BlockSpec((1, tf, d), lambda i, f, te, nu: (te[i], chunk(i, f, nu), 0)),
        ],
        out_specs=pl.BlockSpec((tm, d), row_map),
        scratch_shapes=[pltpu.VMEM((tm, d), F32)],
    )
    return pl.pallas_call(
        kern,
        grid_spec=grid_spec,
        out_shape=jax.ShapeDtypeStruct((p_rows, d), BF16),
        compiler_params=_params(("arbitrary", "arbitrary")),
    )(tile_expert, n_used, x_sorted, w1, w3, w2)


def _route(idx, tm):
    t = idx.shape[0]
    n_assign = 2 * t
    p_rows = n_assign + N_EXPERTS * tm
    e_flat = idx.reshape(-1)
    onehot = (e_flat[:, None] == jnp.arange(N_EXPERTS, dtype=jnp.int32)[None, :]).astype(jnp.int32)
    running = jnp.cumsum(onehot, axis=0)
    counts = running[-1]
    rank = jnp.sum(running * onehot, axis=1) - 1
    padded = ((counts + tm - 1) // tm) * tm
    pend = jnp.cumsum(padded)
    pstart = pend - padded
    gstart = jnp.cumsum(counts) - counts
    dest = pstart[e_flat] + rank
    n_tiles = p_rows // tm
    n_used = (pend[-1] // tm).astype(jnp.int32)
    tile_start = jnp.arange(n_tiles, dtype=jnp.int32) * tm
    tile_expert = jnp.sum((tile_start[:, None] >= pend[None, :]).astype(jnp.int32), axis=1)
    last_expert = jnp.sum((((n_used - 1) * tm) >= pend).astype(jnp.int32))
    tile_expert = jnp.where(tile_start < pend[-1], tile_expert, last_expert).astype(jnp.int32)
    order = jnp.sort(e_flat * n_assign + jnp.arange(n_assign, dtype=jnp.int32)) % n_assign
    row_expert = jnp.repeat(tile_expert, tm)
    within = jnp.arange(p_rows, dtype=jnp.int32) - pstart[row_expert]
    src = jnp.clip(gstart[row_expert] + within, 0, n_assign - 1)
    row_token = jnp.where(within < counts[row_expert], order[src] // 2, 0)
    return row_token, dest.reshape(t, 2), tile_expert, n_used.reshape(1)


def _moe_ffn(h, g, w_r, w1, w3, w2):
    hn, idx_l, gate_l = _router_call(h, g, w_r)
    row_token, slot, tile_expert, n_used = _route(idx_l[:, :2], TM_FFN)
    x_sorted = jnp.take(hn, row_token, axis=0)
    y = _expert_ffn_call(tile_expert, n_used, x_sorted, w1, w3, w2)
    return (h + gate_l[:, 0:1] * jnp.take(y, slot[:, 0], axis=0).astype(F32)
            + gate_l[:, 1:2] * jnp.take(y, slot[:, 1], axis=0).astype(F32))


def _final_norm_kernel(x_ref, g_ref, o_ref):
    x = x_ref[...]
    o_ref[...] = (x * _rms_scale(x, EPS)) * g_ref[...]


def _final_norm_call(h, g):
    t, d = h.shape
    tm = TM_FFN
    return pl.pallas_call(
        _final_norm_kernel,
        grid=(t // tm,),
        in_specs=[pl.BlockSpec((tm, d), lambda i: (i, 0)), pl.BlockSpec((1, d), lambda i: (0, 0))],
        out_specs=pl.BlockSpec((tm, d), lambda i: (i, 0)),
        out_shape=jax.ShapeDtypeStruct((t, d), F32),
        compiler_params=_params(("arbitrary",)),
    )(h, g)


def _rope_tables(positions):
    inv_freq = ROPE_THETA ** (-jnp.arange(0, ROPE_DIM, 2, dtype=F32) / ROPE_DIM)
    b, s = positions.shape
    dim = jnp.arange(LANES, dtype=jnp.int32) % HEAD_DIM
    ang = positions.astype(F32).reshape(b * s, 1) * inv_freq[dim % ROPE_HALF][None, :]
    cos, sin = jnp.cos(ang), jnp.sin(ang)
    first = (dim < ROPE_HALF)[None, :]
    second = jnp.logical_and(dim >= ROPE_HALF, dim < ROPE_DIM)[None, :]
    cos_t = jnp.where(jnp.logical_or(first, second), cos, 1.0)
    sa_t = jnp.where(second, sin, 0.0)
    sb_t = jnp.where(first, -sin, 0.0)
    return cos_t, sa_t, sb_t


def _decay_selectors():
    src = jnp.arange(LANES, dtype=jnp.int32)[:, None]
    dst = jnp.arange(LANES, dtype=jnp.int32)[None, :]
    sels = []
    for s in range(N_SPLIT):
        hit = jnp.logical_and(src < N_FOX_HEADS, dst == N_SPLIT * src + s)
        sels.append(hit.astype(BF16))
    return jnp.stack(sels)


def _split_w_in(w):
    w_f = jnp.pad(w[:, O_F:O_F + N_FOX_HEADS], ((0, 0), (0, LANES - N_FOX_HEADS))).astype(BF16)
    return w.astype(BF16), w_f


def kernel(x, mem, positions, norm_mix_g, w_in, lam_q1, lam_k1, lam_q2, lam_k2, diff_subln_g, fox_bias, w_out, norm_mem_g, mem_norm_g, w_mq, w_mkv, w_mo, norm_ffn_g, w_ff1, w_ff3, w_ff2, w_router, w_e1, w_e3, w_e2, final_norm_g):
    batch, seq, d = x.shape
    n_mem = mem.shape[1]
    depth = w_in.shape[0]
    t = batch * seq
    cos_t, sa_t, sb_t = _rope_tables(positions)
    sel = _decay_selectors()
    h = x.reshape(t, d)
    mem2 = mem.reshape(batch * n_mem, d)
    row = lambda v: v.reshape(1, -1).astype(F32)

    for l in range(depth):
        lam_init = 0.8 - 0.6 * math.exp(-0.3 * l)
        lam = (jnp.exp(jnp.sum(lam_q1[l] * lam_k1[l])) - jnp.exp(jnp.sum(lam_q2[l] * lam_k2[l]))
               + lam_init).reshape(1).astype(F32)
        w_main, w_f = _split_w_in(w_in[l])
        f_bias = jnp.pad(fox_bias[l], (0, LANES - N_FOX_HEADS)).reshape(1, LANES)

        qk, vt, c, kx = _proj_call(h, row(norm_mix_g[l]), w_main, w_f, f_bias, sel, cos_t, sa_t, sb_t,
                                   batch, seq)
        qk3 = qk.reshape(batch, seq, qk.shape[1])
        ct = jnp.transpose(c.reshape(batch, seq, LANES)[:, :, :N_FOX_HEADS], (0, 2, 1))
        o_d = _diff_call(lam, qk3, vt, diff_subln_g[l].reshape(DIFF_V_DIM, 1), 1.0 - lam_init, batch, seq)
        o_f = _fox_call(qk3, kx.reshape(batch, seq, LANES), vt, ct, batch, seq)

        w_o = w_out[l].astype(BF16)
        kv = _norm_matmul_call(mem2, row(mem_norm_g[l]), w_mkv[l].astype(BF16), 512, 1024)
        h = _mix_mem_call(h, o_d.reshape(t, D_DIFF_OUT), o_f.reshape(t, D_FOX),
                          w_o[:D_DIFF_OUT], w_o[D_DIFF_OUT:], row(norm_mem_g[l]),
                          w_mq[l].astype(BF16), kv, w_mo[l].astype(BF16), seq, n_mem)

        i = l // 2
        if l % 2 == 0:
            h = _dense_ffn_call(h, row(norm_ffn_g[l]), w_ff1[i].astype(BF16), w_ff3[i].astype(BF16),
                                w_ff2[i].astype(BF16))
        else:
            w_r = jnp.pad(w_router[i], ((0, 0), (0, LANES - N_EXPERTS))).astype(BF16)
            h = _moe_ffn(h, row(norm_ffn_g[l]), w_r, w_e1[i], w_e3[i], w_e2[i])

    return _final_norm_call(h, row(final_norm_g)).reshape(batch, seq, d)
```

```python
import functools
import math

import jax
import jax.numpy as jnp
from jax import lax
from jax.experimental import pallas as pl
from jax.experimental.pallas import tpu as pltpu

F32 = jnp.float32
BF16 = jnp.bfloat16

HEAD_DIM = 64
N_DIFF_HEADS = 4
DIFF_V_DIM = 128
N_FOX_HEADS = 8
D_DIFF_QK = N_DIFF_HEADS * HEAD_DIM
D_DIFF_OUT = N_DIFF_HEADS * DIFF_V_DIM
D_FOX = N_FOX_HEADS * HEAD_DIM
ATTN_SCALE = 1.0 / math.sqrt(HEAD_DIM)
ROPE_DIM = HEAD_DIM // 4
ROPE_HALF = ROPE_DIM // 2
ROPE_THETA = 500000.0
MEM_HEADS = 4
N_EXPERTS = 8
EPS = 1e-6
SUBLN_EPS = 1e-5
LOG2E = 1.0 / math.log(2.0)

LANES = 128
VMEM_LIMIT = 48 * 1024 * 1024

TM_PROJ = 512
TN_PROJ = 512
N_QK_BLK = 4
TQ = 512
TK = 512
TM_MIX = 512
TM_FFN = 1024
TF_DENSE = 256
TF_MOE = 512
CUM_CHUNK = 256
V_ROWS = LANES + 8
N_SPLIT = 3


def _params(sem):
    return pltpu.CompilerParams(dimension_semantics=sem, vmem_limit_bytes=VMEM_LIMIT)


def _rms_scale(x, eps):
    return lax.rsqrt(jnp.mean(x * x, axis=-1, keepdims=True) + eps)


def _split3(v):
    hi = v.astype(BF16)
    r1 = v - hi.astype(F32)
    mid = r1.astype(BF16)
    lo = (r1 - mid.astype(F32)).astype(BF16)
    return hi, mid, lo


O_VD = 4 * D_DIFF_QK
O_QF = O_VD + D_DIFF_OUT
O_KF = O_QF + D_FOX
O_VF = O_KF + D_FOX
O_F = O_VF + D_FOX
Q_SCALE = ATTN_SCALE * LOG2E
QK_COLS = 2 * TN_PROJ + 2 * N_FOX_HEADS * LANES
_PROJ_BLOCKS = (
    (0, "rope", 0, Q_SCALE),
    (2 * D_DIFF_QK, "rope", TN_PROJ, 1.0),
    (O_QF, "fox_q", 2 * TN_PROJ, Q_SCALE),
    (O_KF, "fox_k", 2 * TN_PROJ + N_FOX_HEADS * LANES, 1.0),
    (O_VD, "vt", 0, 1.0),
    (O_VF, "vt", N_DIFF_HEADS, 1.0),
)


def _proj_kernel(x_ref, g_ref, w_ref, wf_ref, fb_ref, sel_ref, cos_ref, sa_ref, sb_ref,
                 qk_ref, vt_ref, c_ref, carry_ref, *, tiles_per_seq):
    i = pl.program_id(0)
    tm = x_ref.shape[0]
    x = x_ref[...]
    xnb = ((x * _rms_scale(x, EPS)) * g_ref[...]).astype(BF16)

    logit = jnp.dot(xnb, wf_ref[...], preferred_element_type=F32) + fb_ref[...]
    logf = jnp.minimum(logit, 0.0) - jnp.log1p(jnp.exp(-jnp.abs(logit)))

    @pl.when(i % tiles_per_seq == 0)
    def _():
        carry_ref[...] = jnp.zeros_like(carry_ref)

    r = lax.broadcasted_iota(jnp.int32, (CUM_CHUNK, CUM_CHUNK), 0)
    cidx = lax.broadcasted_iota(jnp.int32, (CUM_CHUNK, CUM_CHUNK), 1)
    tri = (cidx <= r).astype(BF16)
    carry = carry_ref[...]
    chunks = []
    for ch in range(tm // CUM_CHUNK):
        hi, mid, lo = _split3(logf[ch * CUM_CHUNK:(ch + 1) * CUM_CHUNK])
        cs = (jnp.dot(tri, hi, preferred_element_type=F32)
              + jnp.dot(tri, mid, preferred_element_type=F32)
              + jnp.dot(tri, lo, preferred_element_type=F32)) + carry
        chunks.append(cs * LOG2E)
        carry = cs[CUM_CHUNK - 1:CUM_CHUNK, :]
    carry_ref[...] = carry
    c2 = jnp.concatenate(chunks, axis=0)
    c_ref[...] = c2
    hi, mid, lo = _split3(-c2)
    kx = (jnp.dot(hi, sel_ref[0], preferred_element_type=F32)
          + jnp.dot(mid, sel_ref[1], preferred_element_type=F32)
          + jnp.dot(lo, sel_ref[2], preferred_element_type=F32))
    lane = lax.broadcasted_iota(jnp.int32, (tm, LANES), 1)
    low_half = lane < HEAD_DIM
    extra = [jnp.logical_and(lane >= HEAD_DIM, lane < HEAD_DIM + N_SPLIT), lane < N_SPLIT]

    cos = cos_ref[...]
    sa = sa_ref[...]
    sb = sb_ref[...]
    ones_row = (lax.broadcasted_iota(jnp.int32, (V_ROWS - LANES, TK), 0) == 0).astype(vt_ref.dtype)
    for src, kind, dst, scale in _PROJ_BLOCKS:
        acc = jnp.dot(xnb, w_ref[:, src:src + TN_PROJ], preferred_element_type=F32)
        if scale != 1.0:
            acc = acc * scale
        if kind == "rope":
            for s in range(TN_PROJ // LANES):
                t = acc[:, s * LANES:(s + 1) * LANES]
                rot = (t * cos + pltpu.roll(t, ROPE_HALF, 1) * sa
                       + pltpu.roll(t, LANES - ROPE_HALF, 1) * sb)
                qk_ref[:, dst + s * LANES:dst + (s + 1) * LANES] = rot.astype(qk_ref.dtype)
        elif kind in ("fox_q", "fox_k"):
            for pr in range(TN_PROJ // LANES):
                pair = acc[:, pr * LANES:(pr + 1) * LANES]
                for par in range(2):
                    head = 2 * pr + par
                    if kind == "fox_q":
                        other = extra[par].astype(F32)
                    else:
                        first = HEAD_DIM if par == 0 else 0
                        moved = pltpu.roll(kx, (first - N_SPLIT * head) % LANES, 1)
                        other = jnp.where(extra[par], moved, 0.0)
                    keep = low_half if par == 0 else jnp.logical_not(low_half)
                    qk_ref[:, dst + head * LANES:dst + (head + 1) * LANES] = (
                        jnp.where(keep, pair, other).astype(qk_ref.dtype))
        else:
            at = acc.T
            for hd in range(TN_PROJ // LANES):
                vt_ref[0, dst + hd, 0, 0:LANES, :] = at[hd * LANES:(hd + 1) * LANES, :].astype(vt_ref.dtype)
                vt_ref[0, dst + hd, 0, LANES:V_ROWS, :] = ones_row


def _proj_call(h, g, w_main, w_f, f_bias, sel, cos_t, sa_t, sb_t, batch, seq):
    t, d = h.shape
    tm = TM_PROJ
    assert tm == TK
    tiles_per_seq = seq // tm
    n_col = w_main.shape[1]
    n_vh = (len(_PROJ_BLOCKS) - N_QK_BLK) * (TN_PROJ // LANES)
    kern = functools.partial(_proj_kernel, tiles_per_seq=tiles_per_seq)
    const2 = lambda i: (0, 0)
    return pl.pallas_call(
        kern,
        grid=(t // tm,),
        in_specs=[
            pl.BlockSpec((tm, d), lambda i: (i, 0)),
            pl.BlockSpec((1, d), const2),
            pl.BlockSpec((d, n_col), const2),
            pl.BlockSpec((d, LANES), const2),
            pl.BlockSpec((1, LANES), const2),
            pl.BlockSpec((N_SPLIT, LANES, LANES), lambda i: (0, 0, 0)),
            pl.BlockSpec((tm, LANES), lambda i: (i, 0)),
            pl.BlockSpec((tm, LANES), lambda i: (i, 0)),
            pl.BlockSpec((tm, LANES), lambda i: (i, 0)),
        ],
        out_specs=[
            pl.BlockSpec((tm, QK_COLS), lambda i: (i, 0)),
            pl.BlockSpec((1, n_vh, 1, V_ROWS, TK), lambda i: (i // tiles_per_seq, 0, i % tiles_per_seq, 0, 0)),
            pl.BlockSpec((tm, LANES), lambda i: (i, 0)),
        ],
        out_shape=[
            jax.ShapeDtypeStruct((t, QK_COLS), BF16),
            jax.ShapeDtypeStruct((batch, n_vh, seq // TK, V_ROWS, TK), BF16),
            jax.ShapeDtypeStruct((t, LANES), F32),
        ],
        scratch_shapes=[pltpu.VMEM((1, LANES), F32)],
        compiler_params=_params(("arbitrary",)),
    )(h, g, w_main, w_f, f_bias, sel, cos_t, sa_t, sb_t)


def _head_mask(q, half):
    lane = lax.broadcasted_iota(jnp.int32, q.shape, 1)
    return jnp.where(lane // HEAD_DIM == half, q, jnp.zeros_like(q))


def _causal_mask():
    kpos = lax.broadcasted_iota(jnp.int32, (TK, TQ), 0)
    qpos = lax.broadcasted_iota(jnp.int32, (TK, TQ), 1)
    return kpos <= qpos


def _scores_t(k, q):
    return lax.dot_general(k, q, (((1,), (1,)), ((), ())), preferred_element_type=F32)


def _online_update(u, vt, shift, acc_ref, m_ref):
    mu = jnp.max(u, axis=0, keepdims=True) + shift
    m_old = m_ref[...]
    m_new = jnp.maximum(m_old, mu)
    p = jnp.exp2(u - (m_new - shift)).astype(BF16)
    pv = jnp.dot(vt, p, preferred_element_type=F32)
    acc_ref[...] = jnp.exp2(m_old - m_new) * acc_ref[...] + pv
    m_ref[...] = m_new


def _normalised(acc_ref):
    acc = acc_ref[...]
    return acc[:LANES] * (1.0 / acc[LANES:LANES + 1])


N_TILES = 4
UNROLL = 4


def _attn_scratch():
    return ([pltpu.VMEM((TK, TQ), F32)] * (3 * N_TILES) + [pltpu.VMEM((V_ROWS, TQ), F32)] * N_TILES
            + [pltpu.VMEM((1, TQ), F32)] * N_TILES)


def _attend(qi, qk_fn, qk_next_fn, vt_fn, shifts, scratch):
    assert TQ == TK
    s_a, s_b, s_c = (scratch[n * N_TILES:(n + 1) * N_TILES] for n in range(3))
    acc = scratch[3 * N_TILES:4 * N_TILES]
    m = scratch[4 * N_TILES:]
    for t in range(N_TILES):
        m[t][...] = jnp.full(m[t].shape, -jnp.inf, F32)
        acc[t][...] = jnp.zeros(acc[t].shape, F32)

    def stage(j, cur, nxt, masked):
        for t in range(N_TILES):
            nxt[t][...] = qk_next_fn(t) if masked else qk_fn(j + 1, t)
            u = cur[t][...]
            if masked:
                u = jnp.where(_causal_mask(), u, -jnp.inf)
            _online_update(u, vt_fn(j, t), shifts[t], acc[t], m[t])

    @pl.when(qi == 0)
    def _():
        for t in range(N_TILES):
            s_a[t][...] = qk_fn(0, t)
        stage(0, s_a, s_c, True)

    @pl.when(qi > 0)
    def _():
        stage(0, s_c, s_a, False)
        bufs = (s_a, s_b)

        def body(i, carry):
            for n in range(UNROLL):
                stage(UNROLL * i + 1 + n, bufs[n % 2], bufs[(n + 1) % 2], False)
            return carry

        lax.fori_loop(0, (qi - 1) // UNROLL, body, 0)

        for rem in range(UNROLL):
            @pl.when((qi - 1) % UNROLL == rem)
            def _(rem=rem):
                for n in range(rem):
                    stage(qi - rem + n, bufs[n % 2], bufs[(n + 1) % 2], False)
                stage(qi, bufs[rem % 2], s_c, True)

    return acc


def _fox_kernel(q_ref, qn_ref, k_ref, vt_ref, ct_ref, o_ref, *scratch):
    grp = pl.program_id(1)
    qi = pl.program_id(2)
    cqs = [ct_ref[0, pl.ds(N_TILES * grp + t, 1), :] for t in range(N_TILES)]

    def scores(j, t, qr):
        start = pl.multiple_of(j * TK, TK)
        return _scores_t(k_ref[0, pl.ds(start, TK), t * LANES:(t + 1) * LANES],
                         qr[0, :, t * LANES:(t + 1) * LANES])

    def vt_fn(j, t):
        return vt_ref[0, t // 2, j]

    acc = _attend(qi, lambda j, t: scores(j, t, q_ref), lambda t: scores(0, t, qn_ref), vt_fn, cqs, scratch)
    for pp in range(2):
        ot = jnp.concatenate([_normalised(acc[2 * pp])[:HEAD_DIM],
                              _normalised(acc[2 * pp + 1])[HEAD_DIM:]], axis=0)
        o_ref[0, :, pp * LANES:(pp + 1) * LANES] = ot.T.astype(o_ref.dtype)


def _fox_call(qk, vt, ct, batch, seq):
    nkb = seq // TK
    nq = seq // TQ
    w4 = N_TILES * LANES
    return pl.pallas_call(
        _fox_kernel,
        grid=(batch, N_FOX_HEADS // N_TILES, nq),
        in_specs=[
            pl.BlockSpec((1, TQ, w4), lambda b, g, qi: (b, qi, 2 + g)),
            pl.BlockSpec((1, TQ, w4), lambda b, g, qi: (b, jnp.minimum(qi + 1, nq - 1), 2 + g)),
            pl.BlockSpec((1, seq, w4), lambda b, g, qi: (b, 0, 4 + g)),
            pl.BlockSpec((1, 2, nkb, V_ROWS, TK), lambda b, g, qi: (b, N_DIFF_HEADS // 2 + g, 0, 0, 0)),
            pl.BlockSpec((1, N_FOX_HEADS, TQ), lambda b, g, qi: (b, 0, qi)),
        ],
        out_specs=pl.BlockSpec((1, TQ, 2 * LANES), lambda b, g, qi: (b, qi, g)),
        out_shape=jax.ShapeDtypeStruct((batch, seq, D_FOX), BF16),
        scratch_shapes=_attn_scratch(),
        compiler_params=_params(("arbitrary", "arbitrary", "arbitrary")),
    )(qk, qk, qk, vt, ct)


def _diff_kernel(lam_ref, q1_ref, q2_ref, q1n_ref, q2n_ref, k1_ref, k2_ref, vt_ref, g_ref, o_ref, *scratch,
                 out_scale):
    qi = pl.program_id(2)
    k_refs = [k1_ref, k2_ref]
    qms = [_head_mask(r[0], hh) for hh in range(2) for r in (q1_ref, q2_ref)]
    qms_next = [_head_mask(r[0], hh) for hh in range(2) for r in (q1n_ref, q2n_ref)]

    def scores(j, t, qm):
        start = pl.multiple_of(j * TK, TK)
        return _scores_t(k_refs[t % 2][0, pl.ds(start, TK), :], qm[t])

    def vt_fn(j, t):
        return vt_ref[0, t // 2, j]

    acc = _attend(qi, lambda j, t: scores(j, t, qms), lambda t: scores(0, t, qms_next), vt_fn,
                  [0.0] * N_TILES, scratch)
    for hh in range(2):
        ot = _normalised(acc[2 * hh]) - lam_ref[0] * _normalised(acc[2 * hh + 1])
        y = ot * lax.rsqrt(jnp.mean(ot * ot, axis=0, keepdims=True) + SUBLN_EPS)
        y = (y * g_ref[...]) * out_scale
        o_ref[0, :, hh * LANES:(hh + 1) * LANES] = y.T.astype(o_ref.dtype)


def _diff_call(lam, qk, vt, g_col, out_scale, batch, seq):
    nkb = seq // TK
    nq = seq // TQ
    n_pairs = N_DIFF_HEADS // 2
    kern = functools.partial(_diff_kernel, out_scale=out_scale)
    grid_spec = pltpu.PrefetchScalarGridSpec(
        num_scalar_prefetch=1,
        grid=(batch, n_pairs, nq),
        in_specs=[
            pl.BlockSpec((1, TQ, LANES), lambda b, p, qi, lam: (b, qi, p)),
            pl.BlockSpec((1, TQ, LANES), lambda b, p, qi, lam: (b, qi, 2 + p)),
            pl.BlockSpec((1, TQ, LANES), lambda b, p, qi, lam: (b, jnp.minimum(qi + 1, nq - 1), p)),
            pl.BlockSpec((1, TQ, LANES), lambda b, p, qi, lam: (b, jnp.minimum(qi + 1, nq - 1), 2 + p)),
            pl.BlockSpec((1, seq, LANES), lambda b, p, qi, lam: (b, 0, 4 + p)),
            pl.BlockSpec((1, seq, LANES), lambda b, p, qi, lam: (b, 0, 6 + p)),
            pl.BlockSpec((1, 2, nkb, V_ROWS, TK), lambda b, p, qi, lam: (b, p, 0, 0, 0)),
            pl.BlockSpec((DIFF_V_DIM, 1), lambda b, p, qi, lam: (0, 0)),
        ],
        out_specs=pl.BlockSpec((1, TQ, 2 * LANES), lambda b, p, qi, lam: (b, qi, p)),
        scratch_shapes=_attn_scratch(),
    )
    return pl.pallas_call(
        kern,
        grid_spec=grid_spec,
        out_shape=jax.ShapeDtypeStruct((batch, seq, D_DIFF_OUT), BF16),
        compiler_params=_params(("arbitrary", "arbitrary", "arbitrary")),
    )(lam, qk, qk, qk, qk, qk, qk, vt, g_col)


def _norm_matmul_kernel(x_ref, g_ref, w_ref, o_ref):
    x = x_ref[...]
    xn = ((x * _rms_scale(x, EPS)) * g_ref[...]).astype(BF16)
    o_ref[...] = jnp.dot(xn, w_ref[...], preferred_element_type=F32).astype(o_ref.dtype)


def _norm_matmul_call(x, g, w, tm, tn):
    t, d = x.shape
    n = w.shape[1]
    return pl.pallas_call(
        _norm_matmul_kernel,
        grid=(t // tm, n // tn),
        in_specs=[pl.BlockSpec((tm, d), lambda i, j: (i, 0)),
                  pl.BlockSpec((1, d), lambda i, j: (0, 0)),
                  pl.BlockSpec((d, tn), lambda i, j: (0, j))],
        out_specs=pl.BlockSpec((tm, tn), lambda i, j: (i, j)),
        out_shape=jax.ShapeDtypeStruct((t, n), BF16),
        compiler_params=_params(("arbitrary", "arbitrary")),
    )(x, g, w)


def _mix_mem_kernel(h_ref, od_ref, of_ref, wod_ref, wof_ref, g_ref, wq_ref, k_ref, v_ref, wo_ref,
                    o_ref, *, mem_scale):
    h1 = (h_ref[...]
          + jnp.dot(od_ref[...], wod_ref[...], preferred_element_type=F32)
          + jnp.dot(of_ref[...], wof_ref[...], preferred_element_type=F32))
    xn = ((h1 * _rms_scale(h1, EPS)) * g_ref[...]).astype(BF16)
    q = jnp.dot(xn, wq_ref[...], preferred_element_type=F32).astype(BF16)
    d = q.shape[1]
    hd = d // MEM_HEADS
    outs = []
    for hh in range(MEM_HEADS):
        qh = q[:, hh * hd:(hh + 1) * hd]
        kh = k_ref[:, hh * hd:(hh + 1) * hd]
        vh = v_ref[:, hh * hd:(hh + 1) * hd]
        s = lax.dot_general(qh, kh, (((1,), (1,)), ((), ())),
                            preferred_element_type=F32) * mem_scale
        m = jnp.max(s, axis=-1, keepdims=True)
        e = jnp.exp(s - m)
        p = e * (1.0 / jnp.sum(e, axis=-1, keepdims=True))
        outs.append(jnp.dot(p.astype(BF16), vh, preferred_element_type=F32).astype(BF16))
    o = jnp.concatenate(outs, axis=1)
    o_ref[...] = h1 + jnp.dot(o, wo_ref[...], preferred_element_type=F32)


def _mix_mem_call(h, o_d, o_f, w_od, w_of, g, w_q, kv, w_o, seq, n_mem):
    t, d = h.shape
    tm = TM_MIX
    tiles_per_seq = seq // tm
    kern = functools.partial(_mix_mem_kernel, mem_scale=1.0 / math.sqrt(d // MEM_HEADS))
    const = lambda i: (0, 0)
    return pl.pallas_call(
        kern,
        grid=(t // tm,),
        in_specs=[
            pl.BlockSpec((tm, d), lambda i: (i, 0)),
            pl.BlockSpec((tm, D_DIFF_OUT), lambda i: (i, 0)),
            pl.BlockSpec((tm, D_FOX), lambda i: (i, 0)),
            pl.BlockSpec((D_DIFF_OUT, d), const),
            pl.BlockSpec((D_FOX, d), const),
            pl.BlockSpec((1, d), const),
            pl.BlockSpec((d, d), const),
            pl.BlockSpec((n_mem, d), lambda i: (i // tiles_per_seq, 0)),
            pl.BlockSpec((n_mem, d), lambda i: (i // tiles_per_seq, 1)),
            pl.BlockSpec((d, d), const),
        ],
        out_specs=pl.BlockSpec((tm, d), lambda i: (i, 0)),
        out_shape=jax.ShapeDtypeStruct((t, d), F32),
        compiler_params=_params(("arbitrary",)),
    )(h, o_d, o_f, w_od, w_of, g, w_q, kv, kv, w_o)


def _swiglu_chunk(x, w1, w3, w2):
    a = jnp.dot(x, w1, preferred_element_type=F32)
    b = jnp.dot(x, w3, preferred_element_type=F32)
    mid = (a * (1.0 / (1.0 + jnp.exp(-a)))) * b
    return jnp.dot(mid.astype(BF16), w2, preferred_element_type=F32)


def _dense_ffn_kernel(h_ref, g_ref, w1_ref, w3_ref, w2_ref, o_ref, xn_ref):
    f = pl.program_id(1)

    @pl.when(f == 0)
    def _():
        x = h_ref[...]
        xn_ref[...] = ((x * _rms_scale(x, EPS)) * g_ref[...]).astype(BF16)
        o_ref[...] = x

    o_ref[...] += _swiglu_chunk(xn_ref[...], w1_ref[...], w3_ref[...], w2_ref[...])


def _dense_ffn_call(h, g, w1, w3, w2):
    t, d = h.shape
    tm, tf = TM_FFN, TF_DENSE
    dff = w1.shape[1]
    return pl.pallas_call(
        _dense_ffn_kernel,
        grid=(t // tm, dff // tf),
        in_specs=[pl.BlockSpec((tm, d), lambda i, f: (i, 0)),
                  pl.BlockSpec((1, d), lambda i, f: (0, 0)),
                  pl.BlockSpec((d, tf), lambda i, f: (0, f)),
                  pl.BlockSpec((d, tf), lambda i, f: (0, f)),
                  pl.BlockSpec((tf, d), lambda i, f: (f, 0))],
        out_specs=pl.BlockSpec((tm, d), lambda i, f: (i, 0)),
        out_shape=jax.ShapeDtypeStruct((t, d), F32),
        scratch_shapes=[pltpu.VMEM((tm, d), BF16)],
        compiler_params=_params(("arbitrary", "arbitrary")),
    )(h, g, w1, w3, w2)


def _router_kernel(h_ref, g_ref, wr_ref, hn_ref, idx_ref, gate_ref):
    x = h_ref[...]
    xn = ((x * _rms_scale(x, EPS)) * g_ref[...]).astype(BF16)
    hn_ref[...] = xn
    logits = jnp.dot(xn, wr_ref[...], preferred_element_type=F32)
    lane = lax.broadcasted_iota(jnp.int32, logits.shape, 1)
    logits = jnp.where(lane < N_EXPERTS, logits, -jnp.inf)
    v1 = jnp.max(logits, axis=-1, keepdims=True)
    i1 = jnp.min(jnp.where(logits == v1, lane, LANES), axis=-1, keepdims=True)
    rest = jnp.where(lane == i1, -jnp.inf, logits)
    v2 = jnp.max(rest, axis=-1, keepdims=True)
    i2 = jnp.min(jnp.where(rest == v2, lane, LANES), axis=-1, keepdims=True)
    e = jnp.exp(v2 - v1)
    inv = 1.0 / (1.0 + e)
    idx_ref[...] = jnp.where(lane == 0, i1, jnp.where(lane == 1, i2, 0))
    gate_ref[...] = jnp.where(lane == 0, inv, jnp.where(lane == 1, e * inv, 0.0))


def _router_call(h, g, w_r):
    t, d = h.shape
    tm = TM_FFN
    return pl.pallas_call(
        _router_kernel,
        grid=(t // tm,),
        in_specs=[pl.BlockSpec((tm, d), lambda i: (i, 0)),
                  pl.BlockSpec((1, d), lambda i: (0, 0)),
                  pl.BlockSpec((d, LANES), lambda i: (0, 0))],
        out_specs=[pl.BlockSpec((tm, d), lambda i: (i, 0)),
                   pl.BlockSpec((tm, LANES), lambda i: (i, 0)),
                   pl.BlockSpec((tm, LANES), lambda i: (i, 0))],
        out_shape=[jax.ShapeDtypeStruct((t, d), BF16),
                   jax.ShapeDtypeStruct((t, LANES), jnp.int32),
                   jax.ShapeDtypeStruct((t, LANES), F32)],
        compiler_params=_params(("arbitrary",)),
    )(h, g, w_r)


def _expert_ffn_kernel(te_ref, nu_ref, x_ref, w1_ref, w3_ref, w2_ref, o_ref, acc_ref, *, n_chunks):
    i = pl.program_id(0)
    f = pl.program_id(1)

    @pl.when(i < nu_ref[0])
    def _():
        @pl.when(f == 0)
        def _():
            acc_ref[...] = jnp.zeros_like(acc_ref)

        acc_ref[...] += _swiglu_chunk(x_ref[...], w1_ref[0, 0].astype(BF16), w3_ref[0, 0].astype(BF16),
                                      w2_ref[0, 0].astype(BF16))

        @pl.when(f == n_chunks - 1)
        def _():
            o_ref[...] = acc_ref[...].astype(o_ref.dtype)


def _expert_ffn_call(tile_expert, n_used, x_sorted, w1, w3, w2, layer):
    p_rows, d = x_sorted.shape
    tm, tf = TM_FFN, TF_MOE
    dff = w1.shape[3]
    n_chunks = dff // tf
    kern = functools.partial(_expert_ffn_kernel, n_chunks=n_chunks)

    def row_map(i, f, te, nu):
        return (jnp.minimum(i, nu[0] - 1), 0)

    def chunk(i, f, nu):
        return jnp.where(i < nu[0], f, n_chunks - 1)

    grid_spec = pltpu.PrefetchScalarGridSpec(
        num_scalar_prefetch=2,
        grid=(p_rows // tm, n_chunks),
        in_specs=[
            pl.BlockSpec((tm, d), row_map),
            pl.BlockSpec((1, 1, d, tf), lambda i, f, te, nu: (layer, te[i], 0, chunk(i, f, nu))),
            pl.BlockSpec((1, 1, d, tf), lambda i, f, te, nu: (layer, te[i], 0, chunk(i, f, nu))),
            pl.BlockSpec((1, 1, tf, d), lambda i, f, te, nu: (layer, te[i], chunk(i, f, nu), 0)),
        ],
        out_specs=pl.BlockSpec((tm, d), row_map),
        scratch_shapes=[pltpu.VMEM((tm, d), F32)],
    )
    return pl.pallas_call(
        kern,
        grid_spec=grid_spec,
        out_shape=jax.ShapeDtypeStruct((p_rows, d), BF16),
        compiler_params=_params(("arbitrary", "arbitrary")),
    )(tile_expert, n_used, x_sorted, w1, w3, w2)


def _route(idx, tm):
    t = idx.shape[0]
    n_assign = 2 * t
    p_rows = n_assign + N_EXPERTS * tm
    e_flat = idx.reshape(-1)
    onehot = (e_flat[:, None] == jnp.arange(N_EXPERTS, dtype=jnp.int32)[None, :]).astype(jnp.int32)
    running = jnp.cumsum(onehot, axis=0)
    counts = running[-1]
    rank = jnp.sum(running * onehot, axis=1) - 1
    padded = ((counts + tm - 1) // tm) * tm
    pend = jnp.cumsum(padded)
    pstart = pend - padded
    gstart = jnp.cumsum(counts) - counts
    dest = pstart[e_flat] + rank
    n_tiles = p_rows // tm
    n_used = (pend[-1] // tm).astype(jnp.int32)
    tile_start = jnp.arange(n_tiles, dtype=jnp.int32) * tm
    tile_expert = jnp.sum((tile_start[:, None] >= pend[None, :]).astype(jnp.int32), axis=1)
    last_expert = jnp.sum((((n_used - 1) * tm) >= pend).astype(jnp.int32))
    tile_expert = jnp.where(tile_start < pend[-1], tile_expert, last_expert).astype(jnp.int32)
    order = jnp.sort(e_flat * n_assign + jnp.arange(n_assign, dtype=jnp.int32)) % n_assign
    row_expert = jnp.repeat(tile_expert, tm)
    within = jnp.arange(p_rows, dtype=jnp.int32) - pstart[row_expert]
    src = jnp.clip(gstart[row_expert] + within, 0, n_assign - 1)
    row_token = jnp.where(within < counts[row_expert], order[src] // 2, 0)
    return row_token, dest.reshape(t, 2), tile_expert, n_used.reshape(1)


def _moe_ffn(h, g, w_r, w1, w3, w2, layer):
    hn, idx_l, gate_l = _router_call(h, g, w_r)
    row_token, slot, tile_expert, n_used = _route(idx_l[:, :2], TM_FFN)
    x_sorted = jnp.take(hn, row_token, axis=0)
    y = _expert_ffn_call(tile_expert, n_used, x_sorted, w1, w3, w2, layer)
    return (h + gate_l[:, 0:1] * jnp.take(y, slot[:, 0], axis=0).astype(F32)
            + gate_l[:, 1:2] * jnp.take(y, slot[:, 1], axis=0).astype(F32))


def _final_norm_kernel(x_ref, g_ref, o_ref):
    x = x_ref[...]
    o_ref[...] = (x * _rms_scale(x, EPS)) * g_ref[...]


def _final_norm_call(h, g):
    t, d = h.shape
    tm = TM_FFN
    return pl.pallas_call(
        _final_norm_kernel,
        grid=(t // tm,),
        in_specs=[pl.BlockSpec((tm, d), lambda i: (i, 0)), pl.BlockSpec((1, d), lambda i: (0, 0))],
        out_specs=pl.BlockSpec((tm, d), lambda i: (i, 0)),
        out_shape=jax.ShapeDtypeStruct((t, d), F32),
        compiler_params=_params(("arbitrary",)),
    )(h, g)


def _rope_tables(positions):
    inv_freq = ROPE_THETA ** (-jnp.arange(0, ROPE_DIM, 2, dtype=F32) / ROPE_DIM)
    b, s = positions.shape
    dim = jnp.arange(LANES, dtype=jnp.int32) % HEAD_DIM
    ang = positions.astype(F32).reshape(b * s, 1) * inv_freq[dim % ROPE_HALF][None, :]
    cos, sin = jnp.cos(ang), jnp.sin(ang)
    first = (dim < ROPE_HALF)[None, :]
    second = jnp.logical_and(dim >= ROPE_HALF, dim < ROPE_DIM)[None, :]
    cos_t = jnp.where(jnp.logical_or(first, second), cos, 1.0)
    sa_t = jnp.where(second, sin, 0.0)
    sb_t = jnp.where(first, -sin, 0.0)
    return cos_t, sa_t, sb_t


def _decay_selectors():
    src = jnp.arange(LANES, dtype=jnp.int32)[:, None]
    dst = jnp.arange(LANES, dtype=jnp.int32)[None, :]
    sels = []
    for s in range(N_SPLIT):
        hit = jnp.logical_and(src < N_FOX_HEADS, dst == N_SPLIT * src + s)
        sels.append(hit.astype(BF16))
    return jnp.stack(sels)


def _split_w_in(w):
    w_f = jnp.pad(w[:, O_F:O_F + N_FOX_HEADS], ((0, 0), (0, LANES - N_FOX_HEADS))).astype(BF16)
    return w.astype(BF16), w_f


def kernel(x, mem, positions, norm_mix_g, w_in, lam_q1, lam_k1, lam_q2, lam_k2, diff_subln_g, fox_bias, w_out, norm_mem_g, mem_norm_g, w_mq, w_mkv, w_mo, norm_ffn_g, w_ff1, w_ff3, w_ff2, w_router, w_e1, w_e3, w_e2, final_norm_g):
    batch, seq, d = x.shape
    n_mem = mem.shape[1]
    depth = w_in.shape[0]
    t = batch * seq
    cos_t, sa_t, sb_t = _rope_tables(positions)
    sel = _decay_selectors()
    h = x.reshape(t, d)
    mem2 = mem.reshape(batch * n_mem, d)
    row = lambda v: v.reshape(1, -1).astype(F32)

    for l in range(depth):
        lam_init = 0.8 - 0.6 * math.exp(-0.3 * l)
        lam = (jnp.exp(jnp.sum(lam_q1[l] * lam_k1[l])) - jnp.exp(jnp.sum(lam_q2[l] * lam_k2[l]))
               + lam_init).reshape(1).astype(F32)
        w_main, w_f = _split_w_in(w_in[l])
        f_bias = jnp.pad(fox_bias[l], (0, LANES - N_FOX_HEADS)).reshape(1, LANES)

        qk, vt, c = _proj_call(h, row(norm_mix_g[l]), w_main, w_f, f_bias, sel, cos_t, sa_t, sb_t, batch, seq)
        qk3 = qk.reshape(batch, seq, qk.shape[1])
        ct = jnp.transpose(c.reshape(batch, seq, LANES)[:, :, :N_FOX_HEADS], (0, 2, 1))
        o_d = _diff_call(lam, qk3, vt, diff_subln_g[l].reshape(DIFF_V_DIM, 1), 1.0 - lam_init, batch, seq)
        o_f = _fox_call(qk3, vt, ct, batch, seq)

        w_o = w_out[l].astype(BF16)
        kv = _norm_matmul_call(mem2, row(mem_norm_g[l]), w_mkv[l].astype(BF16), 512, 1024)
        h = _mix_mem_call(h, o_d.reshape(t, D_DIFF_OUT), o_f.reshape(t, D_FOX),
                          w_o[:D_DIFF_OUT], w_o[D_DIFF_OUT:], row(norm_mem_g[l]),
                          w_mq[l].astype(BF16), kv, w_mo[l].astype(BF16), seq, n_mem)

        i = l // 2
        if l % 2 == 0:
            h = _dense_ffn_call(h, row(norm_ffn_g[l]), w_ff1[i].astype(BF16), w_ff3[i].astype(BF16),
                                w_ff2[i].astype(BF16))
        else:
            w_r = jnp.pad(w_router[i], ((0, 0), (0, LANES - N_EXPERTS))).astype(BF16)
            h = _moe_ffn(h, row(norm_ffn_g[l]), w_r, w_e1, w_e3, w_e2, i)

    return _final_norm_call(h, row(final_norm_g)).reshape(batch, seq, d)
```

```python
import functools
import math

import jax
import jax.numpy as jnp
from jax import lax
from jax.experimental import pallas as pl
from jax.experimental.pallas import tpu as pltpu

F32 = jnp.float32
BF16 = jnp.bfloat16

HEAD_DIM = 64
N_DIFF_HEADS = 4
DIFF_V_DIM = 128
N_FOX_HEADS = 8
D_DIFF_QK = N_DIFF_HEADS * HEAD_DIM
D_DIFF_OUT = N_DIFF_HEADS * DIFF_V_DIM
D_FOX = N_FOX_HEADS * HEAD_DIM
ATTN_SCALE = 1.0 / math.sqrt(HEAD_DIM)
ROPE_DIM = HEAD_DIM // 4
ROPE_HALF = ROPE_DIM // 2
ROPE_THETA = 500000.0
MEM_HEADS = 4
N_EXPERTS = 8
EPS = 1e-6
SUBLN_EPS = 1e-5
LOG2E = 1.0 / math.log(2.0)

LANES = 128
VMEM_LIMIT = 48 * 1024 * 1024

TM_PROJ = 512
TN_PROJ = 512
N_QK_BLK = 4
TQ = 512
TK = 512
TM_MIX = 512
TM_FFN = 1024
TF_DENSE = 256
TF_MOE = 512
CUM_CHUNK = 256
V_ROWS = LANES + 8
N_SPLIT = 3


def _params(sem):
    return pltpu.CompilerParams(dimension_semantics=sem, vmem_limit_bytes=VMEM_LIMIT)


def _rms_scale(x, eps):
    return lax.rsqrt(jnp.mean(x * x, axis=-1, keepdims=True) + eps)


def _split3(v):
    hi = v.astype(BF16)
    r1 = v - hi.astype(F32)
    mid = r1.astype(BF16)
    lo = (r1 - mid.astype(F32)).astype(BF16)
    return hi, mid, lo


O_VD = 4 * D_DIFF_QK
O_QF = O_VD + D_DIFF_OUT
O_KF = O_QF + D_FOX
O_VF = O_KF + D_FOX
O_F = O_VF + D_FOX
Q_SCALE = ATTN_SCALE * LOG2E
QK_COLS = 2 * TN_PROJ + 2 * N_FOX_HEADS * LANES
_PROJ_BLOCKS = (
    (0, "rope", 0, Q_SCALE),
    (2 * D_DIFF_QK, "rope", TN_PROJ, 1.0),
    (O_QF, "fox_q", 2 * TN_PROJ, Q_SCALE),
    (O_KF, "fox_k", 2 * TN_PROJ + N_FOX_HEADS * LANES, 1.0),
    (O_VD, "vt", 0, 1.0),
    (O_VF, "vt", N_DIFF_HEADS, 1.0),
)


def _proj_kernel(x_ref, g_ref, w_ref, wf_ref, fb_ref, sel_ref, cos_ref, sa_ref, sb_ref,
                 qk_ref, vt_ref, c_ref, carry_ref, *, tiles_per_seq):
    i = pl.program_id(0)
    tm = x_ref.shape[0]
    x = x_ref[...]
    xnb = ((x * _rms_scale(x, EPS)) * g_ref[...]).astype(BF16)

    logit = jnp.dot(xnb, wf_ref[...], preferred_element_type=F32) + fb_ref[...]
    logf = jnp.minimum(logit, 0.0) - jnp.log1p(jnp.exp(-jnp.abs(logit)))

    @pl.when(i % tiles_per_seq == 0)
    def _():
        carry_ref[...] = jnp.zeros_like(carry_ref)

    r = lax.broadcasted_iota(jnp.int32, (CUM_CHUNK, CUM_CHUNK), 0)
    cidx = lax.broadcasted_iota(jnp.int32, (CUM_CHUNK, CUM_CHUNK), 1)
    tri = (cidx <= r).astype(BF16)
    carry = carry_ref[...]
    chunks = []
    for ch in range(tm // CUM_CHUNK):
        hi, mid, lo = _split3(logf[ch * CUM_CHUNK:(ch + 1) * CUM_CHUNK])
        cs = (jnp.dot(tri, hi, preferred_element_type=F32)
              + jnp.dot(tri, mid, preferred_element_type=F32)
              + jnp.dot(tri, lo, preferred_element_type=F32)) + carry
        chunks.append(cs * LOG2E)
        carry = cs[CUM_CHUNK - 1:CUM_CHUNK, :]
    carry_ref[...] = carry
    c2 = jnp.concatenate(chunks, axis=0)
    c_ref[...] = c2
    hi, mid, lo = _split3(-c2)
    kx = (jnp.dot(hi, sel_ref[0], preferred_element_type=F32)
          + jnp.dot(mid, sel_ref[1], preferred_element_type=F32)
          + jnp.dot(lo, sel_ref[2], preferred_element_type=F32))
    lane = lax.broadcasted_iota(jnp.int32, (tm, LANES), 1)
    low_half = lane < HEAD_DIM
    extra = [jnp.logical_and(lane >= HEAD_DIM, lane < HEAD_DIM + N_SPLIT), lane < N_SPLIT]

    cos = cos_ref[...]
    sa = sa_ref[...]
    sb = sb_ref[...]
    ones_row = (lax.broadcasted_iota(jnp.int32, (V_ROWS - LANES, TK), 0) == 0).astype(vt_ref.dtype)
    for src, kind, dst, scale in _PROJ_BLOCKS:
        acc = jnp.dot(xnb, w_ref[:, src:src + TN_PROJ], preferred_element_type=F32)
        if scale != 1.0:
            acc = acc * scale
        if kind == "rope":
            for s in range(TN_PROJ // LANES):
                t = acc[:, s * LANES:(s + 1) * LANES]
                rot = (t * cos + pltpu.roll(t, ROPE_HALF, 1) * sa
                       + pltpu.roll(t, LANES - ROPE_HALF, 1) * sb)
                qk_ref[:, dst + s * LANES:dst + (s + 1) * LANES] = rot.astype(qk_ref.dtype)
        elif kind in ("fox_q", "fox_k"):
            for pr in range(TN_PROJ // LANES):
                pair = acc[:, pr * LANES:(pr + 1) * LANES]
                for par in range(2):
                    head = 2 * pr + par
                    if kind == "fox_q":
                        other = extra[par].astype(F32)
                    else:
                        first = HEAD_DIM if par == 0 else 0
                        moved = pltpu.roll(kx, (first - N_SPLIT * head) % LANES, 1)
                        other = jnp.where(extra[par], moved, 0.0)
                    keep = low_half if par == 0 else jnp.logical_not(low_half)
                    qk_ref[:, dst + head * LANES:dst + (head + 1) * LANES] = (
                        jnp.where(keep, pair, other).astype(qk_ref.dtype))
        else:
            at = acc.T
            for hd in range(TN_PROJ // LANES):
                vt_ref[0, dst + hd, 0, 0:LANES, :] = at[hd * LANES:(hd + 1) * LANES, :].astype(vt_ref.dtype)
                vt_ref[0, dst + hd, 0, LANES:V_ROWS, :] = ones_row


def _proj_call(h, g, w_main, w_f, f_bias, sel, cos_t, sa_t, sb_t, batch, seq):
    t, d = h.shape
    tm = TM_PROJ
    assert tm == TK
    tiles_per_seq = seq // tm
    n_col = w_main.shape[1]
    n_vh = (len(_PROJ_BLOCKS) - N_QK_BLK) * (TN_PROJ // LANES)
    kern = functools.partial(_proj_kernel, tiles_per_seq=tiles_per_seq)
    const2 = lambda i: (0, 0)
    return pl.pallas_call(
        kern,
        grid=(t // tm,),
        in_specs=[
            pl.BlockSpec((tm, d), lambda i: (i, 0)),
            pl.BlockSpec((1, d), const2),
            pl.BlockSpec((d, n_col), const2),
            pl.BlockSpec((d, LANES), const2),
            pl.BlockSpec((1, LANES), const2),
            pl.BlockSpec((N_SPLIT, LANES, LANES), lambda i: (0, 0, 0)),
            pl.BlockSpec((tm, LANES), lambda i: (i, 0)),
            pl.BlockSpec((tm, LANES), lambda i: (i, 0)),
            pl.BlockSpec((tm, LANES), lambda i: (i, 0)),
        ],
        out_specs=[
            pl.BlockSpec((tm, QK_COLS), lambda i: (i, 0)),
            pl.BlockSpec((1, n_vh, 1, V_ROWS, TK), lambda i: (i // tiles_per_seq, 0, i % tiles_per_seq, 0, 0)),
            pl.BlockSpec((tm, LANES), lambda i: (i, 0)),
        ],
        out_shape=[
            jax.ShapeDtypeStruct((t, QK_COLS), BF16),
            jax.ShapeDtypeStruct((batch, n_vh, seq // TK, V_ROWS, TK), BF16),
            jax.ShapeDtypeStruct((t, LANES), F32),
        ],
        scratch_shapes=[pltpu.VMEM((1, LANES), F32)],
        compiler_params=_params(("arbitrary",)),
    )(h, g, w_main, w_f, f_bias, sel, cos_t, sa_t, sb_t)


def _head_mask(q, half):
    lane = lax.broadcasted_iota(jnp.int32, q.shape, 1)
    return jnp.where(lane // HEAD_DIM == half, q, jnp.zeros_like(q))


def _causal_mask():
    kpos = lax.broadcasted_iota(jnp.int32, (TK, TQ), 0)
    qpos = lax.broadcasted_iota(jnp.int32, (TK, TQ), 1)
    return kpos <= qpos


def _scores_t(k, q):
    return lax.dot_general(k, q, (((1,), (1,)), ((), ())), preferred_element_type=F32)


def _online_update(u, vt, shift, acc_ref, m_ref):
    mu = jnp.max(u, axis=0, keepdims=True) + shift
    m_old = m_ref[...]
    m_new = jnp.maximum(m_old, mu)
    p = jnp.exp2(u - (m_new - shift)).astype(BF16)
    pv = jnp.dot(vt, p, preferred_element_type=F32)
    acc_ref[...] = jnp.exp2(m_old - m_new) * acc_ref[...] + pv
    m_ref[...] = m_new


def _normalised(acc_ref):
    acc = acc_ref[...]
    return acc[:LANES] * (1.0 / acc[LANES:LANES + 1])


N_TILES = 4
UNROLL = 4


def _attn_scratch():
    return ([pltpu.VMEM((TK, TQ), F32)] * (3 * N_TILES) + [pltpu.VMEM((V_ROWS, TQ), F32)] * N_TILES
            + [pltpu.VMEM((1, TQ), F32)] * N_TILES)


def _attend(qi, qk_fn, qk_next_fn, vt_fn, shifts, scratch, finish):
    assert TQ == TK
    s_a, s_b, s_c = (scratch[n * N_TILES:(n + 1) * N_TILES] for n in range(3))
    acc = scratch[3 * N_TILES:4 * N_TILES]
    m = scratch[4 * N_TILES:]
    for t in range(N_TILES):
        m[t][...] = jnp.full(m[t].shape, -jnp.inf, F32)
        acc[t][...] = jnp.zeros(acc[t].shape, F32)

    def stage(j, cur, nxt, masked):
        for t in range(N_TILES):
            nxt[t][...] = qk_next_fn(t) if masked else qk_fn(j + 1, t)
            u = cur[t][...]
            if masked:
                u = jnp.where(_causal_mask(), u, -jnp.inf)
            _online_update(u, vt_fn(j, t), shifts[t], acc[t], m[t])

    @pl.when(qi == 0)
    def _():
        for t in range(N_TILES):
            s_a[t][...] = qk_fn(0, t)
        stage(0, s_a, s_c, True)
        finish(acc)

    @pl.when(qi > 0)
    def _():
        stage(0, s_c, s_a, False)
        bufs = (s_a, s_b)

        def body(i, carry):
            for n in range(UNROLL):
                stage(UNROLL * i + 1 + n, bufs[n % 2], bufs[(n + 1) % 2], False)
            return carry

        lax.fori_loop(0, (qi - 1) // UNROLL, body, 0)

        for rem in range(UNROLL):
            @pl.when((qi - 1) % UNROLL == rem)
            def _(rem=rem):
                for n in range(rem):
                    stage(qi - rem + n, bufs[n % 2], bufs[(n + 1) % 2], False)
                stage(qi, bufs[rem % 2], s_c, True)
                finish(acc)


def _fox_kernel(q_ref, qn_ref, k_ref, vt_ref, ct_ref, o_ref, *scratch):
    grp = pl.program_id(1)
    qi = pl.program_id(2)
    cqs = [ct_ref[0, pl.ds(N_TILES * grp + t, 1), :] for t in range(N_TILES)]

    def scores(j, t, qr):
        start = pl.multiple_of(j * TK, TK)
        return _scores_t(k_ref[0, pl.ds(start, TK), t * LANES:(t + 1) * LANES],
                         qr[0, :, t * LANES:(t + 1) * LANES])

    def vt_fn(j, t):
        return vt_ref[0, t // 2, j]

    def finish(acc):
        for pp in range(2):
            ot = jnp.concatenate([_normalised(acc[2 * pp])[:HEAD_DIM],
                                  _normalised(acc[2 * pp + 1])[HEAD_DIM:]], axis=0)
            o_ref[0, :, pp * LANES:(pp + 1) * LANES] = ot.T.astype(o_ref.dtype)

    _attend(qi, lambda j, t: scores(j, t, q_ref), lambda t: scores(0, t, qn_ref), vt_fn, cqs, scratch, finish)


def _fox_call(qk, vt, ct, batch, seq):
    nkb = seq // TK
    nq = seq // TQ
    w4 = N_TILES * LANES
    return pl.pallas_call(
        _fox_kernel,
        grid=(batch, N_FOX_HEADS // N_TILES, nq),
        in_specs=[
            pl.BlockSpec((1, TQ, w4), lambda b, g, qi: (b, qi, 2 + g)),
            pl.BlockSpec((1, TQ, w4), lambda b, g, qi: (b, jnp.minimum(qi + 1, nq - 1), 2 + g)),
            pl.BlockSpec((1, seq, w4), lambda b, g, qi: (b, 0, 4 + g)),
            pl.BlockSpec((1, 2, nkb, V_ROWS, TK), lambda b, g, qi: (b, N_DIFF_HEADS // 2 + g, 0, 0, 0)),
            pl.BlockSpec((1, N_FOX_HEADS, TQ), lambda b, g, qi: (b, 0, qi)),
        ],
        out_specs=pl.BlockSpec((1, TQ, 2 * LANES), lambda b, g, qi: (b, qi, g)),
        out_shape=jax.ShapeDtypeStruct((batch, seq, D_FOX), BF16),
        scratch_shapes=_attn_scratch(),
        compiler_params=_params(("arbitrary", "arbitrary", "arbitrary")),
    )(qk, qk, qk, vt, ct)


def _diff_kernel(lam_ref, q1_ref, q2_ref, q1n_ref, q2n_ref, k1_ref, k2_ref, vt_ref, g_ref, o_ref, *scratch,
                 out_scale):
    qi = pl.program_id(2)
    k_refs = [k1_ref, k2_ref]
    qms = [_head_mask(r[0], hh) for hh in range(2) for r in (q1_ref, q2_ref)]
    qms_next = [_head_mask(r[0], hh) for hh in range(2) for r in (q1n_ref, q2n_ref)]

    def scores(j, t, qm):
        start = pl.multiple_of(j * TK, TK)
        return _scores_t(k_refs[t % 2][0, pl.ds(start, TK), :], qm[t])

    def vt_fn(j, t):
        return vt_ref[0, t // 2, j]

    def finish(acc):
        for hh in range(2):
            ot = _normalised(acc[2 * hh]) - lam_ref[0] * _normalised(acc[2 * hh + 1])
            y = ot * lax.rsqrt(jnp.mean(ot * ot, axis=0, keepdims=True) + SUBLN_EPS)
            y = (y * g_ref[...]) * out_scale
            o_ref[0, :, hh * LANES:(hh + 1) * LANES] = y.T.astype(o_ref.dtype)

    _attend(qi, lambda j, t: scores(j, t, qms), lambda t: scores(0, t, qms_next), vt_fn,
            [0.0] * N_TILES, scratch, finish)


def _diff_call(lam, qk, vt, g_col, out_scale, batch, seq):
    nkb = seq // TK
    nq = seq // TQ
    n_pairs = N_DIFF_HEADS // 2
    kern = functools.partial(_diff_kernel, out_scale=out_scale)
    grid_spec = pltpu.PrefetchScalarGridSpec(
        num_scalar_prefetch=1,
        grid=(batch, n_pairs, nq),
        in_specs=[
            pl.BlockSpec((1, TQ, LANES), lambda b, p, qi, lam: (b, qi, p)),
            pl.BlockSpec((1, TQ, LANES), lambda b, p, qi, lam: (b, qi, 2 + p)),
            pl.BlockSpec((1, TQ, LANES), lambda b, p, qi, lam: (b, jnp.minimum(qi + 1, nq - 1), p)),
            pl.BlockSpec((1, TQ, LANES), lambda b, p, qi, lam: (b, jnp.minimum(qi + 1, nq - 1), 2 + p)),
            pl.BlockSpec((1, seq, LANES), lambda b, p, qi, lam: (b, 0, 4 + p)),
            pl.BlockSpec((1, seq, LANES), lambda b, p, qi, lam: (b, 0, 6 + p)),
            pl.BlockSpec((1, 2, nkb, V_ROWS, TK), lambda b, p, qi, lam: (b, p, 0, 0, 0)),
            pl.BlockSpec((DIFF_V_DIM, 1), lambda b, p, qi, lam: (0, 0)),
        ],
        out_specs=pl.BlockSpec((1, TQ, 2 * LANES), lambda b, p, qi, lam: (b, qi, p)),
        scratch_shapes=_attn_scratch(),
    )
    return pl.pallas_call(
        kern,
        grid_spec=grid_spec,
        out_shape=jax.ShapeDtypeStruct((batch, seq, D_DIFF_OUT), BF16),
        compiler_params=_params(("arbitrary", "arbitrary", "arbitrary")),
    )(lam, qk, qk, qk, qk, qk, qk, vt, g_col)


def _norm_matmul_kernel(x_ref, g_ref, w_ref, o_ref):
    x = x_ref[...]
    xn = ((x * _rms_scale(x, EPS)) * g_ref[...]).astype(BF16)
    o_ref[...] = jnp.dot(xn, w_ref[...], preferred_element_type=F32).astype(o_ref.dtype)


def _norm_matmul_call(x, g, w, tm, tn):
    t, d = x.shape
    n = w.shape[1]
    return pl.pallas_call(
        _norm_matmul_kernel,
        grid=(t // tm, n // tn),
        in_specs=[pl.BlockSpec((tm, d), lambda i, j: (i, 0)),
                  pl.BlockSpec((1, d), lambda i, j: (0, 0)),
                  pl.BlockSpec((d, tn), lambda i, j: (0, j))],
        out_specs=pl.BlockSpec((tm, tn), lambda i, j: (i, j)),
        out_shape=jax.ShapeDtypeStruct((t, n), BF16),
        compiler_params=_params(("arbitrary", "arbitrary")),
    )(x, g, w)


def _mix_mem_kernel(h_ref, od_ref, of_ref, wod_ref, wof_ref, g_ref, wq_ref, k_ref, v_ref, wo_ref,
                    o_ref, *, mem_scale):
    h1 = (h_ref[...]
          + jnp.dot(od_ref[...], wod_ref[...], preferred_element_type=F32)
          + jnp.dot(of_ref[...], wof_ref[...], preferred_element_type=F32))
    xn = ((h1 * _rms_scale(h1, EPS)) * g_ref[...]).astype(BF16)
    q = jnp.dot(xn, wq_ref[...], preferred_element_type=F32).astype(BF16)
    d = q.shape[1]
    hd = d // MEM_HEADS
    outs = []
    for hh in range(MEM_HEADS):
        qh = q[:, hh * hd:(hh + 1) * hd]
        kh = k_ref[:, hh * hd:(hh + 1) * hd]
        vh = v_ref[:, hh * hd:(hh + 1) * hd]
        s = lax.dot_general(qh, kh, (((1,), (1,)), ((), ())),
                            preferred_element_type=F32) * mem_scale
        m = jnp.max(s, axis=-1, keepdims=True)
        e = jnp.exp(s - m)
        p = e * (1.0 / jnp.sum(e, axis=-1, keepdims=True))
        outs.append(jnp.dot(p.astype(BF16), vh, preferred_element_type=F32).astype(BF16))
    o = jnp.concatenate(outs, axis=1)
    o_ref[...] = h1 + jnp.dot(o, wo_ref[...], preferred_element_type=F32)


def _mix_mem_call(h, o_d, o_f, w_od, w_of, g, w_q, kv, w_o, seq, n_mem):
    t, d = h.shape
    tm = TM_MIX
    tiles_per_seq = seq // tm
    kern = functools.partial(_mix_mem_kernel, mem_scale=1.0 / math.sqrt(d // MEM_HEADS))
    const = lambda i: (0, 0)
    return pl.pallas_call(
        kern,
        grid=(t // tm,),
        in_specs=[
            pl.BlockSpec((tm, d), lambda i: (i, 0)),
            pl.BlockSpec((tm, D_DIFF_OUT), lambda i: (i, 0)),
            pl.BlockSpec((tm, D_FOX), lambda i: (i, 0)),
            pl.BlockSpec((D_DIFF_OUT, d), const),
            pl.BlockSpec((D_FOX, d), const),
            pl.BlockSpec((1, d), const),
            pl.BlockSpec((d, d), const),
            pl.BlockSpec((n_mem, d), lambda i: (i // tiles_per_seq, 0)),
            pl.BlockSpec((n_mem, d), lambda i: (i // tiles_per_seq, 1)),
            pl.BlockSpec((d, d), const),
        ],
        out_specs=pl.BlockSpec((tm, d), lambda i: (i, 0)),
        out_shape=jax.ShapeDtypeStruct((t, d), F32),
        compiler_params=_params(("arbitrary",)),
    )(h, o_d, o_f, w_od, w_of, g, w_q, kv, kv, w_o)


def _swiglu_chunk(x, w1, w3, w2):
    a = jnp.dot(x, w1, preferred_element_type=F32)
    b = jnp.dot(x, w3, preferred_element_type=F32)
    mid = (a * (1.0 / (1.0 + jnp.exp(-a)))) * b
    return jnp.dot(mid.astype(BF16), w2, preferred_element_type=F32)


def _dense_ffn_kernel(h_ref, g_ref, w1_ref, w3_ref, w2_ref, o_ref, xn_ref):
    f = pl.program_id(1)

    @pl.when(f == 0)
    def _():
        x = h_ref[...]
        xn_ref[...] = ((x * _rms_scale(x, EPS)) * g_ref[...]).astype(BF16)
        o_ref[...] = x

    o_ref[...] += _swiglu_chunk(xn_ref[...], w1_ref[...], w3_ref[...], w2_ref[...])


def _dense_ffn_call(h, g, w1, w3, w2):
    t, d = h.shape
    tm, tf = TM_FFN, TF_DENSE
    dff = w1.shape[1]
    return pl.pallas_call(
        _dense_ffn_kernel,
        grid=(t // tm, dff // tf),
        in_specs=[pl.BlockSpec((tm, d), lambda i, f: (i, 0)),
                  pl.BlockSpec((1, d), lambda i, f: (0, 0)),
                  pl.BlockSpec((d, tf), lambda i, f: (0, f)),
                  pl.BlockSpec((d, tf), lambda i, f: (0, f)),
                  pl.BlockSpec((tf, d), lambda i, f: (f, 0))],
        out_specs=pl.BlockSpec((tm, d), lambda i, f: (i, 0)),
        out_shape=jax.ShapeDtypeStruct((t, d), F32),
        scratch_shapes=[pltpu.VMEM((tm, d), BF16)],
        compiler_params=_params(("arbitrary", "arbitrary")),
    )(h, g, w1, w3, w2)


def _router_kernel(h_ref, g_ref, wr_ref, hn_ref, idx_ref, gate_ref):
    x = h_ref[...]
    xn = ((x * _rms_scale(x, EPS)) * g_ref[...]).astype(BF16)
    hn_ref[...] = xn
    logits = jnp.dot(xn, wr_ref[...], preferred_element_type=F32)
    lane = lax.broadcasted_iota(jnp.int32, logits.shape, 1)
    logits = jnp.where(lane < N_EXPERTS, logits, -jnp.inf)
    v1 = jnp.max(logits, axis=-1, keepdims=True)
    i1 = jnp.min(jnp.where(logits == v1, lane, LANES), axis=-1, keepdims=True)
    rest = jnp.where(lane == i1, -jnp.inf, logits)
    v2 = jnp.max(rest, axis=-1, keepdims=True)
    i2 = jnp.min(jnp.where(rest == v2, lane, LANES), axis=-1, keepdims=True)
    e = jnp.exp(v2 - v1)
    inv = 1.0 / (1.0 + e)
    idx_ref[...] = jnp.where(lane == 0, i1, jnp.where(lane == 1, i2, 0))
    gate_ref[...] = jnp.where(lane == 0, inv, jnp.where(lane == 1, e * inv, 0.0))


def _router_call(h, g, w_r):
    t, d = h.shape
    tm = TM_FFN
    return pl.pallas_call(
        _router_kernel,
        grid=(t // tm,),
        in_specs=[pl.BlockSpec((tm, d), lambda i: (i, 0)),
                  pl.BlockSpec((1, d), lambda i: (0, 0)),
                  pl.BlockSpec((d, LANES), lambda i: (0, 0))],
        out_specs=[pl.BlockSpec((tm, d), lambda i: (i, 0)),
                   pl.BlockSpec((tm, LANES), lambda i: (i, 0)),
                   pl.BlockSpec((tm, LANES), lambda i: (i, 0))],
        out_shape=[jax.ShapeDtypeStruct((t, d), BF16),
                   jax.ShapeDtypeStruct((t, LANES), jnp.int32),
                   jax.ShapeDtypeStruct((t, LANES), F32)],
        compiler_params=_params(("arbitrary",)),
    )(h, g, w_r)


def _expert_ffn_kernel(te_ref, nu_ref, x_ref, w1_ref, w3_ref, w2_ref, o_ref, acc_ref, *, n_chunks):
    i = pl.program_id(0)
    f = pl.program_id(1)

    @pl.when(i < nu_ref[0])
    def _():
        @pl.when(f == 0)
        def _():
            acc_ref[...] = jnp.zeros_like(acc_ref)

        acc_ref[...] += _swiglu_chunk(x_ref[...], w1_ref[0, 0].astype(BF16), w3_ref[0, 0].astype(BF16),
                                      w2_ref[0, 0].astype(BF16))

        @pl.when(f == n_chunks - 1)
        def _():
            o_ref[...] = acc_ref[...].astype(o_ref.dtype)


def _expert_ffn_call(tile_expert, n_used, x_sorted, w1, w3, w2, layer):
    p_rows, d = x_sorted.shape
    tm, tf = TM_FFN, TF_MOE
    dff = w1.shape[3]
    n_chunks = dff // tf
    kern = functools.partial(_expert_ffn_kernel, n_chunks=n_chunks)

    def row_map(i, f, te, nu):
        return (jnp.minimum(i, nu[0] - 1), 0)

    def chunk(i, f, nu):
        return jnp.where(i < nu[0], f, n_chunks - 1)

    grid_spec = pltpu.PrefetchScalarGridSpec(
        num_scalar_prefetch=2,
        grid=(p_rows // tm, n_chunks),
        in_specs=[
            pl.BlockSpec((tm, d), row_map),
            pl.BlockSpec((1, 1, d, tf), lambda i, f, te, nu: (layer, te[i], 0, chunk(i, f, nu))),
            pl.BlockSpec((1, 1, d, tf), lambda i, f, te, nu: (layer, te[i], 0, chunk(i, f, nu))),
            pl.BlockSpec((1, 1, tf, d), lambda i, f, te, nu: (layer, te[i], chunk(i, f, nu), 0)),
        ],
        out_specs=pl.BlockSpec((tm, d), row_map),
        scratch_shapes=[pltpu.VMEM((tm, d), F32)],
    )
    return pl.pallas_call(
        kern,
        grid_spec=grid_spec,
        out_shape=jax.ShapeDtypeStruct((p_rows, d), BF16),
        compiler_params=_params(("arbitrary", "arbitrary")),
    )(tile_expert, n_used, x_sorted, w1, w3, w2)


def _route(idx, tm):
    t = idx.shape[0]
    n_assign = 2 * t
    p_rows = n_assign + N_EXPERTS * tm
    e_flat = idx.reshape(-1)
    onehot = (e_flat[:, None] == jnp.arange(N_EXPERTS, dtype=jnp.int32)[None, :]).astype(jnp.int32)
    running = jnp.cumsum(onehot, axis=0)
    counts = running[-1]
    rank = jnp.sum(running * onehot, axis=1) - 1
    padded = ((counts + tm - 1) // tm) * tm
    pend = jnp.cumsum(padded)
    pstart = pend - padded
    gstart = jnp.cumsum(counts) - counts
    dest = pstart[e_flat] + rank
    n_tiles = p_rows // tm
    n_used = (pend[-1] // tm).astype(jnp.int32)
    tile_start = jnp.arange(n_tiles, dtype=jnp.int32) * tm
    tile_expert = jnp.sum((tile_start[:, None] >= pend[None, :]).astype(jnp.int32), axis=1)
    last_expert = jnp.sum((((n_used - 1) * tm) >= pend).astype(jnp.int32))
    tile_expert = jnp.where(tile_start < pend[-1], tile_expert, last_expert).astype(jnp.int32)
    order = jnp.sort(e_flat * n_assign + jnp.arange(n_assign, dtype=jnp.int32)) % n_assign
    row_expert = jnp.repeat(tile_expert, tm)
    within = jnp.arange(p_rows, dtype=jnp.int32) - pstart[row_expert]
    src = jnp.clip(gstart[row_expert] + within, 0, n_assign - 1)
    row_token = jnp.where(within < counts[row_expert], order[src] // 2, 0)
    return row_token, dest.reshape(t, 2), tile_expert, n_used.reshape(1)


def _moe_ffn(h, g, w_r, w1, w3, w2, layer):
    hn, idx_l, gate_l = _router_call(h, g, w_r)
    row_token, slot, tile_expert, n_used = _route(idx_l[:, :2], TM_FFN)
    x_sorted = jnp.take(hn, row_token, axis=0)
    y = _expert_ffn_call(tile_expert, n_used, x_sorted, w1, w3, w2, layer)
    return jnp.take(y, slot[:, 0], axis=0), jnp.take(y, slot[:, 1], axis=0), gate_l


def _final_norm_kernel(x_ref, g_ref, o_ref):
    x = x_ref[...]
    o_ref[...] = (x * _rms_scale(x, EPS)) * g_ref[...]


def _combine_norm_kernel(h_ref, y0_ref, y1_ref, gate_ref, g_ref, o_ref):
    x = (h_ref[...] + gate_ref[:, 0:1] * y0_ref[...].astype(F32)
         + gate_ref[:, 1:2] * y1_ref[...].astype(F32))
    o_ref[...] = (x * _rms_scale(x, EPS)) * g_ref[...]


def _final_norm_call(h, g, moe_parts=None):
    t, d = h.shape
    tm = TM_FFN
    tile = pl.BlockSpec((tm, d), lambda i: (i, 0))
    gain = pl.BlockSpec((1, d), lambda i: (0, 0))
    if moe_parts is None:
        kern, in_specs, args = _final_norm_kernel, [tile, gain], (h, g)
    else:
        y0, y1, gates = moe_parts
        kern = _combine_norm_kernel
        in_specs = [tile, tile, tile, pl.BlockSpec((tm, LANES), lambda i: (i, 0)), gain]
        args = (h, y0, y1, gates, g)
    return pl.pallas_call(
        kern,
        grid=(t // tm,),
        in_specs=in_specs,
        out_specs=tile,
        out_shape=jax.ShapeDtypeStruct((t, d), F32),
        compiler_params=_params(("arbitrary",)),
    )(*args)


def _rope_tables(positions):
    inv_freq = ROPE_THETA ** (-jnp.arange(0, ROPE_DIM, 2, dtype=F32) / ROPE_DIM)
    b, s = positions.shape
    dim = jnp.arange(LANES, dtype=jnp.int32) % HEAD_DIM
    ang = positions.astype(F32).reshape(b * s, 1) * inv_freq[None, :]
    cos = jnp.tile(jnp.cos(ang), (1, LANES // ROPE_HALF))
    sin = jnp.tile(jnp.sin(ang), (1, LANES // ROPE_HALF))
    first = (dim < ROPE_HALF)[None, :]
    second = jnp.logical_and(dim >= ROPE_HALF, dim < ROPE_DIM)[None, :]
    cos_t = jnp.where(jnp.logical_or(first, second), cos, 1.0)
    sa_t = jnp.where(second, sin, 0.0)
    sb_t = jnp.where(first, -sin, 0.0)
    return cos_t, sa_t, sb_t


def _decay_selectors():
    src = jnp.arange(LANES, dtype=jnp.int32)[:, None]
    dst = jnp.arange(LANES, dtype=jnp.int32)[None, :]
    sels = []
    for s in range(N_SPLIT):
        hit = jnp.logical_and(src < N_FOX_HEADS, dst == N_SPLIT * src + s)
        sels.append(hit.astype(BF16))
    return jnp.stack(sels)


def _split_w_in(w):
    w_f = jnp.pad(w[:, O_F:O_F + N_FOX_HEADS], ((0, 0), (0, LANES - N_FOX_HEADS))).astype(BF16)
    return w.astype(BF16), w_f


def kernel(x, mem, positions, norm_mix_g, w_in, lam_q1, lam_k1, lam_q2, lam_k2, diff_subln_g, fox_bias, w_out, norm_mem_g, mem_norm_g, w_mq, w_mkv, w_mo, norm_ffn_g, w_ff1, w_ff3, w_ff2, w_router, w_e1, w_e3, w_e2, final_norm_g):
    batch, seq, d = x.shape
    n_mem = mem.shape[1]
    depth = w_in.shape[0]
    t = batch * seq
    cos_t, sa_t, sb_t = _rope_tables(positions)
    sel = _decay_selectors()
    h = x.reshape(t, d)
    mem2 = mem.reshape(batch * n_mem, d)
    row = lambda v: v.reshape(1, -1).astype(F32)

    moe_parts = None
    for l in range(depth):
        lam_init = 0.8 - 0.6 * math.exp(-0.3 * l)
        lam = (jnp.exp(jnp.sum(lam_q1[l] * lam_k1[l])) - jnp.exp(jnp.sum(lam_q2[l] * lam_k2[l]))
               + lam_init).reshape(1).astype(F32)
        w_main, w_f = _split_w_in(w_in[l])
        f_bias = jnp.pad(fox_bias[l], (0, LANES - N_FOX_HEADS)).reshape(1, LANES)

        qk, vt, c = _proj_call(h, row(norm_mix_g[l]), w_main, w_f, f_bias, sel, cos_t, sa_t, sb_t, batch, seq)
        qk3 = qk.reshape(batch, seq, qk.shape[1])
        ct = jnp.transpose(c.reshape(batch, seq, LANES)[:, :, :N_FOX_HEADS], (0, 2, 1))
        o_d = _diff_call(lam, qk3, vt, diff_subln_g[l].reshape(DIFF_V_DIM, 1), 1.0 - lam_init, batch, seq)
        o_f = _fox_call(qk3, vt, ct, batch, seq)

        w_o = w_out[l].astype(BF16)
        kv = _norm_matmul_call(mem2, row(mem_norm_g[l]), w_mkv[l].astype(BF16), 512, 1024)
        h = _mix_mem_call(h, o_d.reshape(t, D_DIFF_OUT), o_f.reshape(t, D_FOX),
                          w_o[:D_DIFF_OUT], w_o[D_DIFF_OUT:], row(norm_mem_g[l]),
                          w_mq[l].astype(BF16), kv, w_mo[l].astype(BF16), seq, n_mem)

        i = l // 2
        if l % 2 == 0:
            h = _dense_ffn_call(h, row(norm_ffn_g[l]), w_ff1[i].astype(BF16), w_ff3[i].astype(BF16),
                                w_ff2[i].astype(BF16))
        else:
            w_r = jnp.pad(w_router[i], ((0, 0), (0, LANES - N_EXPERTS))).astype(BF16)
            y0, y1, gates = _moe_ffn(h, row(norm_ffn_g[l]), w_r, w_e1, w_e3, w_e2, i)
            if l == depth - 1:
                moe_parts = (y0, y1, gates)
            else:
                h = h + gates[:, 0:1] * y0.astype(F32) + gates[:, 1:2] * y1.astype(F32)

    return _final_norm_call(h, row(final_norm_g), moe_parts).reshape(batch, seq, d)
```

```python
import functools
import math

import jax
import jax.numpy as jnp
from jax import lax
from jax.experimental import pallas as pl
from jax.experimental.pallas import tpu as pltpu

F32 = jnp.float32
BF16 = jnp.bfloat16

HEAD_DIM = 64
N_DIFF_HEADS = 4
DIFF_V_DIM = 128
N_FOX_HEADS = 8
D_DIFF_QK = N_DIFF_HEADS * HEAD_DIM
D_DIFF_OUT = N_DIFF_HEADS * DIFF_V_DIM
D_FOX = N_FOX_HEADS * HEAD_DIM
ATTN_SCALE = 1.0 / math.sqrt(HEAD_DIM)
ROPE_DIM = HEAD_DIM // 4
ROPE_HALF = ROPE_DIM // 2
ROPE_THETA = 500000.0
MEM_HEADS = 4
N_EXPERTS = 8
EPS = 1e-6
SUBLN_EPS = 1e-5
LOG2E = 1.0 / math.log(2.0)

LANES = 128
VMEM_LIMIT = 48 * 1024 * 1024

TM_PROJ = 512
TN_PROJ = 512
N_QK_BLK = 4
TQ = 512
TK = 512
TM_MIX = 512
TM_FFN = 1024
TF_DENSE = 256
TF_MOE = 512
MOE_SPLIT = 4
CUM_CHUNK = 256
V_ROWS = LANES + 8
N_SPLIT = 3


def _params(sem):
    return pltpu.CompilerParams(dimension_semantics=sem, vmem_limit_bytes=VMEM_LIMIT)


def _rms_scale(x, eps):
    return lax.rsqrt(jnp.mean(x * x, axis=-1, keepdims=True) + eps)


def _split3(v):
    hi = v.astype(BF16)
    r1 = v - hi.astype(F32)
    mid = r1.astype(BF16)
    lo = (r1 - mid.astype(F32)).astype(BF16)
    return hi, mid, lo


O_VD = 4 * D_DIFF_QK
O_QF = O_VD + D_DIFF_OUT
O_KF = O_QF + D_FOX
O_VF = O_KF + D_FOX
O_F = O_VF + D_FOX
Q_SCALE = ATTN_SCALE * LOG2E
QK_COLS = 2 * TN_PROJ + 2 * N_FOX_HEADS * LANES
_PROJ_BLOCKS = (
    (0, "rope", 0, Q_SCALE),
    (2 * D_DIFF_QK, "rope", TN_PROJ, 1.0),
    (O_QF, "fox_q", 2 * TN_PROJ, Q_SCALE),
    (O_KF, "fox_k", 2 * TN_PROJ + N_FOX_HEADS * LANES, 1.0),
    (O_VD, "vt", 0, 1.0),
    (O_VF, "vt", N_DIFF_HEADS, 1.0),
)


def _proj_kernel(x_ref, g_ref, w_ref, wf_ref, fb_ref, sel_ref, cos_ref, sa_ref, sb_ref,
                 qk_ref, vt_ref, c_ref, carry_ref, *, tiles_per_seq):
    i = pl.program_id(0)
    tm = x_ref.shape[0]
    x = x_ref[...]
    xnb = ((x * _rms_scale(x, EPS)) * g_ref[...]).astype(BF16)

    logit = jnp.dot(xnb, wf_ref[...], preferred_element_type=F32) + fb_ref[...]
    logf = jnp.minimum(logit, 0.0) - jnp.log1p(jnp.exp(-jnp.abs(logit)))

    @pl.when(i % tiles_per_seq == 0)
    def _():
        carry_ref[...] = jnp.zeros_like(carry_ref)

    r = lax.broadcasted_iota(jnp.int32, (CUM_CHUNK, CUM_CHUNK), 0)
    cidx = lax.broadcasted_iota(jnp.int32, (CUM_CHUNK, CUM_CHUNK), 1)
    tri = (cidx <= r).astype(BF16)
    carry = carry_ref[...]
    chunks = []
    for ch in range(tm // CUM_CHUNK):
        hi, mid, lo = _split3(logf[ch * CUM_CHUNK:(ch + 1) * CUM_CHUNK])
        cs = (jnp.dot(tri, hi, preferred_element_type=F32)
              + jnp.dot(tri, mid, preferred_element_type=F32)
              + jnp.dot(tri, lo, preferred_element_type=F32)) + carry
        chunks.append(cs * LOG2E)
        carry = cs[CUM_CHUNK - 1:CUM_CHUNK, :]
    carry_ref[...] = carry
    c2 = jnp.concatenate(chunks, axis=0)
    c_ref[...] = c2
    hi, mid, lo = _split3(-c2)
    kx = (jnp.dot(hi, sel_ref[0], preferred_element_type=F32)
          + jnp.dot(mid, sel_ref[1], preferred_element_type=F32)
          + jnp.dot(lo, sel_ref[2], preferred_element_type=F32))
    lane = lax.broadcasted_iota(jnp.int32, (tm, LANES), 1)
    low_half = lane < HEAD_DIM
    extra = [jnp.logical_and(lane >= HEAD_DIM, lane < HEAD_DIM + N_SPLIT), lane < N_SPLIT]

    cos = cos_ref[...]
    sa = sa_ref[...]
    sb = sb_ref[...]
    ones_row = (lax.broadcasted_iota(jnp.int32, (V_ROWS - LANES, TK), 0) == 0).astype(vt_ref.dtype)
    for src, kind, dst, scale in _PROJ_BLOCKS:
        acc = jnp.dot(xnb, w_ref[:, src:src + TN_PROJ], preferred_element_type=F32)
        if scale != 1.0:
            acc = acc * scale
        if kind == "rope":
            for s in range(TN_PROJ // LANES):
                t = acc[:, s * LANES:(s + 1) * LANES]
                rot = (t * cos + pltpu.roll(t, ROPE_HALF, 1) * sa
                       + pltpu.roll(t, LANES - ROPE_HALF, 1) * sb)
                qk_ref[:, dst + s * LANES:dst + (s + 1) * LANES] = rot.astype(qk_ref.dtype)
        elif kind in ("fox_q", "fox_k"):
            for pr in range(TN_PROJ // LANES):
                pair = acc[:, pr * LANES:(pr + 1) * LANES]
                for par in range(2):
                    head = 2 * pr + par
                    if kind == "fox_q":
                        other = extra[par].astype(F32)
                    else:
                        first = HEAD_DIM if par == 0 else 0
                        moved = pltpu.roll(kx, (first - N_SPLIT * head) % LANES, 1)
                        other = jnp.where(extra[par], moved, 0.0)
                    keep = low_half if par == 0 else jnp.logical_not(low_half)
                    qk_ref[:, dst + head * LANES:dst + (head + 1) * LANES] = (
                        jnp.where(keep, pair, other).astype(qk_ref.dtype))
        else:
            at = acc.T
            for hd in range(TN_PROJ // LANES):
                vt_ref[0, dst + hd, 0, 0:LANES, :] = at[hd * LANES:(hd + 1) * LANES, :].astype(vt_ref.dtype)
                vt_ref[0, dst + hd, 0, LANES:V_ROWS, :] = ones_row


def _proj_call(h, g, w_main, w_f, f_bias, sel, cos_t, sa_t, sb_t, batch, seq):
    t, d = h.shape
    tm = TM_PROJ
    assert tm == TK
    tiles_per_seq = seq // tm
    n_col = w_main.shape[1]
    n_vh = (len(_PROJ_BLOCKS) - N_QK_BLK) * (TN_PROJ // LANES)
    kern = functools.partial(_proj_kernel, tiles_per_seq=tiles_per_seq)
    const2 = lambda i: (0, 0)
    return pl.pallas_call(
        kern,
        grid=(t // tm,),
        in_specs=[
            pl.BlockSpec((tm, d), lambda i: (i, 0)),
            pl.BlockSpec((1, d), const2),
            pl.BlockSpec((d, n_col), const2),
            pl.BlockSpec((d, LANES), const2),
            pl.BlockSpec((1, LANES), const2),
            pl.BlockSpec((N_SPLIT, LANES, LANES), lambda i: (0, 0, 0)),
            pl.BlockSpec((tm, LANES), lambda i: (i, 0)),
            pl.BlockSpec((tm, LANES), lambda i: (i, 0)),
            pl.BlockSpec((tm, LANES), lambda i: (i, 0)),
        ],
        out_specs=[
            pl.BlockSpec((tm, QK_COLS), lambda i: (i, 0)),
            pl.BlockSpec((1, n_vh, 1, V_ROWS, TK), lambda i: (i // tiles_per_seq, 0, i % tiles_per_seq, 0, 0)),
            pl.BlockSpec((tm, LANES), lambda i: (i, 0)),
        ],
        out_shape=[
            jax.ShapeDtypeStruct((t, QK_COLS), BF16),
            jax.ShapeDtypeStruct((batch, n_vh, seq // TK, V_ROWS, TK), BF16),
            jax.ShapeDtypeStruct((t, LANES), F32),
        ],
        scratch_shapes=[pltpu.VMEM((1, LANES), F32)],
        compiler_params=_params(("arbitrary",)),
    )(h, g, w_main, w_f, f_bias, sel, cos_t, sa_t, sb_t)


def _head_mask(q, half):
    lane = lax.broadcasted_iota(jnp.int32, q.shape, 1)
    return jnp.where(lane // HEAD_DIM == half, q, jnp.zeros_like(q))


def _causal_mask():
    kpos = lax.broadcasted_iota(jnp.int32, (TK, TQ), 0)
    qpos = lax.broadcasted_iota(jnp.int32, (TK, TQ), 1)
    return kpos <= qpos


def _scores_t(k, q):
    return lax.dot_general(k, q, (((1,), (1,)), ((), ())), preferred_element_type=F32)


def _online_update(u, vt, shift, acc_ref, m_ref):
    mu = jnp.max(u, axis=0, keepdims=True) + shift
    m_old = m_ref[...]
    m_new = jnp.maximum(m_old, mu)
    p = jnp.exp2(u - (m_new - shift)).astype(BF16)
    pv = jnp.dot(vt, p, preferred_element_type=F32)
    acc_ref[...] = jnp.exp2(m_old - m_new) * acc_ref[...] + pv
    m_ref[...] = m_new


def _normalised(acc_ref):
    acc = acc_ref[...]
    return acc[:LANES] * (1.0 / acc[LANES:LANES + 1])


N_TILES = 4
UNROLL = 4


def _attn_scratch():
    return ([pltpu.VMEM((TK, TQ), F32)] * (3 * N_TILES) + [pltpu.VMEM((V_ROWS, TQ), F32)] * N_TILES
            + [pltpu.VMEM((1, TQ), F32)] * N_TILES)


def _attend(qi, qk_fn, qk_next_fn, vt_fn, shifts, scratch, finish):
    assert TQ == TK
    s_a, s_b, s_c = (scratch[n * N_TILES:(n + 1) * N_TILES] for n in range(3))
    acc = scratch[3 * N_TILES:4 * N_TILES]
    m = scratch[4 * N_TILES:]
    for t in range(N_TILES):
        m[t][...] = jnp.full(m[t].shape, -jnp.inf, F32)
        acc[t][...] = jnp.zeros(acc[t].shape, F32)

    def stage(j, cur, nxt, masked):
        for t in range(N_TILES):
            nxt[t][...] = qk_next_fn(t) if masked else qk_fn(j + 1, t)
            u = cur[t][...]
            if masked:
                u = jnp.where(_causal_mask(), u, -jnp.inf)
            _online_update(u, vt_fn(j, t), shifts[t], acc[t], m[t])

    @pl.when(qi == 0)
    def _():
        for t in range(N_TILES):
            s_a[t][...] = qk_fn(0, t)
        stage(0, s_a, s_c, True)
        finish(acc)

    @pl.when(qi > 0)
    def _():
        stage(0, s_c, s_a, False)
        bufs = (s_a, s_b)

        def body(i, carry):
            for n in range(UNROLL):
                stage(UNROLL * i + 1 + n, bufs[n % 2], bufs[(n + 1) % 2], False)
            return carry

        lax.fori_loop(0, (qi - 1) // UNROLL, body, 0)

        for rem in range(UNROLL):
            @pl.when((qi - 1) % UNROLL == rem)
            def _(rem=rem):
                for n in range(rem):
                    stage(qi - rem + n, bufs[n % 2], bufs[(n + 1) % 2], False)
                stage(qi, bufs[rem % 2], s_c, True)
                finish(acc)


def _fox_kernel(q_ref, qn_ref, k_ref, vt_ref, ct_ref, o_ref, *scratch):
    grp = pl.program_id(1)
    qi = pl.program_id(2)
    cqs = [ct_ref[0, pl.ds(N_TILES * grp + t, 1), :] for t in range(N_TILES)]

    def scores(j, t, qr):
        start = pl.multiple_of(j * TK, TK)
        return _scores_t(k_ref[0, pl.ds(start, TK), t * LANES:(t + 1) * LANES],
                         qr[0, :, t * LANES:(t + 1) * LANES])

    def vt_fn(j, t):
        return vt_ref[0, t // 2, j]

    def finish(acc):
        for pp in range(2):
            ot = jnp.concatenate([_normalised(acc[2 * pp])[:HEAD_DIM],
                                  _normalised(acc[2 * pp + 1])[HEAD_DIM:]], axis=0)
            o_ref[0, :, pp * LANES:(pp + 1) * LANES] = ot.T.astype(o_ref.dtype)

    _attend(qi, lambda j, t: scores(j, t, q_ref), lambda t: scores(0, t, qn_ref), vt_fn, cqs, scratch, finish)


def _fox_call(qk, vt, ct, batch, seq):
    nkb = seq // TK
    nq = seq // TQ
    w4 = N_TILES * LANES
    return pl.pallas_call(
        _fox_kernel,
        grid=(batch, N_FOX_HEADS // N_TILES, nq),
        in_specs=[
            pl.BlockSpec((1, TQ, w4), lambda b, g, qi: (b, qi, 2 + g)),
            pl.BlockSpec((1, TQ, w4), lambda b, g, qi: (b, jnp.minimum(qi + 1, nq - 1), 2 + g)),
            pl.BlockSpec((1, seq, w4), lambda b, g, qi: (b, 0, 4 + g)),
            pl.BlockSpec((1, 2, nkb, V_ROWS, TK), lambda b, g, qi: (b, N_DIFF_HEADS // 2 + g, 0, 0, 0)),
            pl.BlockSpec((1, N_FOX_HEADS, TQ), lambda b, g, qi: (b, 0, qi)),
        ],
        out_specs=pl.BlockSpec((1, TQ, 2 * LANES), lambda b, g, qi: (b, qi, g)),
        out_shape=jax.ShapeDtypeStruct((batch, seq, D_FOX), BF16),
        scratch_shapes=_attn_scratch(),
        compiler_params=_params(("arbitrary", "arbitrary", "arbitrary")),
    )(qk, qk, qk, vt, ct)


def _diff_kernel(lam_ref, q1_ref, q2_ref, q1n_ref, q2n_ref, k1_ref, k2_ref, vt_ref, g_ref, o_ref, *scratch,
                 out_scale):
    qi = pl.program_id(2)
    k_refs = [k1_ref, k2_ref]
    qms = [_head_mask(r[0], hh) for hh in range(2) for r in (q1_ref, q2_ref)]
    qms_next = [_head_mask(r[0], hh) for hh in range(2) for r in (q1n_ref, q2n_ref)]

    def scores(j, t, qm):
        start = pl.multiple_of(j * TK, TK)
        return _scores_t(k_refs[t % 2][0, pl.ds(start, TK), :], qm[t])

    def vt_fn(j, t):
        return vt_ref[0, t // 2, j]

    def finish(acc):
        for hh in range(2):
            ot = _normalised(acc[2 * hh]) - lam_ref[0] * _normalised(acc[2 * hh + 1])
            y = ot * lax.rsqrt(jnp.mean(ot * ot, axis=0, keepdims=True) + SUBLN_EPS)
            y = (y * g_ref[...]) * out_scale
            o_ref[0, :, hh * LANES:(hh + 1) * LANES] = y.T.astype(o_ref.dtype)

    _attend(qi, lambda j, t: scores(j, t, qms), lambda t: scores(0, t, qms_next), vt_fn,
            [0.0] * N_TILES, scratch, finish)


def _diff_call(lam, qk, vt, g_col, out_scale, batch, seq):
    nkb = seq // TK
    nq = seq // TQ
    n_pairs = N_DIFF_HEADS // 2
    kern = functools.partial(_diff_kernel, out_scale=out_scale)
    grid_spec = pltpu.PrefetchScalarGridSpec(
        num_scalar_prefetch=1,
        grid=(batch, n_pairs, nq),
        in_specs=[
            pl.BlockSpec((1, TQ, LANES), lambda b, p, qi, lam: (b, qi, p)),
            pl.BlockSpec((1, TQ, LANES), lambda b, p, qi, lam: (b, qi, 2 + p)),
            pl.BlockSpec((1, TQ, LANES), lambda b, p, qi, lam: (b, jnp.minimum(qi + 1, nq - 1), p)),
            pl.BlockSpec((1, TQ, LANES), lambda b, p, qi, lam: (b, jnp.minimum(qi + 1, nq - 1), 2 + p)),
            pl.BlockSpec((1, seq, LANES), lambda b, p, qi, lam: (b, 0, 4 + p)),
            pl.BlockSpec((1, seq, LANES), lambda b, p, qi, lam: (b, 0, 6 + p)),
            pl.BlockSpec((1, 2, nkb, V_ROWS, TK), lambda b, p, qi, lam: (b, p, 0, 0, 0)),
            pl.BlockSpec((DIFF_V_DIM, 1), lambda b, p, qi, lam: (0, 0)),
        ],
        out_specs=pl.BlockSpec((1, TQ, 2 * LANES), lambda b, p, qi, lam: (b, qi, p)),
        scratch_shapes=_attn_scratch(),
    )
    return pl.pallas_call(
        kern,
        grid_spec=grid_spec,
        out_shape=jax.ShapeDtypeStruct((batch, seq, D_DIFF_OUT), BF16),
        compiler_params=_params(("arbitrary", "arbitrary", "arbitrary")),
    )(lam, qk, qk, qk, qk, qk, qk, vt, g_col)


def _norm_matmul_kernel(x_ref, g_ref, w_ref, o_ref):
    x = x_ref[...]
    xn = ((x * _rms_scale(x, EPS)) * g_ref[...]).astype(BF16)
    o_ref[...] = jnp.dot(xn, w_ref[...], preferred_element_type=F32).astype(o_ref.dtype)


def _norm_matmul_call(x, g, w, tm, tn):
    t, d = x.shape
    n = w.shape[1]
    return pl.pallas_call(
        _norm_matmul_kernel,
        grid=(t // tm, n // tn),
        in_specs=[pl.BlockSpec((tm, d), lambda i, j: (i, 0)),
                  pl.BlockSpec((1, d), lambda i, j: (0, 0)),
                  pl.BlockSpec((d, tn), lambda i, j: (0, j))],
        out_specs=pl.BlockSpec((tm, tn), lambda i, j: (i, j)),
        out_shape=jax.ShapeDtypeStruct((t, n), BF16),
        compiler_params=_params(("arbitrary", "arbitrary")),
    )(x, g, w)


def _mix_mem_kernel(h_ref, od_ref, of_ref, wod_ref, wof_ref, g_ref, wq_ref, k_ref, v_ref, wo_ref,
                    o_ref, *, mem_scale):
    h1 = (h_ref[...]
          + jnp.dot(od_ref[...], wod_ref[...], preferred_element_type=F32)
          + jnp.dot(of_ref[...], wof_ref[...], preferred_element_type=F32))
    xn = ((h1 * _rms_scale(h1, EPS)) * g_ref[...]).astype(BF16)
    q = jnp.dot(xn, wq_ref[...], preferred_element_type=F32).astype(BF16)
    d = q.shape[1]
    hd = d // MEM_HEADS
    outs = []
    for hh in range(MEM_HEADS):
        qh = q[:, hh * hd:(hh + 1) * hd]
        kh = k_ref[:, hh * hd:(hh + 1) * hd]
        vh = v_ref[:, hh * hd:(hh + 1) * hd]
        s = lax.dot_general(qh, kh, (((1,), (1,)), ((), ())),
                            preferred_element_type=F32) * mem_scale
        m = jnp.max(s, axis=-1, keepdims=True)
        e = jnp.exp(s - m)
        p = e * (1.0 / jnp.sum(e, axis=-1, keepdims=True))
        outs.append(jnp.dot(p.astype(BF16), vh, preferred_element_type=F32).astype(BF16))
    o = jnp.concatenate(outs, axis=1)
    o_ref[...] = h1 + jnp.dot(o, wo_ref[...], preferred_element_type=F32)


def _mix_mem_call(h, o_d, o_f, w_od, w_of, g, w_q, kv, w_o, seq, n_mem):
    t, d = h.shape
    tm = TM_MIX
    tiles_per_seq = seq // tm
    kern = functools.partial(_mix_mem_kernel, mem_scale=1.0 / math.sqrt(d // MEM_HEADS))
    const = lambda i: (0, 0)
    return pl.pallas_call(
        kern,
        grid=(t // tm,),
        in_specs=[
            pl.BlockSpec((tm, d), lambda i: (i, 0)),
            pl.BlockSpec((tm, D_DIFF_OUT), lambda i: (i, 0)),
            pl.BlockSpec((tm, D_FOX), lambda i: (i, 0)),
            pl.BlockSpec((D_DIFF_OUT, d), const),
            pl.BlockSpec((D_FOX, d), const),
            pl.BlockSpec((1, d), const),
            pl.BlockSpec((d, d), const),
            pl.BlockSpec((n_mem, d), lambda i: (i // tiles_per_seq, 0)),
            pl.BlockSpec((n_mem, d), lambda i: (i // tiles_per_seq, 1)),
            pl.BlockSpec((d, d), const),
        ],
        out_specs=pl.BlockSpec((tm, d), lambda i: (i, 0)),
        out_shape=jax.ShapeDtypeStruct((t, d), F32),
        compiler_params=_params(("arbitrary",)),
    )(h, o_d, o_f, w_od, w_of, g, w_q, kv, kv, w_o)


def _swiglu_chunk(x, w1, w3, w2):
    a = jnp.dot(x, w1, preferred_element_type=F32)
    b = jnp.dot(x, w3, preferred_element_type=F32)
    mid = (a * (1.0 / (1.0 + jnp.exp(-a)))) * b
    return jnp.dot(mid.astype(BF16), w2, preferred_element_type=F32)


def _dense_ffn_kernel(h_ref, g_ref, w1_ref, w3_ref, w2_ref, o_ref, xn_ref):
    f = pl.program_id(1)

    @pl.when(f == 0)
    def _():
        x = h_ref[...]
        xn_ref[...] = ((x * _rms_scale(x, EPS)) * g_ref[...]).astype(BF16)
        o_ref[...] = x

    o_ref[...] += _swiglu_chunk(xn_ref[...], w1_ref[...], w3_ref[...], w2_ref[...])


def _dense_ffn_call(h, g, w1, w3, w2):
    t, d = h.shape
    tm, tf = TM_FFN, TF_DENSE
    dff = w1.shape[1]
    return pl.pallas_call(
        _dense_ffn_kernel,
        grid=(t // tm, dff // tf),
        in_specs=[pl.BlockSpec((tm, d), lambda i, f: (i, 0)),
                  pl.BlockSpec((1, d), lambda i, f: (0, 0)),
                  pl.BlockSpec((d, tf), lambda i, f: (0, f)),
                  pl.BlockSpec((d, tf), lambda i, f: (0, f)),
                  pl.BlockSpec((tf, d), lambda i, f: (f, 0))],
        out_specs=pl.BlockSpec((tm, d), lambda i, f: (i, 0)),
        out_shape=jax.ShapeDtypeStruct((t, d), F32),
        scratch_shapes=[pltpu.VMEM((tm, d), BF16)],
        compiler_params=_params(("arbitrary", "arbitrary")),
    )(h, g, w1, w3, w2)


def _router_kernel(h_ref, g_ref, wr_ref, hn_ref, idx_ref, gate_ref):
    x = h_ref[...]
    xn = ((x * _rms_scale(x, EPS)) * g_ref[...]).astype(BF16)
    hn_ref[...] = xn
    logits = jnp.dot(xn, wr_ref[...], preferred_element_type=F32)
    lane = lax.broadcasted_iota(jnp.int32, logits.shape, 1)
    logits = jnp.where(lane < N_EXPERTS, logits, -jnp.inf)
    v1 = jnp.max(logits, axis=-1, keepdims=True)
    i1 = jnp.min(jnp.where(logits == v1, lane, LANES), axis=-1, keepdims=True)
    rest = jnp.where(lane == i1, -jnp.inf, logits)
    v2 = jnp.max(rest, axis=-1, keepdims=True)
    i2 = jnp.min(jnp.where(rest == v2, lane, LANES), axis=-1, keepdims=True)
    e = jnp.exp(v2 - v1)
    inv = 1.0 / (1.0 + e)
    idx_ref[...] = jnp.where(lane == 0, i1, jnp.where(lane == 1, i2, 0))
    gate_ref[...] = jnp.where(lane == 0, inv, jnp.where(lane == 1, e * inv, 0.0))


def _router_call(h, g, w_r):
    t, d = h.shape
    tm = TM_FFN
    return pl.pallas_call(
        _router_kernel,
        grid=(t // tm,),
        in_specs=[pl.BlockSpec((tm, d), lambda i: (i, 0)),
                  pl.BlockSpec((1, d), lambda i: (0, 0)),
                  pl.BlockSpec((d, LANES), lambda i: (0, 0))],
        out_specs=[pl.BlockSpec((tm, d), lambda i: (i, 0)),
                   pl.BlockSpec((tm, LANES), lambda i: (i, 0)),
                   pl.BlockSpec((tm, LANES), lambda i: (i, 0))],
        out_shape=[jax.ShapeDtypeStruct((t, d), BF16),
                   jax.ShapeDtypeStruct((t, LANES), jnp.int32),
                   jax.ShapeDtypeStruct((t, LANES), F32)],
        compiler_params=_params(("arbitrary",)),
    )(h, g, w_r)


def _expert_ffn_kernel(te_ref, nu_ref, x_ref, w1_ref, w3_ref, w2_ref, *rest, n_chunks, tile0):
    o_ref, acc_ref = rest[-2:]
    i = pl.program_id(0)
    f = pl.program_id(1)

    @pl.when(tile0 + i < nu_ref[0])
    def _():
        @pl.when(f == 0)
        def _():
            acc_ref[...] = jnp.zeros_like(acc_ref)

        acc_ref[...] += _swiglu_chunk(x_ref[...], w1_ref[0, 0].astype(BF16), w3_ref[0, 0].astype(BF16),
                                      w2_ref[0, 0].astype(BF16))

        @pl.when(f == n_chunks - 1)
        def _():
            o_ref[...] = acc_ref[...].astype(o_ref.dtype)

    @pl.when(jnp.logical_and(tile0 >= nu_ref[0], jnp.logical_and(i == 0, f == 0)))
    def _():
        o_ref[...] = jnp.zeros_like(o_ref)


def _expert_ffn_call(tile_expert, n_used, x_rows, w1, w3, w2, layer, tile0, p_rows, y_prev=None):
    rows, d = x_rows.shape
    tm, tf = TM_FFN, TF_MOE
    n_local = rows // tm
    dff = w1.shape[3]
    n_chunks = dff // tf
    kern = functools.partial(_expert_ffn_kernel, n_chunks=n_chunks, tile0=tile0)

    def local(i, nu):
        return jnp.minimum(i, jnp.clip(nu[0] - 1 - tile0, 0, n_local - 1))

    def chunk(i, f, nu):
        return jnp.where(tile0 + i < nu[0], f, n_chunks - 1)

    def expert(i, te):
        return te[tile0 + i]

    in_specs = [
        pl.BlockSpec((tm, d), lambda i, f, te, nu: (local(i, nu), 0)),
        pl.BlockSpec((1, 1, d, tf), lambda i, f, te, nu: (layer, expert(i, te), 0, chunk(i, f, nu))),
        pl.BlockSpec((1, 1, d, tf), lambda i, f, te, nu: (layer, expert(i, te), 0, chunk(i, f, nu))),
        pl.BlockSpec((1, 1, tf, d), lambda i, f, te, nu: (layer, expert(i, te), chunk(i, f, nu), 0)),
    ]
    args = [tile_expert, n_used, x_rows, w1, w3, w2]
    aliases = {}
    if y_prev is not None:
        in_specs.append(pl.BlockSpec(memory_space=pl.ANY))
        aliases = {len(args): 0}
        args.append(y_prev)
    grid_spec = pltpu.PrefetchScalarGridSpec(
        num_scalar_prefetch=2,
        grid=(n_local, n_chunks),
        in_specs=in_specs,
        out_specs=pl.BlockSpec((tm, d), lambda i, f, te, nu: (tile0 + local(i, nu), 0)),
        scratch_shapes=[pltpu.VMEM((tm, d), F32)],
    )
    return pl.pallas_call(
        kern,
        grid_spec=grid_spec,
        out_shape=jax.ShapeDtypeStruct((p_rows, d), BF16),
        input_output_aliases=aliases,
        compiler_params=_params(("arbitrary", "arbitrary")),
    )(*args)


def _route(idx, tm):
    t = idx.shape[0]
    n_assign = 2 * t
    p_rows = n_assign + N_EXPERTS * tm
    e_flat = idx.reshape(-1)
    onehot = (e_flat[:, None] == jnp.arange(N_EXPERTS, dtype=jnp.int32)[None, :]).astype(jnp.int32)
    running = jnp.cumsum(onehot, axis=0)
    counts = running[-1]
    rank = jnp.sum(running * onehot, axis=1) - 1
    padded = ((counts + tm - 1) // tm) * tm
    pend = jnp.cumsum(padded)
    pstart = pend - padded
    gstart = jnp.cumsum(counts) - counts
    dest = pstart[e_flat] + rank
    n_tiles = p_rows // tm
    n_used = (pend[-1] // tm).astype(jnp.int32)
    tile_start = jnp.arange(n_tiles, dtype=jnp.int32) * tm
    tile_expert = jnp.sum((tile_start[:, None] >= pend[None, :]).astype(jnp.int32), axis=1)
    last_expert = jnp.sum((((n_used - 1) * tm) >= pend).astype(jnp.int32))
    tile_expert = jnp.where(tile_start < pend[-1], tile_expert, last_expert).astype(jnp.int32)
    order = jnp.sort(e_flat * n_assign + jnp.arange(n_assign, dtype=jnp.int32)) % n_assign
    row_expert = jnp.repeat(tile_expert, tm)
    within = jnp.arange(p_rows, dtype=jnp.int32) - pstart[row_expert]
    src = jnp.clip(gstart[row_expert] + within, 0, n_assign - 1)
    row_token = jnp.where(within < counts[row_expert], order[src] // 2, 0)
    return row_token, dest.reshape(t, 2), tile_expert, n_used.reshape(1)


def _moe_ffn(h, g, w_r, w1, w3, w2, layer):
    hn, idx_l, gate_l = _router_call(h, g, w_r)
    row_token, slot, tile_expert, n_used = _route(idx_l[:, :2], TM_FFN)
    p_rows = row_token.shape[0]
    assert p_rows % (MOE_SPLIT * TM_FFN) == 0
    rows = p_rows // MOE_SPLIT
    y = None
    for c in range(MOE_SPLIT):
        x_rows = jnp.take(hn, row_token[c * rows:(c + 1) * rows], axis=0)
        y = _expert_ffn_call(tile_expert, n_used, x_rows, w1, w3, w2, layer, c * (rows // TM_FFN), p_rows, y)
    return jnp.take(y, slot[:, 0], axis=0), jnp.take(y, slot[:, 1], axis=0), gate_l


def _final_norm_kernel(x_ref, g_ref, o_ref):
    x = x_ref[...]
    o_ref[...] = (x * _rms_scale(x, EPS)) * g_ref[...]


def _combine_norm_kernel(h_ref, y0_ref, y1_ref, gate_ref, g_ref, o_ref):
    x = (h_ref[...] + gate_ref[:, 0:1] * y0_ref[...].astype(F32)
         + gate_ref[:, 1:2] * y1_ref[...].astype(F32))
    o_ref[...] = (x * _rms_scale(x, EPS)) * g_ref[...]


def _final_norm_call(h, g, moe_parts=None):
    t, d = h.shape
    tm = TM_FFN
    tile = pl.BlockSpec((tm, d), lambda i: (i, 0))
    gain = pl.BlockSpec((1, d), lambda i: (0, 0))
    if moe_parts is None:
        kern, in_specs, args = _final_norm_kernel, [tile, gain], (h, g)
    else:
        y0, y1, gates = moe_parts
        kern = _combine_norm_kernel
        in_specs = [tile, tile, tile, pl.BlockSpec((tm, LANES), lambda i: (i, 0)), gain]
        args = (h, y0, y1, gates, g)
    return pl.pallas_call(
        kern,
        grid=(t // tm,),
        in_specs=in_specs,
        out_specs=tile,
        out_shape=jax.ShapeDtypeStruct((t, d), F32),
        compiler_params=_params(("arbitrary",)),
    )(*args)


ROPE_ROWS = 8


def _rope_table_kernel(pos_ref, freq_ref, cos_ref, sa_ref, sb_ref):
    dim = lax.broadcasted_iota(jnp.int32, (LANES, LANES), 1) % HEAD_DIM
    first = dim < ROPE_HALF
    second = jnp.logical_and(dim >= ROPE_HALF, dim < ROPE_DIM)
    freq = freq_ref[...]
    for r in range(ROPE_ROWS):
        ang = (freq * pos_ref[r:r + 1, :].astype(F32)).T
        cos, sin = jnp.cos(ang), jnp.sin(ang)
        rows = slice(r * LANES, (r + 1) * LANES)
        cos_ref[rows, :] = jnp.where(jnp.logical_or(first, second), cos, 1.0)
        sa_ref[rows, :] = jnp.where(second, sin, 0.0)
        sb_ref[rows, :] = jnp.where(first, -sin, 0.0)


def _rope_tables(positions):
    inv_freq = ROPE_THETA ** (-jnp.arange(0, ROPE_DIM, 2, dtype=F32) / ROPE_DIM)
    freq = jnp.tile(inv_freq, LANES // ROPE_HALF).reshape(LANES, 1)
    t = positions.size
    tm = ROPE_ROWS * LANES
    table = jax.ShapeDtypeStruct((t, LANES), F32)
    tile = pl.BlockSpec((tm, LANES), lambda i: (i, 0))
    return pl.pallas_call(
        _rope_table_kernel,
        grid=(t // tm,),
        in_specs=[pl.BlockSpec((ROPE_ROWS, LANES), lambda i: (i, 0)),
                  pl.BlockSpec((LANES, 1), lambda i: (0, 0))],
        out_specs=[tile, tile, tile],
        out_shape=[table, table, table],
        compiler_params=_params(("arbitrary",)),
    )(positions.reshape(t // LANES, LANES), freq)


def _decay_selectors():
    src = jnp.arange(LANES, dtype=jnp.int32)[:, None]
    dst = jnp.arange(LANES, dtype=jnp.int32)[None, :]
    sels = []
    for s in range(N_SPLIT):
        hit = jnp.logical_and(src < N_FOX_HEADS, dst == N_SPLIT * src + s)
        sels.append(hit.astype(BF16))
    return jnp.stack(sels)


def _split_w_in(w):
    w_f = jnp.pad(w[:, O_F:O_F + N_FOX_HEADS], ((0, 0), (0, LANES - N_FOX_HEADS))).astype(BF16)
    return w.astype(BF16), w_f


def kernel(x, mem, positions, norm_mix_g, w_in, lam_q1, lam_k1, lam_q2, lam_k2, diff_subln_g, fox_bias, w_out, norm_mem_g, mem_norm_g, w_mq, w_mkv, w_mo, norm_ffn_g, w_ff1, w_ff3, w_ff2, w_router, w_e1, w_e3, w_e2, final_norm_g):
    batch, seq, d = x.shape
    n_mem = mem.shape[1]
    depth = w_in.shape[0]
    t = batch * seq
    cos_t, sa_t, sb_t = _rope_tables(positions)
    sel = _decay_selectors()
    h = x.reshape(t, d)
    mem2 = mem.reshape(batch * n_mem, d)
    row = lambda v: v.reshape(1, -1).astype(F32)

    moe_parts = None
    for l in range(depth):
        lam_init = 0.8 - 0.6 * math.exp(-0.3 * l)
        lam = (jnp.exp(jnp.sum(lam_q1[l] * lam_k1[l])) - jnp.exp(jnp.sum(lam_q2[l] * lam_k2[l]))
               + lam_init).reshape(1).astype(F32)
        w_main, w_f = _split_w_in(w_in[l])
        f_bias = jnp.pad(fox_bias[l], (0, LANES - N_FOX_HEADS)).reshape(1, LANES)

        qk, vt, c = _proj_call(h, row(norm_mix_g[l]), w_main, w_f, f_bias, sel, cos_t, sa_t, sb_t, batch, seq)
        qk3 = qk.reshape(batch, seq, qk.shape[1])
        ct = jnp.transpose(c.reshape(batch, seq, LANES)[:, :, :N_FOX_HEADS], (0, 2, 1))
        o_d = _diff_call(lam, qk3, vt, diff_subln_g[l].reshape(DIFF_V_DIM, 1), 1.0 - lam_init, batch, seq)
        o_f = _fox_call(qk3, vt, ct, batch, seq)

        w_o = w_out[l].astype(BF16)
        kv = _norm_matmul_call(mem2, row(mem_norm_g[l]), w_mkv[l].astype(BF16), 512, 1024)
        h = _mix_mem_call(h, o_d.reshape(t, D_DIFF_OUT), o_f.reshape(t, D_FOX),
                          w_o[:D_DIFF_OUT], w_o[D_DIFF_OUT:], row(norm_mem_g[l]),
                          w_mq[l].astype(BF16), kv, w_mo[l].astype(BF16), seq, n_mem)

        i = l // 2
        if l % 2 == 0:
            h = _dense_ffn_call(h, row(norm_ffn_g[l]), w_ff1[i].astype(BF16), w_ff3[i].astype(BF16),
                                w_ff2[i].astype(BF16))
        else:
            w_r = jnp.pad(w_router[i], ((0, 0), (0, LANES - N_EXPERTS))).astype(BF16)
            y0, y1, gates = _moe_ffn(h, row(norm_ffn_g[l]), w_r, w_e1, w_e3, w_e2, i)
            if l == depth - 1:
                moe_parts = (y0, y1, gates)
            else:
                h = h + gates[:, 0:1] * y0.astype(F32) + gates[:, 1:2] * y1.astype(F32)

    return _final_norm_call(h, row(final_norm_g), moe_parts).reshape(batch, seq, d)
```

```python
import functools
import math

import jax
import jax.numpy as jnp
from jax import lax
from jax.experimental import pallas as pl
from jax.experimental.pallas import tpu as pltpu

F32 = jnp.float32
BF16 = jnp.bfloat16

HEAD_DIM = 64
N_DIFF_HEADS = 4
DIFF_V_DIM = 128
N_FOX_HEADS = 8
D_DIFF_QK = N_DIFF_HEADS * HEAD_DIM
D_DIFF_OUT = N_DIFF_HEADS * DIFF_V_DIM
D_FOX = N_FOX_HEADS * HEAD_DIM
ATTN_SCALE = 1.0 / math.sqrt(HEAD_DIM)
ROPE_DIM = HEAD_DIM // 4
ROPE_HALF = ROPE_DIM // 2
ROPE_THETA = 500000.0
MEM_HEADS = 4
N_EXPERTS = 8
EPS = 1e-6
SUBLN_EPS = 1e-5
LOG2E = 1.0 / math.log(2.0)

LANES = 128
VMEM_LIMIT = 48 * 1024 * 1024

TM_PROJ = 512
TN_PROJ = 512
N_QK_BLK = 4
TQ = 512
TK = 512
TM_MIX = 512
TM_FFN = 1024
TF_DENSE = 256
TF_MOE = 512
MOE_SPLIT = 4
V_ROWS = LANES + 8
N_SPLIT = 3
GATE_ROWS = 16


def _params(sem):
    return pltpu.CompilerParams(dimension_semantics=sem, vmem_limit_bytes=VMEM_LIMIT)


def _rms_scale(x, eps):
    return lax.rsqrt(jnp.mean(x * x, axis=-1, keepdims=True) + eps)


def _split3(v):
    hi = v.astype(BF16)
    r1 = v - hi.astype(F32)
    mid = r1.astype(BF16)
    lo = (r1 - mid.astype(F32)).astype(BF16)
    return hi, mid, lo


O_VD = 4 * D_DIFF_QK
O_QF = O_VD + D_DIFF_OUT
O_KF = O_QF + D_FOX
O_VF = O_KF + D_FOX
O_F = O_VF + D_FOX
Q_SCALE = ATTN_SCALE * LOG2E
QK_COLS = 2 * TN_PROJ + 2 * N_FOX_HEADS * LANES
_PROJ_BLOCKS = (
    (0, "rope", 0, Q_SCALE),
    (2 * D_DIFF_QK, "rope", TN_PROJ, 1.0),
    (O_QF, "fox_q", 2 * TN_PROJ, Q_SCALE),
    (O_KF, "fox_k", 2 * TN_PROJ + N_FOX_HEADS * LANES, 1.0),
    (O_VD, "vt", 0, 1.0),
    (O_VF, "vt", N_DIFF_HEADS, 1.0),
)


def _proj_kernel(x_ref, g_ref, w_ref, wf_ref, fb_ref, sel_ref, cos_ref, sa_ref, sb_ref,
                 qk_ref, vt_ref, c_ref, carry_ref, *, tiles_per_seq):
    i = pl.program_id(0)
    tm = x_ref.shape[0]
    x = x_ref[...]
    xnb = ((x * _rms_scale(x, EPS)) * g_ref[...]).astype(BF16)

    logit = lax.dot_general(wf_ref[...], xnb, (((1,), (1,)), ((), ())),
                            preferred_element_type=F32) + fb_ref[...]
    logf = jnp.minimum(logit, 0.0) - jnp.log1p(jnp.exp(-jnp.abs(logit)))

    @pl.when(i % tiles_per_seq == 0)
    def _():
        carry_ref[...] = jnp.zeros_like(carry_ref)

    r = lax.broadcasted_iota(jnp.int32, (tm, tm), 0)
    cidx = lax.broadcasted_iota(jnp.int32, (tm, tm), 1)
    tri = (r <= cidx).astype(BF16)
    hi, mid, lo = _split3(logf)
    cs = (jnp.dot(hi, tri, preferred_element_type=F32)
          + jnp.dot(mid, tri, preferred_element_type=F32)
          + jnp.dot(lo, tri, preferred_element_type=F32)) + carry_ref[:, 0:1]
    carry_ref[...] = jnp.broadcast_to(cs[:, tm - 1:tm], carry_ref.shape)
    c2 = cs * LOG2E
    c_ref[0] = c2
    hi, mid, lo = _split3(-c2)
    kx = (jnp.dot(sel_ref[0], hi, preferred_element_type=F32)
          + jnp.dot(sel_ref[1], mid, preferred_element_type=F32)
          + jnp.dot(sel_ref[2], lo, preferred_element_type=F32)).T
    lane = lax.broadcasted_iota(jnp.int32, (tm, LANES), 1)
    low_half = lane < HEAD_DIM
    extra = [jnp.logical_and(lane >= HEAD_DIM, lane < HEAD_DIM + N_SPLIT), lane < N_SPLIT]

    cos = cos_ref[...]
    sa = sa_ref[...]
    sb = sb_ref[...]
    ones_row = (lax.broadcasted_iota(jnp.int32, (V_ROWS - LANES, TK), 0) == 0).astype(vt_ref.dtype)
    for src, kind, dst, scale in _PROJ_BLOCKS:
        acc = jnp.dot(xnb, w_ref[:, src:src + TN_PROJ], preferred_element_type=F32)
        if scale != 1.0:
            acc = acc * scale
        if kind == "rope":
            for s in range(TN_PROJ // LANES):
                t = acc[:, s * LANES:(s + 1) * LANES]
                rot = (t * cos + pltpu.roll(t, ROPE_HALF, 1) * sa
                       + pltpu.roll(t, LANES - ROPE_HALF, 1) * sb)
                qk_ref[:, dst + s * LANES:dst + (s + 1) * LANES] = rot.astype(qk_ref.dtype)
        elif kind in ("fox_q", "fox_k"):
            for pr in range(TN_PROJ // LANES):
                pair = acc[:, pr * LANES:(pr + 1) * LANES]
                for par in range(2):
                    head = 2 * pr + par
                    if kind == "fox_q":
                        other = extra[par].astype(F32)
                    else:
                        first = HEAD_DIM if par == 0 else 0
                        moved = pltpu.roll(kx, (first - N_SPLIT * head) % LANES, 1)
                        other = jnp.where(extra[par], moved, 0.0)
                    keep = low_half if par == 0 else jnp.logical_not(low_half)
                    qk_ref[:, dst + head * LANES:dst + (head + 1) * LANES] = (
                        jnp.where(keep, pair, other).astype(qk_ref.dtype))
        else:
            at = acc.T
            for hd in range(TN_PROJ // LANES):
                vt_ref[0, dst + hd, 0, 0:LANES, :] = at[hd * LANES:(hd + 1) * LANES, :].astype(vt_ref.dtype)
                vt_ref[0, dst + hd, 0, LANES:V_ROWS, :] = ones_row


def _proj_call(h, g, w_main, w_f, f_bias, sel, cos_t, sa_t, sb_t, batch, seq):
    t, d = h.shape
    tm = TM_PROJ
    assert tm == TK
    tiles_per_seq = seq // tm
    n_col = w_main.shape[1]
    n_vh = (len(_PROJ_BLOCKS) - N_QK_BLK) * (TN_PROJ // LANES)
    kern = functools.partial(_proj_kernel, tiles_per_seq=tiles_per_seq)
    const2 = lambda i: (0, 0)
    return pl.pallas_call(
        kern,
        grid=(t // tm,),
        in_specs=[
            pl.BlockSpec((tm, d), lambda i: (i, 0)),
            pl.BlockSpec((1, d), const2),
            pl.BlockSpec((d, n_col), const2),
            pl.BlockSpec((GATE_ROWS, d), const2),
            pl.BlockSpec((GATE_ROWS, 1), const2),
            pl.BlockSpec((N_SPLIT, LANES, GATE_ROWS), lambda i: (0, 0, 0)),
            pl.BlockSpec((tm, LANES), lambda i: (i, 0)),
            pl.BlockSpec((tm, LANES), lambda i: (i, 0)),
            pl.BlockSpec((tm, LANES), lambda i: (i, 0)),
        ],
        out_specs=[
            pl.BlockSpec((tm, QK_COLS), lambda i: (i, 0)),
            pl.BlockSpec((1, n_vh, 1, V_ROWS, TK), lambda i: (i // tiles_per_seq, 0, i % tiles_per_seq, 0, 0)),
            pl.BlockSpec((1, GATE_ROWS, tm), lambda i: (i // tiles_per_seq, 0, i % tiles_per_seq)),
        ],
        out_shape=[
            jax.ShapeDtypeStruct((t, QK_COLS), BF16),
            jax.ShapeDtypeStruct((batch, n_vh, seq // TK, V_ROWS, TK), BF16),
            jax.ShapeDtypeStruct((batch, GATE_ROWS, seq), F32),
        ],
        scratch_shapes=[pltpu.VMEM((GATE_ROWS, LANES), F32)],
        compiler_params=_params(("arbitrary",)),
    )(h, g, w_main, w_f, f_bias, sel, cos_t, sa_t, sb_t)


def _head_mask(q, half):
    lane = lax.broadcasted_iota(jnp.int32, q.shape, 1)
    return jnp.where(lane // HEAD_DIM == half, q, jnp.zeros_like(q))


def _causal_mask():
    kpos = lax.broadcasted_iota(jnp.int32, (TK, TQ), 0)
    qpos = lax.broadcasted_iota(jnp.int32, (TK, TQ), 1)
    return kpos <= qpos


def _scores_t(k, q):
    return lax.dot_general(k, q, (((1,), (1,)), ((), ())), preferred_element_type=F32)


def _online_update(u, vt, shift, acc_ref, m_ref):
    mu = jnp.max(u, axis=0, keepdims=True) + shift
    m_old = m_ref[...]
    m_new = jnp.maximum(m_old, mu)
    p = jnp.exp2(u - (m_new - shift)).astype(BF16)
    pv = jnp.dot(vt, p, preferred_element_type=F32)
    acc_ref[...] = jnp.exp2(m_old - m_new) * acc_ref[...] + pv
    m_ref[...] = m_new


def _normalised(acc_ref):
    acc = acc_ref[...]
    return acc[:LANES] * (1.0 / acc[LANES:LANES + 1])


N_TILES = 4
UNROLL = 4


def _attn_scratch():
    return ([pltpu.VMEM((TK, TQ), F32)] * (3 * N_TILES) + [pltpu.VMEM((V_ROWS, TQ), F32)] * N_TILES
            + [pltpu.VMEM((1, TQ), F32)] * N_TILES)


def _attend(qi, qk_fn, qk_next_fn, vt_fn, shifts, scratch, finish):
    assert TQ == TK
    s_a, s_b, s_c = (scratch[n * N_TILES:(n + 1) * N_TILES] for n in range(3))
    acc = scratch[3 * N_TILES:4 * N_TILES]
    m = scratch[4 * N_TILES:]
    for t in range(N_TILES):
        m[t][...] = jnp.full(m[t].shape, -jnp.inf, F32)
        acc[t][...] = jnp.zeros(acc[t].shape, F32)

    def stage(j, cur, nxt, masked):
        for t in range(N_TILES):
            nxt[t][...] = qk_next_fn(t) if masked else qk_fn(j + 1, t)
            u = cur[t][...]
            if masked:
                u = jnp.where(_causal_mask(), u, -jnp.inf)
            _online_update(u, vt_fn(j, t), shifts[t], acc[t], m[t])

    @pl.when(qi == 0)
    def _():
        for t in range(N_TILES):
            s_a[t][...] = qk_fn(0, t)
        stage(0, s_a, s_c, True)
        finish(acc)

    @pl.when(qi > 0)
    def _():
        stage(0, s_c, s_a, False)
        bufs = (s_a, s_b)

        def body(i, carry):
            for n in range(UNROLL):
                stage(UNROLL * i + 1 + n, bufs[n % 2], bufs[(n + 1) % 2], False)
            return carry

        lax.fori_loop(0, (qi - 1) // UNROLL, body, 0)

        for rem in range(UNROLL):
            @pl.when((qi - 1) % UNROLL == rem)
            def _(rem=rem):
                for n in range(rem):
                    stage(qi - rem + n, bufs[n % 2], bufs[(n + 1) % 2], False)
                stage(qi, bufs[rem % 2], s_c, True)
                finish(acc)


def _fox_kernel(q_ref, qn_ref, k_ref, vt_ref, ct_ref, o_ref, *scratch):
    grp = pl.program_id(1)
    qi = pl.program_id(2)
    cqs = [ct_ref[0, pl.ds(N_TILES * grp + t, 1), :] for t in range(N_TILES)]

    def scores(j, t, qr):
        start = pl.multiple_of(j * TK, TK)
        return _scores_t(k_ref[0, pl.ds(start, TK), t * LANES:(t + 1) * LANES],
                         qr[0, :, t * LANES:(t + 1) * LANES])

    def vt_fn(j, t):
        return vt_ref[0, t // 2, j]

    def finish(acc):
        for pp in range(2):
            ot = jnp.concatenate([_normalised(acc[2 * pp])[:HEAD_DIM],
                                  _normalised(acc[2 * pp + 1])[HEAD_DIM:]], axis=0)
            o_ref[0, :, pp * LANES:(pp + 1) * LANES] = ot.T.astype(o_ref.dtype)

    _attend(qi, lambda j, t: scores(j, t, q_ref), lambda t: scores(0, t, qn_ref), vt_fn, cqs, scratch, finish)


def _fox_call(qk, vt, ct, batch, seq):
    nkb = seq // TK
    nq = seq // TQ
    w4 = N_TILES * LANES
    return pl.pallas_call(
        _fox_kernel,
        grid=(batch, N_FOX_HEADS // N_TILES, nq),
        in_specs=[
            pl.BlockSpec((1, TQ, w4), lambda b, g, qi: (b, qi, 2 + g)),
            pl.BlockSpec((1, TQ, w4), lambda b, g, qi: (b, jnp.minimum(qi + 1, nq - 1), 2 + g)),
            pl.BlockSpec((1, seq, w4), lambda b, g, qi: (b, 0, 4 + g)),
            pl.BlockSpec((1, 2, nkb, V_ROWS, TK), lambda b, g, qi: (b, N_DIFF_HEADS // 2 + g, 0, 0, 0)),
            pl.BlockSpec((1, GATE_ROWS, TQ), lambda b, g, qi: (b, 0, qi)),
        ],
        out_specs=pl.BlockSpec((1, TQ, 2 * LANES), lambda b, g, qi: (b, qi, g)),
        out_shape=jax.ShapeDtypeStruct((batch, seq, D_FOX), BF16),
        scratch_shapes=_attn_scratch(),
        compiler_params=_params(("arbitrary", "arbitrary", "arbitrary")),
    )(qk, qk, qk, vt, ct)


def _diff_kernel(lam_ref, q1_ref, q2_ref, q1n_ref, q2n_ref, k1_ref, k2_ref, vt_ref, g_ref, o_ref, *scratch,
                 out_scale):
    qi = pl.program_id(2)
    k_refs = [k1_ref, k2_ref]
    qms = [_head_mask(r[0], hh) for hh in range(2) for r in (q1_ref, q2_ref)]
    qms_next = [_head_mask(r[0], hh) for hh in range(2) for r in (q1n_ref, q2n_ref)]

    def scores(j, t, qm):
        start = pl.multiple_of(j * TK, TK)
        return _scores_t(k_refs[t % 2][0, pl.ds(start, TK), :], qm[t])

    def vt_fn(j, t):
        return vt_ref[0, t // 2, j]

    def finish(acc):
        for hh in range(2):
            ot = _normalised(acc[2 * hh]) - lam_ref[0] * _normalised(acc[2 * hh + 1])
            y = ot * lax.rsqrt(jnp.mean(ot * ot, axis=0, keepdims=True) + SUBLN_EPS)
            y = (y * g_ref[...]) * out_scale
            o_ref[0, :, hh * LANES:(hh + 1) * LANES] = y.T.astype(o_ref.dtype)

    _attend(qi, lambda j, t: scores(j, t, qms), lambda t: scores(0, t, qms_next), vt_fn,
            [0.0] * N_TILES, scratch, finish)


def _diff_call(lam, qk, vt, g_col, out_scale, batch, seq):
    nkb = seq // TK
    nq = seq // TQ
    n_pairs = N_DIFF_HEADS // 2
    kern = functools.partial(_diff_kernel, out_scale=out_scale)
    grid_spec = pltpu.PrefetchScalarGridSpec(
        num_scalar_prefetch=1,
        grid=(batch, n_pairs, nq),
        in_specs=[
            pl.BlockSpec((1, TQ, LANES), lambda b, p, qi, lam: (b, qi, p)),
            pl.BlockSpec((1, TQ, LANES), lambda b, p, qi, lam: (b, qi, 2 + p)),
            pl.BlockSpec((1, TQ, LANES), lambda b, p, qi, lam: (b, jnp.minimum(qi + 1, nq - 1), p)),
            pl.BlockSpec((1, TQ, LANES), lambda b, p, qi, lam: (b, jnp.minimum(qi + 1, nq - 1), 2 + p)),
            pl.BlockSpec((1, seq, LANES), lambda b, p, qi, lam: (b, 0, 4 + p)),
            pl.BlockSpec((1, seq, LANES), lambda b, p, qi, lam: (b, 0, 6 + p)),
            pl.BlockSpec((1, 2, nkb, V_ROWS, TK), lambda b, p, qi, lam: (b, p, 0, 0, 0)),
            pl.BlockSpec((DIFF_V_DIM, 1), lambda b, p, qi, lam: (0, 0)),
        ],
        out_specs=pl.BlockSpec((1, TQ, 2 * LANES), lambda b, p, qi, lam: (b, qi, p)),
        scratch_shapes=_attn_scratch(),
    )
    return pl.pallas_call(
        kern,
        grid_spec=grid_spec,
        out_shape=jax.ShapeDtypeStruct((batch, seq, D_DIFF_OUT), BF16),
        compiler_params=_params(("arbitrary", "arbitrary", "arbitrary")),
    )(lam, qk, qk, qk, qk, qk, qk, vt, g_col)


def _norm_matmul_kernel(x_ref, g_ref, w_ref, o_ref):
    x = x_ref[...]
    xn = ((x * _rms_scale(x, EPS)) * g_ref[...]).astype(BF16)
    o_ref[...] = jnp.dot(xn, w_ref[...], preferred_element_type=F32).astype(o_ref.dtype)


def _norm_matmul_call(x, g, w, tm, tn):
    t, d = x.shape
    n = w.shape[1]
    return pl.pallas_call(
        _norm_matmul_kernel,
        grid=(t // tm, n // tn),
        in_specs=[pl.BlockSpec((tm, d), lambda i, j: (i, 0)),
                  pl.BlockSpec((1, d), lambda i, j: (0, 0)),
                  pl.BlockSpec((d, tn), lambda i, j: (0, j))],
        out_specs=pl.BlockSpec((tm, tn), lambda i, j: (i, j)),
        out_shape=jax.ShapeDtypeStruct((t, n), BF16),
        compiler_params=_params(("arbitrary", "arbitrary")),
    )(x, g, w)


def _mix_mem_kernel(h_ref, od_ref, of_ref, wod_ref, wof_ref, g_ref, wq_ref, k_ref, v_ref, wo_ref,
                    o_ref, *, mem_scale):
    h1 = (h_ref[...]
          + jnp.dot(od_ref[...], wod_ref[...], preferred_element_type=F32)
          + jnp.dot(of_ref[...], wof_ref[...], preferred_element_type=F32))
    xn = ((h1 * _rms_scale(h1, EPS)) * g_ref[...]).astype(BF16)
    q = jnp.dot(xn, wq_ref[...], preferred_element_type=F32).astype(BF16)
    d = q.shape[1]
    hd = d // MEM_HEADS
    outs = []
    for hh in range(MEM_HEADS):
        qh = q[:, hh * hd:(hh + 1) * hd]
        kh = k_ref[:, hh * hd:(hh + 1) * hd]
        vh = v_ref[:, hh * hd:(hh + 1) * hd]
        s = lax.dot_general(qh, kh, (((1,), (1,)), ((), ())),
                            preferred_element_type=F32) * mem_scale
        m = jnp.max(s, axis=-1, keepdims=True)
        e = jnp.exp(s - m)
        p = e * (1.0 / jnp.sum(e, axis=-1, keepdims=True))
        outs.append(jnp.dot(p.astype(BF16), vh, preferred_element_type=F32).astype(BF16))
    o = jnp.concatenate(outs, axis=1)
    o_ref[...] = h1 + jnp.dot(o, wo_ref[...], preferred_element_type=F32)


def _mix_mem_call(h, o_d, o_f, w_od, w_of, g, w_q, kv, w_o, seq, n_mem):
    t, d = h.shape
    tm = TM_MIX
    tiles_per_seq = seq // tm
    kern = functools.partial(_mix_mem_kernel, mem_scale=1.0 / math.sqrt(d // MEM_HEADS))
    const = lambda i: (0, 0)
    return pl.pallas_call(
        kern,
        grid=(t // tm,),
        in_specs=[
            pl.BlockSpec((tm, d), lambda i: (i, 0)),
            pl.BlockSpec((tm, D_DIFF_OUT), lambda i: (i, 0)),
            pl.BlockSpec((tm, D_FOX), lambda i: (i, 0)),
            pl.BlockSpec((D_DIFF_OUT, d), const),
            pl.BlockSpec((D_FOX, d), const),
            pl.BlockSpec((1, d), const),
            pl.BlockSpec((d, d), const),
            pl.BlockSpec((n_mem, d), lambda i: (i // tiles_per_seq, 0)),
            pl.BlockSpec((n_mem, d), lambda i: (i // tiles_per_seq, 1)),
            pl.BlockSpec((d, d), const),
        ],
        out_specs=pl.BlockSpec((tm, d), lambda i: (i, 0)),
        out_shape=jax.ShapeDtypeStruct((t, d), F32),
        compiler_params=_params(("arbitrary",)),
    )(h, o_d, o_f, w_od, w_of, g, w_q, kv, kv, w_o)


def _swiglu_chunk(x, w1, w3, w2):
    a = jnp.dot(x, w1, preferred_element_type=F32)
    b = jnp.dot(x, w3, preferred_element_type=F32)
    mid = (a * (1.0 / (1.0 + jnp.exp(-a)))) * b
    return jnp.dot(mid.astype(BF16), w2, preferred_element_type=F32)


def _dense_ffn_kernel(h_ref, g_ref, w1_ref, w3_ref, w2_ref, o_ref, xn_ref):
    f = pl.program_id(1)

    @pl.when(f == 0)
    def _():
        x = h_ref[...]
        xn_ref[...] = ((x * _rms_scale(x, EPS)) * g_ref[...]).astype(BF16)
        o_ref[...] = x

    o_ref[...] += _swiglu_chunk(xn_ref[...], w1_ref[...], w3_ref[...], w2_ref[...])


def _dense_ffn_call(h, g, w1, w3, w2):
    t, d = h.shape
    tm, tf = TM_FFN, TF_DENSE
    dff = w1.shape[1]
    return pl.pallas_call(
        _dense_ffn_kernel,
        grid=(t // tm, dff // tf),
        in_specs=[pl.BlockSpec((tm, d), lambda i, f: (i, 0)),
                  pl.BlockSpec((1, d), lambda i, f: (0, 0)),
                  pl.BlockSpec((d, tf), lambda i, f: (0, f)),
                  pl.BlockSpec((d, tf), lambda i, f: (0, f)),
                  pl.BlockSpec((tf, d), lambda i, f: (f, 0))],
        out_specs=pl.BlockSpec((tm, d), lambda i, f: (i, 0)),
        out_shape=jax.ShapeDtypeStruct((t, d), F32),
        scratch_shapes=[pltpu.VMEM((tm, d), BF16)],
        compiler_params=_params(("arbitrary", "arbitrary")),
    )(h, g, w1, w3, w2)


def _router_kernel(h_ref, g_ref, wr_ref, hn_ref, idx_ref, gate_ref):
    x = h_ref[...]
    xn = ((x * _rms_scale(x, EPS)) * g_ref[...]).astype(BF16)
    hn_ref[...] = xn
    logits = jnp.dot(xn, wr_ref[...], preferred_element_type=F32)
    lane = lax.broadcasted_iota(jnp.int32, logits.shape, 1)
    logits = jnp.where(lane < N_EXPERTS, logits, -jnp.inf)
    v1 = jnp.max(logits, axis=-1, keepdims=True)
    i1 = jnp.min(jnp.where(logits == v1, lane, LANES), axis=-1, keepdims=True)
    rest = jnp.where(lane == i1, -jnp.inf, logits)
    v2 = jnp.max(rest, axis=-1, keepdims=True)
    i2 = jnp.min(jnp.where(rest == v2, lane, LANES), axis=-1, keepdims=True)
    e = jnp.exp(v2 - v1)
    inv = 1.0 / (1.0 + e)
    idx_ref[...] = jnp.where(lane == 0, i1, jnp.where(lane == 1, i2, 0))
    gate_ref[...] = jnp.where(lane == 0, inv, jnp.where(lane == 1, e * inv, 0.0))


def _router_call(h, g, w_r):
    t, d = h.shape
    tm = TM_FFN
    return pl.pallas_call(
        _router_kernel,
        grid=(t // tm,),
        in_specs=[pl.BlockSpec((tm, d), lambda i: (i, 0)),
                  pl.BlockSpec((1, d), lambda i: (0, 0)),
                  pl.BlockSpec((d, LANES), lambda i: (0, 0))],
        out_specs=[pl.BlockSpec((tm, d), lambda i: (i, 0)),
                   pl.BlockSpec((tm, LANES), lambda i: (i, 0)),
                   pl.BlockSpec((tm, LANES), lambda i: (i, 0))],
        out_shape=[jax.ShapeDtypeStruct((t, d), BF16),
                   jax.ShapeDtypeStruct((t, LANES), jnp.int32),
                   jax.ShapeDtypeStruct((t, LANES), F32)],
        compiler_params=_params(("arbitrary",)),
    )(h, g, w_r)


def _expert_ffn_kernel(te_ref, nu_ref, x_ref, w1_ref, w3_ref, w2_ref, *rest, n_chunks, tile0):
    o_ref, acc_ref = rest[-2:]
    i = pl.program_id(0)
    f = pl.program_id(1)

    @pl.when(tile0 + i < nu_ref[0])
    def _():
        @pl.when(f == 0)
        def _():
            acc_ref[...] = jnp.zeros_like(acc_ref)

        acc_ref[...] += _swiglu_chunk(x_ref[...], w1_ref[0, 0].astype(BF16), w3_ref[0, 0].astype(BF16),
                                      w2_ref[0, 0].astype(BF16))

        @pl.when(f == n_chunks - 1)
        def _():
            o_ref[...] = acc_ref[...].astype(o_ref.dtype)

    @pl.when(jnp.logical_and(tile0 >= nu_ref[0], jnp.logical_and(i == 0, f == 0)))
    def _():
        o_ref[...] = jnp.zeros_like(o_ref)


def _expert_ffn_call(tile_expert, n_used, x_rows, w1, w3, w2, layer, tile0, p_rows, y_prev=None):
    rows, d = x_rows.shape
    tm, tf = TM_FFN, TF_MOE
    n_local = rows // tm
    dff = w1.shape[3]
    n_chunks = dff // tf
    kern = functools.partial(_expert_ffn_kernel, n_chunks=n_chunks, tile0=tile0)

    def local(i, nu):
        return jnp.minimum(i, jnp.clip(nu[0] - 1 - tile0, 0, n_local - 1))

    def chunk(i, f, nu):
        return jnp.where(tile0 + i < nu[0], f, n_chunks - 1)

    def expert(i, te):
        return te[tile0 + i]

    in_specs = [
        pl.BlockSpec((tm, d), lambda i, f, te, nu: (local(i, nu), 0)),
        pl.BlockSpec((1, 1, d, tf), lambda i, f, te, nu: (layer, expert(i, te), 0, chunk(i, f, nu))),
        pl.BlockSpec((1, 1, d, tf), lambda i, f, te, nu: (layer, expert(i, te), 0, chunk(i, f, nu))),
        pl.BlockSpec((1, 1, tf, d), lambda i, f, te, nu: (layer, expert(i, te), chunk(i, f, nu), 0)),
    ]
    args = [tile_expert, n_used, x_rows, w1, w3, w2]
    aliases = {}
    if y_prev is not None:
        in_specs.append(pl.BlockSpec(memory_space=pl.ANY))
        aliases = {len(args): 0}
        args.append(y_prev)
    grid_spec = pltpu.PrefetchScalarGridSpec(
        num_scalar_prefetch=2,
        grid=(n_local, n_chunks),
        in_specs=in_specs,
        out_specs=pl.BlockSpec((tm, d), lambda i, f, te, nu: (tile0 + local(i, nu), 0)),
        scratch_shapes=[pltpu.VMEM((tm, d), F32)],
    )
    return pl.pallas_call(
        kern,
        grid_spec=grid_spec,
        out_shape=jax.ShapeDtypeStruct((p_rows, d), BF16),
        input_output_aliases=aliases,
        compiler_params=_params(("arbitrary", "arbitrary")),
    )(*args)


def _route(idx, tm):
    t = idx.shape[0]
    n_assign = 2 * t
    p_rows = n_assign + N_EXPERTS * tm
    e_flat = idx.reshape(-1)
    onehot = (e_flat[None, :] == jnp.arange(N_EXPERTS, dtype=jnp.int32)[:, None]).astype(jnp.int32)
    running = jnp.cumsum(onehot, axis=1)
    counts = running[:, -1]
    padded = ((counts + tm - 1) // tm) * tm
    pend = jnp.cumsum(padded)
    pstart = pend - padded
    gstart = jnp.cumsum(counts) - counts
    dest = jnp.sum(onehot * (running - 1 + pstart[:, None]), axis=0)
    n_tiles = p_rows // tm
    n_used = (pend[-1] // tm).astype(jnp.int32)
    tile_start = jnp.arange(n_tiles, dtype=jnp.int32) * tm
    tile_expert = jnp.sum((tile_start[:, None] >= pend[None, :]).astype(jnp.int32), axis=1)
    last_expert = jnp.sum((((n_used - 1) * tm) >= pend).astype(jnp.int32))
    tile_expert = jnp.where(tile_start < pend[-1], tile_expert, last_expert).astype(jnp.int32)
    order = jnp.sort(e_flat * n_assign + jnp.arange(n_assign, dtype=jnp.int32)) % n_assign
    within = ((tile_start - pstart[tile_expert])[:, None]
              + jnp.arange(tm, dtype=jnp.int32)[None, :])
    src = jnp.clip(gstart[tile_expert][:, None] + within, 0, n_assign - 1)
    row_token = jnp.where(within < counts[tile_expert][:, None], order[src] // 2, 0)
    return row_token.reshape(p_rows), dest.reshape(t, 2), tile_expert, n_used.reshape(1)


def _moe_ffn(h, g, w_r, w1, w3, w2, layer):
    hn, idx_l, gate_l = _router_call(h, g, w_r)
    row_token, slot, tile_expert, n_used = _route(idx_l[:, :2], TM_FFN)
    p_rows = row_token.shape[0]
    assert p_rows % (MOE_SPLIT * TM_FFN) == 0
    rows = p_rows // MOE_SPLIT
    y = None
    for c in range(MOE_SPLIT):
        x_rows = jnp.take(hn, row_token[c * rows:(c + 1) * rows], axis=0)
        y = _expert_ffn_call(tile_expert, n_used, x_rows, w1, w3, w2, layer, c * (rows // TM_FFN), p_rows, y)
    return jnp.take(y, slot[:, 0], axis=0), jnp.take(y, slot[:, 1], axis=0), gate_l


def _final_norm_kernel(x_ref, g_ref, o_ref):
    x = x_ref[...]
    o_ref[...] = (x * _rms_scale(x, EPS)) * g_ref[...]


def _combine_norm_kernel(h_ref, y0_ref, y1_ref, gate_ref, g_ref, o_ref):
    x = (h_ref[...] + gate_ref[:, 0:1] * y0_ref[...].astype(F32)
         + gate_ref[:, 1:2] * y1_ref[...].astype(F32))
    o_ref[...] = (x * _rms_scale(x, EPS)) * g_ref[...]


def _final_norm_call(h, g, moe_parts=None):
    t, d = h.shape
    tm = TM_FFN
    tile = pl.BlockSpec((tm, d), lambda i: (i, 0))
    gain = pl.BlockSpec((1, d), lambda i: (0, 0))
    if moe_parts is None:
        kern, in_specs, args = _final_norm_kernel, [tile, gain], (h, g)
    else:
        y0, y1, gates = moe_parts
        kern = _combine_norm_kernel
        in_specs = [tile, tile, tile, pl.BlockSpec((tm, LANES), lambda i: (i, 0)), gain]
        args = (h, y0, y1, gates, g)
    return pl.pallas_call(
        kern,
        grid=(t // tm,),
        in_specs=in_specs,
        out_specs=tile,
        out_shape=jax.ShapeDtypeStruct((t, d), F32),
        compiler_params=_params(("arbitrary",)),
    )(*args)


ROPE_ROWS = 8


def _rope_table_kernel(pos_ref, freq_ref, cos_ref, sa_ref, sb_ref):
    dim = lax.broadcasted_iota(jnp.int32, (LANES, LANES), 1) % HEAD_DIM
    first = dim < ROPE_HALF
    second = jnp.logical_and(dim >= ROPE_HALF, dim < ROPE_DIM)
    freq = freq_ref[...]
    for r in range(ROPE_ROWS):
        ang = (freq * pos_ref[r:r + 1, :].astype(F32)).T
        cos, sin = jnp.cos(ang), jnp.sin(ang)
        rows = slice(r * LANES, (r + 1) * LANES)
        cos_ref[rows, :] = jnp.where(jnp.logical_or(first, second), cos, 1.0)
        sa_ref[rows, :] = jnp.where(second, sin, 0.0)
        sb_ref[rows, :] = jnp.where(first, -sin, 0.0)


def _rope_tables(positions):
    inv_freq = ROPE_THETA ** (-jnp.arange(0, ROPE_DIM, 2, dtype=F32) / ROPE_DIM)
    freq = jnp.tile(inv_freq, LANES // ROPE_HALF).reshape(LANES, 1)
    t = positions.size
    tm = ROPE_ROWS * LANES
    table = jax.ShapeDtypeStruct((t, LANES), F32)
    tile = pl.BlockSpec((tm, LANES), lambda i: (i, 0))
    return pl.pallas_call(
        _rope_table_kernel,
        grid=(t // tm,),
        in_specs=[pl.BlockSpec((ROPE_ROWS, LANES), lambda i: (i, 0)),
                  pl.BlockSpec((LANES, 1), lambda i: (0, 0))],
        out_specs=[tile, tile, tile],
        out_shape=[table, table, table],
        compiler_params=_params(("arbitrary",)),
    )(positions.reshape(t // LANES, LANES), freq)


def _decay_selectors():
    dst = jnp.arange(LANES, dtype=jnp.int32)[:, None]
    src = jnp.arange(GATE_ROWS, dtype=jnp.int32)[None, :]
    sels = []
    for s in range(N_SPLIT):
        hit = jnp.logical_and(src < N_FOX_HEADS, dst == N_SPLIT * src + s)
        sels.append(hit.astype(BF16))
    return jnp.stack(sels)


def _split_w_in(w):
    w_f = jnp.pad(w[:, O_F:O_F + N_FOX_HEADS].T, ((0, GATE_ROWS - N_FOX_HEADS), (0, 0))).astype(BF16)
    return w.astype(BF16), w_f


def kernel(x, mem, positions, norm_mix_g, w_in, lam_q1, lam_k1, lam_q2, lam_k2, diff_subln_g, fox_bias, w_out, norm_mem_g, mem_norm_g, w_mq, w_mkv, w_mo, norm_ffn_g, w_ff1, w_ff3, w_ff2, w_router, w_e1, w_e3, w_e2, final_norm_g):
    batch, seq, d = x.shape
    n_mem = mem.shape[1]
    depth = w_in.shape[0]
    t = batch * seq
    cos_t, sa_t, sb_t = _rope_tables(positions)
    sel = _decay_selectors()
    h = x.reshape(t, d)
    mem2 = mem.reshape(batch * n_mem, d)
    row = lambda v: v.reshape(1, -1).astype(F32)

    moe_parts = None
    for l in range(depth):
        lam_init = 0.8 - 0.6 * math.exp(-0.3 * l)
        lam = (jnp.exp(jnp.sum(lam_q1[l] * lam_k1[l])) - jnp.exp(jnp.sum(lam_q2[l] * lam_k2[l]))
               + lam_init).reshape(1).astype(F32)
        w_main, w_f = _split_w_in(w_in[l])
        f_bias = jnp.pad(fox_bias[l], (0, GATE_ROWS - N_FOX_HEADS)).reshape(GATE_ROWS, 1)

        qk, vt, ct = _proj_call(h, row(norm_mix_g[l]), w_main, w_f, f_bias, sel, cos_t, sa_t, sb_t, batch, seq)
        qk3 = qk.reshape(batch, seq, qk.shape[1])
        o_d = _diff_call(lam, qk3, vt, diff_subln_g[l].reshape(DIFF_V_DIM, 1), 1.0 - lam_init, batch, seq)
        o_f = _fox_call(qk3, vt, ct, batch, seq)

        w_o = w_out[l].astype(BF16)
        kv = _norm_matmul_call(mem2, row(mem_norm_g[l]), w_mkv[l].astype(BF16), 512, 1024)
        h = _mix_mem_call(h, o_d.reshape(t, D_DIFF_OUT), o_f.reshape(t, D_FOX),
                          w_o[:D_DIFF_OUT], w_o[D_DIFF_OUT:], row(norm_mem_g[l]),
                          w_mq[l].astype(BF16), kv, w_mo[l].astype(BF16), seq, n_mem)

        i = l // 2
        if l % 2 == 0:
            h = _dense_ffn_call(h, row(norm_ffn_g[l]), w_ff1[i].astype(BF16), w_ff3[i].astype(BF16),
                                w_ff2[i].astype(BF16))
        else:
            w_r = jnp.pad(w_router[i], ((0, 0), (0, LANES - N_EXPERTS))).astype(BF16)
            y0, y1, gates = _moe_ffn(h, row(norm_ffn_g[l]), w_r, w_e1, w_e3, w_e2, i)
            if l == depth - 1:
                moe_parts = (y0, y1, gates)
            else:
                h = h + gates[:, 0:1] * y0.astype(F32) + gates[:, 1:2] * y1.astype(F32)

    return _final_norm_call(h, row(final_norm_g), moe_parts).reshape(batch, seq, d)
```

```python
import functools
import math

import jax
import jax.numpy as jnp
from jax import lax
from jax.experimental import pallas as pl
from jax.experimental.pallas import tpu as pltpu

F32 = jnp.float32
BF16 = jnp.bfloat16

HEAD_DIM = 64
N_DIFF_HEADS = 4
DIFF_V_DIM = 128
N_FOX_HEADS = 8
D_DIFF_QK = N_DIFF_HEADS * HEAD_DIM
D_DIFF_OUT = N_DIFF_HEADS * DIFF_V_DIM
D_FOX = N_FOX_HEADS * HEAD_DIM
ATTN_SCALE = 1.0 / math.sqrt(HEAD_DIM)
ROPE_DIM = HEAD_DIM // 4
ROPE_HALF = ROPE_DIM // 2
ROPE_THETA = 500000.0
MEM_HEADS = 4
N_EXPERTS = 8
EPS = 1e-6
SUBLN_EPS = 1e-5
LOG2E = 1.0 / math.log(2.0)

LANES = 128
VMEM_LIMIT = 48 * 1024 * 1024

TM_PROJ = 512
TN_PROJ = 512
N_QK_BLK = 4
TQ = 512
TK = 512
TM_MIX = 512
TM_KV = 512
TN_KV = 1024
TM_FFN = 1024
TF_DENSE = 256
TF_MOE = 512
MOE_SPLIT = 4
V_ROWS = LANES + 8
N_SPLIT = 3
GATE_ROWS = 16


def _params(sem):
    return pltpu.CompilerParams(dimension_semantics=sem, vmem_limit_bytes=VMEM_LIMIT)


def _rms_scale(x, eps):
    return lax.rsqrt(jnp.mean(x * x, axis=-1, keepdims=True) + eps)


def _split3(v):
    hi = v.astype(BF16)
    r1 = v - hi.astype(F32)
    mid = r1.astype(BF16)
    lo = (r1 - mid.astype(F32)).astype(BF16)
    return hi, mid, lo


O_VD = 4 * D_DIFF_QK
O_QF = O_VD + D_DIFF_OUT
O_KF = O_QF + D_FOX
O_VF = O_KF + D_FOX
O_F = O_VF + D_FOX
Q_SCALE = ATTN_SCALE * LOG2E
QK_COLS = 2 * TN_PROJ + 2 * N_FOX_HEADS * LANES
_PROJ_BLOCKS = (
    (0, "rope", 0, Q_SCALE),
    (2 * D_DIFF_QK, "rope", TN_PROJ, 1.0),
    (O_QF, "fox_q", 2 * TN_PROJ, Q_SCALE),
    (O_KF, "fox_k", 2 * TN_PROJ + N_FOX_HEADS * LANES, 1.0),
    (O_VD, "vt", 0, 1.0),
    (O_VF, "vt", N_DIFF_HEADS, 1.0),
)


def _proj_kernel(x_ref, g_ref, w_ref, wf_ref, fb_ref, sel_ref, cos_ref, sa_ref, sb_ref,
                 qk_ref, vt_ref, c_ref, carry_ref, *, tiles_per_seq):
    i = pl.program_id(0)
    tm = x_ref.shape[0]
    x = x_ref[...]
    xnb = ((x * _rms_scale(x, EPS)) * g_ref[...]).astype(BF16)

    logit = lax.dot_general(wf_ref[...], xnb, (((1,), (1,)), ((), ())),
                            preferred_element_type=F32) + fb_ref[...]
    logf = jnp.minimum(logit, 0.0) - jnp.log1p(jnp.exp(-jnp.abs(logit)))

    @pl.when(i % tiles_per_seq == 0)
    def _():
        carry_ref[...] = jnp.zeros_like(carry_ref)

    r = lax.broadcasted_iota(jnp.int32, (tm, tm), 0)
    cidx = lax.broadcasted_iota(jnp.int32, (tm, tm), 1)
    tri = (r <= cidx).astype(BF16)
    hi, mid, lo = _split3(logf)
    cs = (jnp.dot(hi, tri, preferred_element_type=F32)
          + jnp.dot(mid, tri, preferred_element_type=F32)
          + jnp.dot(lo, tri, preferred_element_type=F32)) + carry_ref[:, 0:1]
    carry_ref[...] = jnp.broadcast_to(cs[:, tm - 1:tm], carry_ref.shape)
    c2 = cs * LOG2E
    c_ref[0] = c2
    hi, mid, lo = _split3(-c2)
    kx = (jnp.dot(sel_ref[0], hi, preferred_element_type=F32)
          + jnp.dot(sel_ref[1], mid, preferred_element_type=F32)
          + jnp.dot(sel_ref[2], lo, preferred_element_type=F32)).T
    lane = lax.broadcasted_iota(jnp.int32, (tm, LANES), 1)
    low_half = lane < HEAD_DIM
    extra = [jnp.logical_and(lane >= HEAD_DIM, lane < HEAD_DIM + N_SPLIT), lane < N_SPLIT]

    cos = cos_ref[...]
    sa = sa_ref[...]
    sb = sb_ref[...]
    ones_row = (lax.broadcasted_iota(jnp.int32, (V_ROWS - LANES, TK), 0) == 0).astype(vt_ref.dtype)
    for src, kind, dst, scale in _PROJ_BLOCKS:
        acc = jnp.dot(xnb, w_ref[:, src:src + TN_PROJ], preferred_element_type=F32)
        if scale != 1.0:
            acc = acc * scale
        if kind == "rope":
            for s in range(TN_PROJ // LANES):
                t = acc[:, s * LANES:(s + 1) * LANES]
                rot = (t * cos + pltpu.roll(t, ROPE_HALF, 1) * sa
                       + pltpu.roll(t, LANES - ROPE_HALF, 1) * sb)
                qk_ref[:, dst + s * LANES:dst + (s + 1) * LANES] = rot.astype(qk_ref.dtype)
        elif kind in ("fox_q", "fox_k"):
            for pr in range(TN_PROJ // LANES):
                pair = acc[:, pr * LANES:(pr + 1) * LANES]
                for par in range(2):
                    head = 2 * pr + par
                    if kind == "fox_q":
                        other = extra[par].astype(F32)
                    else:
                        first = HEAD_DIM if par == 0 else 0
                        moved = pltpu.roll(kx, (first - N_SPLIT * head) % LANES, 1)
                        other = jnp.where(extra[par], moved, 0.0)
                    keep = low_half if par == 0 else jnp.logical_not(low_half)
                    qk_ref[:, dst + head * LANES:dst + (head + 1) * LANES] = (
                        jnp.where(keep, pair, other).astype(qk_ref.dtype))
        else:
            at = acc.T
            for hd in range(TN_PROJ // LANES):
                vt_ref[0, dst + hd, 0, 0:LANES, :] = at[hd * LANES:(hd + 1) * LANES, :].astype(vt_ref.dtype)
                vt_ref[0, dst + hd, 0, LANES:V_ROWS, :] = ones_row


def _proj_call(h, g, w_main, w_f, f_bias, sel, cos_t, sa_t, sb_t, batch, seq):
    t, d = h.shape
    tm = TM_PROJ
    assert tm == TK
    tiles_per_seq = seq // tm
    n_col = w_main.shape[1]
    n_vh = (len(_PROJ_BLOCKS) - N_QK_BLK) * (TN_PROJ // LANES)
    kern = functools.partial(_proj_kernel, tiles_per_seq=tiles_per_seq)
    const2 = lambda i: (0, 0)
    return pl.pallas_call(
        kern,
        grid=(t // tm,),
        in_specs=[
            pl.BlockSpec((tm, d), lambda i: (i, 0)),
            pl.BlockSpec((1, d), const2),
            pl.BlockSpec((d, n_col), const2),
            pl.BlockSpec((GATE_ROWS, d), const2),
            pl.BlockSpec((GATE_ROWS, 1), const2),
            pl.BlockSpec((N_SPLIT, LANES, GATE_ROWS), lambda i: (0, 0, 0)),
            pl.BlockSpec((tm, LANES), lambda i: (i, 0)),
            pl.BlockSpec((tm, LANES), lambda i: (i, 0)),
            pl.BlockSpec((tm, LANES), lambda i: (i, 0)),
        ],
        out_specs=[
            pl.BlockSpec((tm, QK_COLS), lambda i: (i, 0)),
            pl.BlockSpec((1, n_vh, 1, V_ROWS, TK), lambda i: (i // tiles_per_seq, 0, i % tiles_per_seq, 0, 0)),
            pl.BlockSpec((1, GATE_ROWS, tm), lambda i: (i // tiles_per_seq, 0, i % tiles_per_seq)),
        ],
        out_shape=[
            jax.ShapeDtypeStruct((t, QK_COLS), BF16),
            jax.ShapeDtypeStruct((batch, n_vh, seq // TK, V_ROWS, TK), BF16),
            jax.ShapeDtypeStruct((batch, GATE_ROWS, seq), F32),
        ],
        scratch_shapes=[pltpu.VMEM((GATE_ROWS, LANES), F32)],
        compiler_params=_params(("arbitrary",)),
    )(h, g, w_main, w_f, f_bias, sel, cos_t, sa_t, sb_t)


def _head_mask(q, half):
    lane = lax.broadcasted_iota(jnp.int32, q.shape, 1)
    return jnp.where(lane // HEAD_DIM == half, q, jnp.zeros_like(q))


def _causal_mask():
    kpos = lax.broadcasted_iota(jnp.int32, (TK, TQ), 0)
    qpos = lax.broadcasted_iota(jnp.int32, (TK, TQ), 1)
    return kpos <= qpos


def _scores_t(k, q):
    return lax.dot_general(k, q, (((1,), (1,)), ((), ())), preferred_element_type=F32)


def _online_update(u, vt, shift, acc_ref, m_ref):
    mu = jnp.max(u, axis=0, keepdims=True) + shift
    m_old = m_ref[...]
    m_new = jnp.maximum(m_old, mu)
    p = jnp.exp2(u - (m_new - shift)).astype(BF16)
    pv = jnp.dot(vt, p, preferred_element_type=F32)
    acc_ref[...] = jnp.exp2(m_old - m_new) * acc_ref[...] + pv
    m_ref[...] = m_new


def _normalised(acc_ref):
    acc = acc_ref[...]
    return acc[:LANES] * (1.0 / acc[LANES:LANES + 1])


N_TILES = 4
UNROLL = 4


def _attn_scratch():
    return ([pltpu.VMEM((TK, TQ), F32)] * (3 * N_TILES) + [pltpu.VMEM((V_ROWS, TQ), F32)] * N_TILES
            + [pltpu.VMEM((1, TQ), F32)] * N_TILES)


def _attend(qi, qk_fn, qk_next_fn, vt_fn, shifts, scratch, finish):
    assert TQ == TK
    s_a, s_b, s_c = (scratch[n * N_TILES:(n + 1) * N_TILES] for n in range(3))
    acc = scratch[3 * N_TILES:4 * N_TILES]
    m = scratch[4 * N_TILES:]
    for t in range(N_TILES):
        m[t][...] = jnp.full(m[t].shape, -jnp.inf, F32)
        acc[t][...] = jnp.zeros(acc[t].shape, F32)

    def stage(j, cur, nxt, masked):
        for t in range(N_TILES):
            nxt[t][...] = qk_next_fn(t) if masked else qk_fn(j + 1, t)
            u = cur[t][...]
            if masked:
                u = jnp.where(_causal_mask(), u, -jnp.inf)
            _online_update(u, vt_fn(j, t), shifts[t], acc[t], m[t])

    @pl.when(qi == 0)
    def _():
        for t in range(N_TILES):
            s_a[t][...] = qk_fn(0, t)
        stage(0, s_a, s_c, True)
        finish(acc)

    @pl.when(qi > 0)
    def _():
        stage(0, s_c, s_a, False)
        bufs = (s_a, s_b)

        def body(i, carry):
            for n in range(UNROLL):
                stage(UNROLL * i + 1 + n, bufs[n % 2], bufs[(n + 1) % 2], False)
            return carry

        lax.fori_loop(0, (qi - 1) // UNROLL, body, 0)

        for rem in range(UNROLL):
            @pl.when((qi - 1) % UNROLL == rem)
            def _(rem=rem):
                for n in range(rem):
                    stage(qi - rem + n, bufs[n % 2], bufs[(n + 1) % 2], False)
                stage(qi, bufs[rem % 2], s_c, True)
                finish(acc)


def _fox_kernel(q_ref, qn_ref, k_ref, vt_ref, ct_ref, o_ref, *scratch):
    grp = pl.program_id(1)
    qi = pl.program_id(2)
    cqs = [ct_ref[0, pl.ds(N_TILES * grp + t, 1), :] for t in range(N_TILES)]

    def scores(j, t, qr):
        start = pl.multiple_of(j * TK, TK)
        return _scores_t(k_ref[0, pl.ds(start, TK), t * LANES:(t + 1) * LANES],
                         qr[0, :, t * LANES:(t + 1) * LANES])

    def vt_fn(j, t):
        return vt_ref[0, t // 2, j]

    def finish(acc):
        for pp in range(2):
            ot = jnp.concatenate([_normalised(acc[2 * pp])[:HEAD_DIM],
                                  _normalised(acc[2 * pp + 1])[HEAD_DIM:]], axis=0)
            o_ref[0, :, pp * LANES:(pp + 1) * LANES] = ot.T.astype(o_ref.dtype)

    _attend(qi, lambda j, t: scores(j, t, q_ref), lambda t: scores(0, t, qn_ref), vt_fn, cqs, scratch, finish)


def _fox_call(qk, vt, ct, batch, seq):
    nkb = seq // TK
    nq = seq // TQ
    w4 = N_TILES * LANES
    return pl.pallas_call(
        _fox_kernel,
        grid=(batch, N_FOX_HEADS // N_TILES, nq),
        in_specs=[
            pl.BlockSpec((1, TQ, w4), lambda b, g, qi: (b, qi, 2 + g)),
            pl.BlockSpec((1, TQ, w4), lambda b, g, qi: (b, jnp.minimum(qi + 1, nq - 1), 2 + g)),
            pl.BlockSpec((1, seq, w4), lambda b, g, qi: (b, 0, 4 + g)),
            pl.BlockSpec((1, 2, nkb, V_ROWS, TK), lambda b, g, qi: (b, N_DIFF_HEADS // 2 + g, 0, 0, 0)),
            pl.BlockSpec((1, GATE_ROWS, TQ), lambda b, g, qi: (b, 0, qi)),
        ],
        out_specs=pl.BlockSpec((1, TQ, 2 * LANES), lambda b, g, qi: (b, qi, g)),
        out_shape=jax.ShapeDtypeStruct((batch, seq, D_FOX), BF16),
        scratch_shapes=_attn_scratch(),
        compiler_params=_params(("arbitrary", "arbitrary", "arbitrary")),
    )(qk, qk, qk, vt, ct)


def _diff_kernel(lam_ref, q1_ref, q2_ref, q1n_ref, q2n_ref, k1_ref, k2_ref, vt_ref, g_ref, o_ref, *scratch,
                 out_scale):
    qi = pl.program_id(2)
    k_refs = [k1_ref, k2_ref]
    qms = [_head_mask(r[0], hh) for hh in range(2) for r in (q1_ref, q2_ref)]
    qms_next = [_head_mask(r[0], hh) for hh in range(2) for r in (q1n_ref, q2n_ref)]

    def scores(j, t, qm):
        start = pl.multiple_of(j * TK, TK)
        return _scores_t(k_refs[t % 2][0, pl.ds(start, TK), :], qm[t])

    def vt_fn(j, t):
        return vt_ref[0, t // 2, j]

    def finish(acc):
        for hh in range(2):
            ot = _normalised(acc[2 * hh]) - lam_ref[0] * _normalised(acc[2 * hh + 1])
            y = ot * lax.rsqrt(jnp.mean(ot * ot, axis=0, keepdims=True) + SUBLN_EPS)
            y = (y * g_ref[...]) * out_scale
            o_ref[0, :, hh * LANES:(hh + 1) * LANES] = y.T.astype(o_ref.dtype)

    _attend(qi, lambda j, t: scores(j, t, qms), lambda t: scores(0, t, qms_next), vt_fn,
            [0.0] * N_TILES, scratch, finish)


def _diff_call(lam, qk, vt, g_col, out_scale, batch, seq):
    nkb = seq // TK
    nq = seq // TQ
    n_pairs = N_DIFF_HEADS // 2
    kern = functools.partial(_diff_kernel, out_scale=out_scale)
    grid_spec = pltpu.PrefetchScalarGridSpec(
        num_scalar_prefetch=1,
        grid=(batch, n_pairs, nq),
        in_specs=[
            pl.BlockSpec((1, TQ, LANES), lambda b, p, qi, lam: (b, qi, p)),
            pl.BlockSpec((1, TQ, LANES), lambda b, p, qi, lam: (b, qi, 2 + p)),
            pl.BlockSpec((1, TQ, LANES), lambda b, p, qi, lam: (b, jnp.minimum(qi + 1, nq - 1), p)),
            pl.BlockSpec((1, TQ, LANES), lambda b, p, qi, lam: (b, jnp.minimum(qi + 1, nq - 1), 2 + p)),
            pl.BlockSpec((1, seq, LANES), lambda b, p, qi, lam: (b, 0, 4 + p)),
            pl.BlockSpec((1, seq, LANES), lambda b, p, qi, lam: (b, 0, 6 + p)),
            pl.BlockSpec((1, 2, nkb, V_ROWS, TK), lambda b, p, qi, lam: (b, p, 0, 0, 0)),
            pl.BlockSpec((DIFF_V_DIM, 1), lambda b, p, qi, lam: (0, 0)),
        ],
        out_specs=pl.BlockSpec((1, TQ, 2 * LANES), lambda b, p, qi, lam: (b, qi, p)),
        scratch_shapes=_attn_scratch(),
    )
    return pl.pallas_call(
        kern,
        grid_spec=grid_spec,
        out_shape=jax.ShapeDtypeStruct((batch, seq, D_DIFF_OUT), BF16),
        compiler_params=_params(("arbitrary", "arbitrary", "arbitrary")),
    )(lam, qk, qk, qk, qk, qk, qk, vt, g_col)


def _norm_matmul_kernel(x_ref, g_ref, w_ref, o_ref):
    x = x_ref[...]
    xn = ((x * _rms_scale(x, EPS)) * g_ref[...]).astype(BF16)
    o_ref[...] = jnp.dot(xn, w_ref[...], preferred_element_type=F32).astype(o_ref.dtype)


def _norm_matmul_call(x, g, w, tm, tn):
    t, d = x.shape
    n = w.shape[1]
    return pl.pallas_call(
        _norm_matmul_kernel,
        grid=(t // tm, n // tn),
        in_specs=[pl.BlockSpec((tm, d), lambda i, j: (i, 0)),
                  pl.BlockSpec((1, d), lambda i, j: (0, 0)),
                  pl.BlockSpec((d, tn), lambda i, j: (0, j))],
        out_specs=pl.BlockSpec((tm, tn), lambda i, j: (i, j)),
        out_shape=jax.ShapeDtypeStruct((t, n), BF16),
        compiler_params=_params(("arbitrary", "arbitrary")),
    )(x, g, w)


def _mix_mem_kernel(h_ref, od_ref, of_ref, wod_ref, wof_ref, g_ref, wq_ref, k_ref, v_ref, wo_ref,
                    o_ref, *, mem_scale):
    h1 = (h_ref[...]
          + jnp.dot(od_ref[...], wod_ref[...], preferred_element_type=F32)
          + jnp.dot(of_ref[...], wof_ref[...], preferred_element_type=F32))
    xn = ((h1 * _rms_scale(h1, EPS)) * g_ref[...]).astype(BF16)
    q = jnp.dot(xn, wq_ref[...], preferred_element_type=F32).astype(BF16)
    d = q.shape[1]
    hd = d // MEM_HEADS
    outs = []
    for hh in range(MEM_HEADS):
        qh = q[:, hh * hd:(hh + 1) * hd]
        kh = k_ref[:, hh * hd:(hh + 1) * hd]
        vh = v_ref[:, hh * hd:(hh + 1) * hd]
        s = lax.dot_general(qh, kh, (((1,), (1,)), ((), ())),
                            preferred_element_type=F32) * mem_scale
        m = jnp.max(s, axis=-1, keepdims=True)
        e = jnp.exp(s - m)
        p = e * (1.0 / jnp.sum(e, axis=-1, keepdims=True))
        outs.append(jnp.dot(p.astype(BF16), vh, preferred_element_type=F32).astype(BF16))
    o = jnp.concatenate(outs, axis=1)
    o_ref[...] = h1 + jnp.dot(o, wo_ref[...], preferred_element_type=F32)


def _mix_mem_call(h, o_d, o_f, w_od, w_of, g, w_q, kv, w_o, seq, n_mem):
    t, d = h.shape
    tm = TM_MIX
    tiles_per_seq = seq // tm
    kern = functools.partial(_mix_mem_kernel, mem_scale=1.0 / math.sqrt(d // MEM_HEADS))
    const = lambda i: (0, 0)
    return pl.pallas_call(
        kern,
        grid=(t // tm,),
        in_specs=[
            pl.BlockSpec((tm, d), lambda i: (i, 0)),
            pl.BlockSpec((tm, D_DIFF_OUT), lambda i: (i, 0)),
            pl.BlockSpec((tm, D_FOX), lambda i: (i, 0)),
            pl.BlockSpec((D_DIFF_OUT, d), const),
            pl.BlockSpec((D_FOX, d), const),
            pl.BlockSpec((1, d), const),
            pl.BlockSpec((d, d), const),
            pl.BlockSpec((n_mem, d), lambda i: (i // tiles_per_seq, 0)),
            pl.BlockSpec((n_mem, d), lambda i: (i // tiles_per_seq, 1)),
            pl.BlockSpec((d, d), const),
        ],
        out_specs=pl.BlockSpec((tm, d), lambda i: (i, 0)),
        out_shape=jax.ShapeDtypeStruct((t, d), F32),
        compiler_params=_params(("arbitrary",)),
    )(h, o_d, o_f, w_od, w_of, g, w_q, kv, kv, w_o)


def _swiglu_chunk(x, w1, w3, w2):
    a = jnp.dot(x, w1, preferred_element_type=F32)
    b = jnp.dot(x, w3, preferred_element_type=F32)
    mid = (a * (1.0 / (1.0 + jnp.exp(-a)))) * b
    return jnp.dot(mid.astype(BF16), w2, preferred_element_type=F32)


def _dense_ffn_kernel(h_ref, g_ref, w1_ref, w3_ref, w2_ref, o_ref, xn_ref):
    f = pl.program_id(1)

    @pl.when(f == 0)
    def _():
        x = h_ref[...]
        xn_ref[...] = ((x * _rms_scale(x, EPS)) * g_ref[...]).astype(BF16)
        o_ref[...] = x

    o_ref[...] += _swiglu_chunk(xn_ref[...], w1_ref[...], w3_ref[...], w2_ref[...])


def _dense_ffn_call(h, g, w1, w3, w2):
    t, d = h.shape
    tm, tf = TM_FFN, TF_DENSE
    dff = w1.shape[1]
    return pl.pallas_call(
        _dense_ffn_kernel,
        grid=(t // tm, dff // tf),
        in_specs=[pl.BlockSpec((tm, d), lambda i, f: (i, 0)),
                  pl.BlockSpec((1, d), lambda i, f: (0, 0)),
                  pl.BlockSpec((d, tf), lambda i, f: (0, f)),
                  pl.BlockSpec((d, tf), lambda i, f: (0, f)),
                  pl.BlockSpec((tf, d), lambda i, f: (f, 0))],
        out_specs=pl.BlockSpec((tm, d), lambda i, f: (i, 0)),
        out_shape=jax.ShapeDtypeStruct((t, d), F32),
        scratch_shapes=[pltpu.VMEM((tm, d), BF16)],
        compiler_params=_params(("arbitrary", "arbitrary")),
    )(h, g, w1, w3, w2)


def _router_kernel(h_ref, g_ref, wr_ref, hn_ref, idx_ref, gate_ref):
    x = h_ref[...]
    xn = ((x * _rms_scale(x, EPS)) * g_ref[...]).astype(BF16)
    hn_ref[...] = xn
    logits = jnp.dot(xn, wr_ref[...], preferred_element_type=F32)
    lane = lax.broadcasted_iota(jnp.int32, logits.shape, 1)
    logits = jnp.where(lane < N_EXPERTS, logits, -jnp.inf)
    v1 = jnp.max(logits, axis=-1, keepdims=True)
    i1 = jnp.min(jnp.where(logits == v1, lane, LANES), axis=-1, keepdims=True)
    rest = jnp.where(lane == i1, -jnp.inf, logits)
    v2 = jnp.max(rest, axis=-1, keepdims=True)
    i2 = jnp.min(jnp.where(rest == v2, lane, LANES), axis=-1, keepdims=True)
    e = jnp.exp(v2 - v1)
    inv = 1.0 / (1.0 + e)
    idx_ref[...] = jnp.where(lane == 0, i1, jnp.where(lane == 1, i2, 0))
    gate_ref[...] = jnp.where(lane == 0, inv, jnp.where(lane == 1, e * inv, 0.0))


def _router_call(h, g, w_r):
    t, d = h.shape
    tm = TM_FFN
    return pl.pallas_call(
        _router_kernel,
        grid=(t // tm,),
        in_specs=[pl.BlockSpec((tm, d), lambda i: (i, 0)),
                  pl.BlockSpec((1, d), lambda i: (0, 0)),
                  pl.BlockSpec((d, LANES), lambda i: (0, 0))],
        out_specs=[pl.BlockSpec((tm, d), lambda i: (i, 0)),
                   pl.BlockSpec((tm, LANES), lambda i: (i, 0)),
                   pl.BlockSpec((tm, LANES), lambda i: (i, 0))],
        out_shape=[jax.ShapeDtypeStruct((t, d), BF16),
                   jax.ShapeDtypeStruct((t, LANES), jnp.int32),
                   jax.ShapeDtypeStruct((t, LANES), F32)],
        compiler_params=_params(("arbitrary",)),
    )(h, g, w_r)


def _expert_ffn_kernel(te_ref, nu_ref, x_ref, w1_ref, w3_ref, w2_ref, *rest, n_chunks, tile0):
    o_ref, acc_ref = rest[-2:]
    i = pl.program_id(0)
    f = pl.program_id(1)

    @pl.when(tile0 + i < nu_ref[0])
    def _():
        @pl.when(f == 0)
        def _():
            acc_ref[...] = jnp.zeros_like(acc_ref)

        acc_ref[...] += _swiglu_chunk(x_ref[...], w1_ref[0, 0].astype(BF16), w3_ref[0, 0].astype(BF16),
                                      w2_ref[0, 0].astype(BF16))

        @pl.when(f == n_chunks - 1)
        def _():
            o_ref[...] = acc_ref[...].astype(o_ref.dtype)

    @pl.when(jnp.logical_and(tile0 >= nu_ref[0], jnp.logical_and(i == 0, f == 0)))
    def _():
        o_ref[...] = jnp.zeros_like(o_ref)


def _expert_ffn_call(tile_expert, n_used, x_rows, w1, w3, w2, layer, tile0, p_rows, y_prev=None):
    rows, d = x_rows.shape
    tm, tf = TM_FFN, TF_MOE
    n_local = rows // tm
    dff = w1.shape[3]
    n_chunks = dff // tf
    kern = functools.partial(_expert_ffn_kernel, n_chunks=n_chunks, tile0=tile0)

    def local(i, nu):
        return jnp.minimum(i, jnp.clip(nu[0] - 1 - tile0, 0, n_local - 1))

    def chunk(i, f, nu):
        return jnp.where(tile0 + i < nu[0], f, n_chunks - 1)

    def expert(i, te):
        return te[tile0 + i]

    in_specs = [
        pl.BlockSpec((tm, d), lambda i, f, te, nu: (local(i, nu), 0)),
        pl.BlockSpec((1, 1, d, tf), lambda i, f, te, nu: (layer, expert(i, te), 0, chunk(i, f, nu))),
        pl.BlockSpec((1, 1, d, tf), lambda i, f, te, nu: (layer, expert(i, te), 0, chunk(i, f, nu))),
        pl.BlockSpec((1, 1, tf, d), lambda i, f, te, nu: (layer, expert(i, te), chunk(i, f, nu), 0)),
    ]
    args = [tile_expert, n_used, x_rows, w1, w3, w2]
    aliases = {}
    if y_prev is not None:
        in_specs.append(pl.BlockSpec(memory_space=pl.ANY))
        aliases = {len(args): 0}
        args.append(y_prev)
    grid_spec = pltpu.PrefetchScalarGridSpec(
        num_scalar_prefetch=2,
        grid=(n_local, n_chunks),
        in_specs=in_specs,
        out_specs=pl.BlockSpec((tm, d), lambda i, f, te, nu: (tile0 + local(i, nu), 0)),
        scratch_shapes=[pltpu.VMEM((tm, d), F32)],
    )
    return pl.pallas_call(
        kern,
        grid_spec=grid_spec,
        out_shape=jax.ShapeDtypeStruct((p_rows, d), BF16),
        input_output_aliases=aliases,
        compiler_params=_params(("arbitrary", "arbitrary")),
    )(*args)


def _route(idx, tm):
    t = idx.shape[0]
    n_assign = 2 * t
    p_rows = n_assign + N_EXPERTS * tm
    e_flat = idx.reshape(-1)
    onehot = (e_flat[None, :] == jnp.arange(N_EXPERTS, dtype=jnp.int32)[:, None]).astype(jnp.int32)
    running = jnp.cumsum(onehot, axis=1)
    counts = running[:, -1]
    padded = ((counts + tm - 1) // tm) * tm
    pend = jnp.cumsum(padded)
    pstart = pend - padded
    gstart = jnp.cumsum(counts) - counts
    dest = jnp.sum(onehot * (running - 1 + pstart[:, None]), axis=0)
    n_tiles = p_rows // tm
    n_used = (pend[-1] // tm).astype(jnp.int32)
    tile_start = jnp.arange(n_tiles, dtype=jnp.int32) * tm
    tile_expert = jnp.sum((tile_start[:, None] >= pend[None, :]).astype(jnp.int32), axis=1)
    last_expert = jnp.sum((((n_used - 1) * tm) >= pend).astype(jnp.int32))
    tile_expert = jnp.where(tile_start < pend[-1], tile_expert, last_expert).astype(jnp.int32)
    order = jnp.sort(e_flat * n_assign + jnp.arange(n_assign, dtype=jnp.int32)) % n_assign
    within = ((tile_start - pstart[tile_expert])[:, None]
              + jnp.arange(tm, dtype=jnp.int32)[None, :])
    src = jnp.clip(gstart[tile_expert][:, None] + within, 0, n_assign - 1)
    row_token = jnp.where(within < counts[tile_expert][:, None], order[src] // 2, 0)
    return row_token.reshape(p_rows), dest.reshape(t, 2), tile_expert, n_used.reshape(1)


def _moe_ffn(h, g, w_r, w1, w3, w2, layer):
    hn, idx_l, gate_l = _router_call(h, g, w_r)
    row_token, slot, tile_expert, n_used = _route(idx_l[:, :2], TM_FFN)
    p_rows = row_token.shape[0]
    assert p_rows % (MOE_SPLIT * TM_FFN) == 0
    rows = p_rows // MOE_SPLIT
    y = None
    for c in range(MOE_SPLIT):
        x_rows = jnp.take(hn, row_token[c * rows:(c + 1) * rows], axis=0)
        y = _expert_ffn_call(tile_expert, n_used, x_rows, w1, w3, w2, layer, c * (rows // TM_FFN), p_rows, y)
    return jnp.take(y, slot[:, 0], axis=0), jnp.take(y, slot[:, 1], axis=0), gate_l


def _final_norm_kernel(x_ref, g_ref, o_ref):
    x = x_ref[...]
    o_ref[...] = (x * _rms_scale(x, EPS)) * g_ref[...]


def _combine_norm_kernel(h_ref, y0_ref, y1_ref, gate_ref, g_ref, o_ref):
    x = (h_ref[...] + gate_ref[:, 0:1] * y0_ref[...].astype(F32)
         + gate_ref[:, 1:2] * y1_ref[...].astype(F32))
    o_ref[...] = (x * _rms_scale(x, EPS)) * g_ref[...]


def _final_norm_call(h, g, moe_parts=None):
    t, d = h.shape
    tm = TM_FFN
    tile = pl.BlockSpec((tm, d), lambda i: (i, 0))
    gain = pl.BlockSpec((1, d), lambda i: (0, 0))
    if moe_parts is None:
        kern, in_specs, args = _final_norm_kernel, [tile, gain], (h, g)
    else:
        y0, y1, gates = moe_parts
        kern = _combine_norm_kernel
        in_specs = [tile, tile, tile, pl.BlockSpec((tm, LANES), lambda i: (i, 0)), gain]
        args = (h, y0, y1, gates, g)
    return pl.pallas_call(
        kern,
        grid=(t // tm,),
        in_specs=in_specs,
        out_specs=tile,
        out_shape=jax.ShapeDtypeStruct((t, d), F32),
        compiler_params=_params(("arbitrary",)),
    )(*args)


ROPE_ROWS = 8


def _rope_table_kernel(pos_ref, freq_ref, cos_ref, sa_ref, sb_ref):
    dim = lax.broadcasted_iota(jnp.int32, (LANES, LANES), 1) % HEAD_DIM
    first = dim < ROPE_HALF
    second = jnp.logical_and(dim >= ROPE_HALF, dim < ROPE_DIM)
    freq = freq_ref[...]
    for r in range(ROPE_ROWS):
        ang = (freq * pos_ref[r:r + 1, :].astype(F32)).T
        cos, sin = jnp.cos(ang), jnp.sin(ang)
        rows = slice(r * LANES, (r + 1) * LANES)
        cos_ref[rows, :] = jnp.where(jnp.logical_or(first, second), cos, 1.0)
        sa_ref[rows, :] = jnp.where(second, sin, 0.0)
        sb_ref[rows, :] = jnp.where(first, -sin, 0.0)


def _rope_tables(positions):
    inv_freq = ROPE_THETA ** (-jnp.arange(0, ROPE_DIM, 2, dtype=F32) / ROPE_DIM)
    freq = jnp.tile(inv_freq, LANES // ROPE_HALF).reshape(LANES, 1)
    t = positions.size
    tm = ROPE_ROWS * LANES
    table = jax.ShapeDtypeStruct((t, LANES), F32)
    tile = pl.BlockSpec((tm, LANES), lambda i: (i, 0))
    return pl.pallas_call(
        _rope_table_kernel,
        grid=(t // tm,),
        in_specs=[pl.BlockSpec((ROPE_ROWS, LANES), lambda i: (i, 0)),
                  pl.BlockSpec((LANES, 1), lambda i: (0, 0))],
        out_specs=[tile, tile, tile],
        out_shape=[table, table, table],
        compiler_params=_params(("arbitrary",)),
    )(positions.reshape(t // LANES, LANES), freq)


def _decay_selectors():
    dst = jnp.arange(LANES, dtype=jnp.int32)[:, None]
    src = jnp.arange(GATE_ROWS, dtype=jnp.int32)[None, :]
    sels = []
    for s in range(N_SPLIT):
        hit = jnp.logical_and(src < N_FOX_HEADS, dst == N_SPLIT * src + s)
        sels.append(hit.astype(BF16))
    return jnp.stack(sels)


def _split_w_in(w):
    w_f = jnp.pad(w[:, O_F:O_F + N_FOX_HEADS].T, ((0, GATE_ROWS - N_FOX_HEADS), (0, 0))).astype(BF16)
    return w.astype(BF16), w_f


def kernel(x, mem, positions, norm_mix_g, w_in, lam_q1, lam_k1, lam_q2, lam_k2, diff_subln_g, fox_bias, w_out, norm_mem_g, mem_norm_g, w_mq, w_mkv, w_mo, norm_ffn_g, w_ff1, w_ff3, w_ff2, w_router, w_e1, w_e3, w_e2, final_norm_g):
    batch, seq, d = x.shape
    n_mem = mem.shape[1]
    depth = w_in.shape[0]
    t = batch * seq
    cos_t, sa_t, sb_t = _rope_tables(positions)
    sel = _decay_selectors()
    h = x.reshape(t, d)
    mem2 = mem.reshape(batch * n_mem, d)
    row = lambda v: v.reshape(1, -1).astype(F32)

    moe_parts = None
    for l in range(depth):
        lam_init = 0.8 - 0.6 * math.exp(-0.3 * l)
        lam = (jnp.exp(jnp.sum(lam_q1[l] * lam_k1[l])) - jnp.exp(jnp.sum(lam_q2[l] * lam_k2[l]))
               + lam_init).reshape(1).astype(F32)
        w_main, w_f = _split_w_in(w_in[l])
        f_bias = jnp.pad(fox_bias[l], (0, GATE_ROWS - N_FOX_HEADS)).reshape(GATE_ROWS, 1)

        qk, vt, ct = _proj_call(h, row(norm_mix_g[l]), w_main, w_f, f_bias, sel, cos_t, sa_t, sb_t, batch, seq)
        qk3 = qk.reshape(batch, seq, qk.shape[1])
        o_d = _diff_call(lam, qk3, vt, diff_subln_g[l].reshape(DIFF_V_DIM, 1), 1.0 - lam_init, batch, seq)
        o_f = _fox_call(qk3, vt, ct, batch, seq)

        w_o = w_out[l].astype(BF16)
        kv = _norm_matmul_call(mem2, row(mem_norm_g[l]), w_mkv[l].astype(BF16), TM_KV, TN_KV)
        h = _mix_mem_call(h, o_d.reshape(t, D_DIFF_OUT), o_f.reshape(t, D_FOX),
                          w_o[:D_DIFF_OUT], w_o[D_DIFF_OUT:], row(norm_mem_g[l]),
                          w_mq[l].astype(BF16), kv, w_mo[l].astype(BF16), seq, n_mem)

        i = l // 2
        if l % 2 == 0:
            h = _dense_ffn_call(h, row(norm_ffn_g[l]), w_ff1[i].astype(BF16), w_ff3[i].astype(BF16),
                                w_ff2[i].astype(BF16))
        else:
            w_r = jnp.pad(w_router[i], ((0, 0), (0, LANES - N_EXPERTS))).astype(BF16)
            y0, y1, gates = _moe_ffn(h, row(norm_ffn_g[l]), w_r, w_e1, w_e3, w_e2, i)
            if l == depth - 1:
                moe_parts = (y0, y1, gates)
            else:
                h = h + gates[:, 0:1] * y0.astype(F32) + gates[:, 1:2] * y1.astype(F32)

    return _final_norm_call(h, row(final_norm_g), moe_parts).reshape(batch, seq, d)
```

```python
import functools
import math

import jax
import jax.numpy as jnp
from jax import lax
from jax.experimental import pallas as pl
from jax.experimental.pallas import tpu as pltpu

F32 = jnp.float32
BF16 = jnp.bfloat16

HEAD_DIM = 64
N_DIFF_HEADS = 4
DIFF_V_DIM = 128
N_FOX_HEADS = 8
D_DIFF_QK = N_DIFF_HEADS * HEAD_DIM
D_DIFF_OUT = N_DIFF_HEADS * DIFF_V_DIM
D_FOX = N_FOX_HEADS * HEAD_DIM
ATTN_SCALE = 1.0 / math.sqrt(HEAD_DIM)
ROPE_DIM = HEAD_DIM // 4
ROPE_HALF = ROPE_DIM // 2
ROPE_THETA = 500000.0
MEM_HEADS = 4
N_EXPERTS = 8
EPS = 1e-6
SUBLN_EPS = 1e-5
LOG2E = 1.0 / math.log(2.0)

LANES = 128
VMEM_LIMIT = 48 * 1024 * 1024

TM_PROJ = 512
TN_PROJ = 512
N_QK_BLK = 4
TQ = 512
TK = 512
TM_MIX = 512
TM_KV = 512
TN_KV = 1024
TM_FFN = 1024
TF_DENSE = 256
TF_MOE = 512
MOE_SPLIT = 2
V_ROWS = LANES + 8
N_SPLIT = 3
GATE_ROWS = 16


def _params(sem):
    return pltpu.CompilerParams(dimension_semantics=sem, vmem_limit_bytes=VMEM_LIMIT)


def _rms_scale(x, eps):
    return lax.rsqrt(jnp.mean(x * x, axis=-1, keepdims=True) + eps)


def _split3(v):
    hi = v.astype(BF16)
    r1 = v - hi.astype(F32)
    mid = r1.astype(BF16)
    lo = (r1 - mid.astype(F32)).astype(BF16)
    return hi, mid, lo


O_VD = 4 * D_DIFF_QK
O_QF = O_VD + D_DIFF_OUT
O_KF = O_QF + D_FOX
O_VF = O_KF + D_FOX
O_F = O_VF + D_FOX
Q_SCALE = ATTN_SCALE * LOG2E
QK_COLS = 2 * TN_PROJ + 2 * N_FOX_HEADS * LANES
_PROJ_BLOCKS = (
    (0, "rope", 0, Q_SCALE),
    (2 * D_DIFF_QK, "rope", TN_PROJ, 1.0),
    (O_QF, "fox_q", 2 * TN_PROJ, Q_SCALE),
    (O_KF, "fox_k", 2 * TN_PROJ + N_FOX_HEADS * LANES, 1.0),
    (O_VD, "vt", 0, 1.0),
    (O_VF, "vt", N_DIFF_HEADS, 1.0),
)


def _proj_kernel(x_ref, g_ref, w_ref, wf_ref, fb_ref, sel_ref, cos_ref, sa_ref, sb_ref,
                 qk_ref, vt_ref, c_ref, carry_ref, *, tiles_per_seq):
    i = pl.program_id(0)
    tm = x_ref.shape[0]
    x = x_ref[...]
    xnb = ((x * _rms_scale(x, EPS)) * g_ref[...]).astype(BF16)

    logit = lax.dot_general(wf_ref[...], xnb, (((1,), (1,)), ((), ())),
                            preferred_element_type=F32) + fb_ref[...]
    logf = jnp.minimum(logit, 0.0) - jnp.log1p(jnp.exp(-jnp.abs(logit)))

    @pl.when(i % tiles_per_seq == 0)
    def _():
        carry_ref[...] = jnp.zeros_like(carry_ref)

    r = lax.broadcasted_iota(jnp.int32, (tm, tm), 0)
    cidx = lax.broadcasted_iota(jnp.int32, (tm, tm), 1)
    tri = (r <= cidx).astype(BF16)
    hi, mid, lo = _split3(logf)
    cs = (jnp.dot(hi, tri, preferred_element_type=F32)
          + jnp.dot(mid, tri, preferred_element_type=F32)
          + jnp.dot(lo, tri, preferred_element_type=F32)) + carry_ref[:, 0:1]
    carry_ref[...] = jnp.broadcast_to(cs[:, tm - 1:tm], carry_ref.shape)
    c2 = cs * LOG2E
    c_ref[0] = c2
    hi, mid, lo = _split3(-c2)
    kx = (jnp.dot(sel_ref[0], hi, preferred_element_type=F32)
          + jnp.dot(sel_ref[1], mid, preferred_element_type=F32)
          + jnp.dot(sel_ref[2], lo, preferred_element_type=F32)).T
    lane = lax.broadcasted_iota(jnp.int32, (tm, LANES), 1)
    low_half = lane < HEAD_DIM
    extra = [jnp.logical_and(lane >= HEAD_DIM, lane < HEAD_DIM + N_SPLIT), lane < N_SPLIT]

    cos = cos_ref[...]
    sa = sa_ref[...]
    sb = sb_ref[...]
    ones_row = (lax.broadcasted_iota(jnp.int32, (V_ROWS - LANES, TK), 0) == 0).astype(vt_ref.dtype)
    for src, kind, dst, scale in _PROJ_BLOCKS:
        acc = jnp.dot(xnb, w_ref[:, src:src + TN_PROJ], preferred_element_type=F32)
        if scale != 1.0:
            acc = acc * scale
        if kind == "rope":
            for s in range(TN_PROJ // LANES):
                t = acc[:, s * LANES:(s + 1) * LANES]
                rot = (t * cos + pltpu.roll(t, ROPE_HALF, 1) * sa
                       + pltpu.roll(t, LANES - ROPE_HALF, 1) * sb)
                qk_ref[:, dst + s * LANES:dst + (s + 1) * LANES] = rot.astype(qk_ref.dtype)
        elif kind in ("fox_q", "fox_k"):
            for pr in range(TN_PROJ // LANES):
                pair = acc[:, pr * LANES:(pr + 1) * LANES]
                for par in range(2):
                    head = 2 * pr + par
                    if kind == "fox_q":
                        other = extra[par].astype(F32)
                    else:
                        first = HEAD_DIM if par == 0 else 0
                        moved = pltpu.roll(kx, (first - N_SPLIT * head) % LANES, 1)
                        other = jnp.where(extra[par], moved, 0.0)
                    keep = low_half if par == 0 else jnp.logical_not(low_half)
                    qk_ref[:, dst + head * LANES:dst + (head + 1) * LANES] = (
                        jnp.where(keep, pair, other).astype(qk_ref.dtype))
        else:
            at = acc.T
            for hd in range(TN_PROJ // LANES):
                vt_ref[0, dst + hd, 0, 0:LANES, :] = at[hd * LANES:(hd + 1) * LANES, :].astype(vt_ref.dtype)
                vt_ref[0, dst + hd, 0, LANES:V_ROWS, :] = ones_row


def _proj_call(h, g, w_main, w_f, f_bias, sel, cos_t, sa_t, sb_t, batch, seq):
    t, d = h.shape
    tm = TM_PROJ
    assert tm == TK
    tiles_per_seq = seq // tm
    n_col = w_main.shape[1]
    n_vh = (len(_PROJ_BLOCKS) - N_QK_BLK) * (TN_PROJ // LANES)
    kern = functools.partial(_proj_kernel, tiles_per_seq=tiles_per_seq)
    const2 = lambda i: (0, 0)
    return pl.pallas_call(
        kern,
        grid=(t // tm,),
        in_specs=[
            pl.BlockSpec((tm, d), lambda i: (i, 0)),
            pl.BlockSpec((1, d), const2),
            pl.BlockSpec((d, n_col), const2),
            pl.BlockSpec((GATE_ROWS, d), const2),
            pl.BlockSpec((GATE_ROWS, 1), const2),
            pl.BlockSpec((N_SPLIT, LANES, GATE_ROWS), lambda i: (0, 0, 0)),
            pl.BlockSpec((tm, LANES), lambda i: (i, 0)),
            pl.BlockSpec((tm, LANES), lambda i: (i, 0)),
            pl.BlockSpec((tm, LANES), lambda i: (i, 0)),
        ],
        out_specs=[
            pl.BlockSpec((tm, QK_COLS), lambda i: (i, 0)),
            pl.BlockSpec((1, n_vh, 1, V_ROWS, TK), lambda i: (i // tiles_per_seq, 0, i % tiles_per_seq, 0, 0)),
            pl.BlockSpec((1, GATE_ROWS, tm), lambda i: (i // tiles_per_seq, 0, i % tiles_per_seq)),
        ],
        out_shape=[
            jax.ShapeDtypeStruct((t, QK_COLS), BF16),
            jax.ShapeDtypeStruct((batch, n_vh, seq // TK, V_ROWS, TK), BF16),
            jax.ShapeDtypeStruct((batch, GATE_ROWS, seq), F32),
        ],
        scratch_shapes=[pltpu.VMEM((GATE_ROWS, LANES), F32)],
        compiler_params=_params(("arbitrary",)),
    )(h, g, w_main, w_f, f_bias, sel, cos_t, sa_t, sb_t)


def _head_mask(q, half):
    lane = lax.broadcasted_iota(jnp.int32, q.shape, 1)
    return jnp.where(lane // HEAD_DIM == half, q, jnp.zeros_like(q))


def _causal_mask():
    kpos = lax.broadcasted_iota(jnp.int32, (TK, TQ), 0)
    qpos = lax.broadcasted_iota(jnp.int32, (TK, TQ), 1)
    return kpos <= qpos


def _scores_t(k, q):
    return lax.dot_general(k, q, (((1,), (1,)), ((), ())), preferred_element_type=F32)


def _online_update(u, vt, shift, acc_ref, m_ref):
    mu = jnp.max(u, axis=0, keepdims=True) + shift
    m_old = m_ref[...]
    m_new = jnp.maximum(m_old, mu)
    p = jnp.exp2(u - (m_new - shift)).astype(BF16)
    pv = jnp.dot(vt, p, preferred_element_type=F32)
    acc_ref[...] = jnp.exp2(m_old - m_new) * acc_ref[...] + pv
    m_ref[...] = m_new


def _normalised(acc_ref):
    acc = acc_ref[...]
    return acc[:LANES] * (1.0 / acc[LANES:LANES + 1])


N_TILES = 4
UNROLL = 4


def _attn_scratch():
    return ([pltpu.VMEM((TK, TQ), F32)] * (3 * N_TILES) + [pltpu.VMEM((V_ROWS, TQ), F32)] * N_TILES
            + [pltpu.VMEM((1, TQ), F32)] * N_TILES)


def _attend(qi, qk_fn, qk_next_fn, vt_fn, shifts, scratch, finish):
    assert TQ == TK
    s_a, s_b, s_c = (scratch[n * N_TILES:(n + 1) * N_TILES] for n in range(3))
    acc = scratch[3 * N_TILES:4 * N_TILES]
    m = scratch[4 * N_TILES:]
    for t in range(N_TILES):
        m[t][...] = jnp.full(m[t].shape, -jnp.inf, F32)
        acc[t][...] = jnp.zeros(acc[t].shape, F32)

    def stage(j, cur, nxt, masked):
        for t in range(N_TILES):
            nxt[t][...] = qk_next_fn(t) if masked else qk_fn(j + 1, t)
            u = cur[t][...]
            if masked:
                u = jnp.where(_causal_mask(), u, -jnp.inf)
            _online_update(u, vt_fn(j, t), shifts[t], acc[t], m[t])

    @pl.when(qi == 0)
    def _():
        for t in range(N_TILES):
            s_a[t][...] = qk_fn(0, t)
        stage(0, s_a, s_c, True)
        finish(acc)

    @pl.when(qi > 0)
    def _():
        stage(0, s_c, s_a, False)
        bufs = (s_a, s_b)

        def body(i, carry):
            for n in range(UNROLL):
                stage(UNROLL * i + 1 + n, bufs[n % 2], bufs[(n + 1) % 2], False)
            return carry

        lax.fori_loop(0, (qi - 1) // UNROLL, body, 0)

        for rem in range(UNROLL):
            @pl.when((qi - 1) % UNROLL == rem)
            def _(rem=rem):
                for n in range(rem):
                    stage(qi - rem + n, bufs[n % 2], bufs[(n + 1) % 2], False)
                stage(qi, bufs[rem % 2], s_c, True)
                finish(acc)


def _fox_kernel(q_ref, qn_ref, k_ref, vt_ref, ct_ref, o_ref, *scratch):
    grp = pl.program_id(1)
    qi = pl.program_id(2)
    cqs = [ct_ref[0, pl.ds(N_TILES * grp + t, 1), :] for t in range(N_TILES)]

    def scores(j, t, qr):
        start = pl.multiple_of(j * TK, TK)
        return _scores_t(k_ref[0, pl.ds(start, TK), t * LANES:(t + 1) * LANES],
                         qr[0, :, t * LANES:(t + 1) * LANES])

    def vt_fn(j, t):
        return vt_ref[0, t // 2, j]

    def finish(acc):
        for pp in range(2):
            ot = jnp.concatenate([_normalised(acc[2 * pp])[:HEAD_DIM],
                                  _normalised(acc[2 * pp + 1])[HEAD_DIM:]], axis=0)
            o_ref[0, :, pp * LANES:(pp + 1) * LANES] = ot.T.astype(o_ref.dtype)

    _attend(qi, lambda j, t: scores(j, t, q_ref), lambda t: scores(0, t, qn_ref), vt_fn, cqs, scratch, finish)


def _fox_call(qk, vt, ct, batch, seq):
    nkb = seq // TK
    nq = seq // TQ
    w4 = N_TILES * LANES
    return pl.pallas_call(
        _fox_kernel,
        grid=(batch, N_FOX_HEADS // N_TILES, nq),
        in_specs=[
            pl.BlockSpec((1, TQ, w4), lambda b, g, qi: (b, qi, 2 + g)),
            pl.BlockSpec((1, TQ, w4), lambda b, g, qi: (b, jnp.minimum(qi + 1, nq - 1), 2 + g)),
            pl.BlockSpec((1, seq, w4), lambda b, g, qi: (b, 0, 4 + g)),
            pl.BlockSpec((1, 2, nkb, V_ROWS, TK), lambda b, g, qi: (b, N_DIFF_HEADS // 2 + g, 0, 0, 0)),
            pl.BlockSpec((1, GATE_ROWS, TQ), lambda b, g, qi: (b, 0, qi)),
        ],
        out_specs=pl.BlockSpec((1, TQ, 2 * LANES), lambda b, g, qi: (b, qi, g)),
        out_shape=jax.ShapeDtypeStruct((batch, seq, D_FOX), BF16),
        scratch_shapes=_attn_scratch(),
        compiler_params=_params(("arbitrary", "arbitrary", "arbitrary")),
    )(qk, qk, qk, vt, ct)


def _diff_kernel(lam_ref, q1_ref, q2_ref, q1n_ref, q2n_ref, k1_ref, k2_ref, vt_ref, g_ref, o_ref, *scratch,
                 out_scale):
    qi = pl.program_id(2)
    k_refs = [k1_ref, k2_ref]
    qms = [_head_mask(r[0], hh) for hh in range(2) for r in (q1_ref, q2_ref)]
    qms_next = [_head_mask(r[0], hh) for hh in range(2) for r in (q1n_ref, q2n_ref)]

    def scores(j, t, qm):
        start = pl.multiple_of(j * TK, TK)
        return _scores_t(k_refs[t % 2][0, pl.ds(start, TK), :], qm[t])

    def vt_fn(j, t):
        return vt_ref[0, t // 2, j]

    def finish(acc):
        for hh in range(2):
            ot = _normalised(acc[2 * hh]) - lam_ref[0] * _normalised(acc[2 * hh + 1])
            y = ot * lax.rsqrt(jnp.mean(ot * ot, axis=0, keepdims=True) + SUBLN_EPS)
            y = (y * g_ref[...]) * out_scale
            o_ref[0, :, hh * LANES:(hh + 1) * LANES] = y.T.astype(o_ref.dtype)

    _attend(qi, lambda j, t: scores(j, t, qms), lambda t: scores(0, t, qms_next), vt_fn,
            [0.0] * N_TILES, scratch, finish)


def _diff_call(lam, qk, vt, g_col, out_scale, batch, seq):
    nkb = seq // TK
    nq = seq // TQ
    n_pairs = N_DIFF_HEADS // 2
    kern = functools.partial(_diff_kernel, out_scale=out_scale)
    grid_spec = pltpu.PrefetchScalarGridSpec(
        num_scalar_prefetch=1,
        grid=(batch, n_pairs, nq),
        in_specs=[
            pl.BlockSpec((1, TQ, LANES), lambda b, p, qi, lam: (b, qi, p)),
            pl.BlockSpec((1, TQ, LANES), lambda b, p, qi, lam: (b, qi, 2 + p)),
            pl.BlockSpec((1, TQ, LANES), lambda b, p, qi, lam: (b, jnp.minimum(qi + 1, nq - 1), p)),
            pl.BlockSpec((1, TQ, LANES), lambda b, p, qi, lam: (b, jnp.minimum(qi + 1, nq - 1), 2 + p)),
            pl.BlockSpec((1, seq, LANES), lambda b, p, qi, lam: (b, 0, 4 + p)),
            pl.BlockSpec((1, seq, LANES), lambda b, p, qi, lam: (b, 0, 6 + p)),
            pl.BlockSpec((1, 2, nkb, V_ROWS, TK), lambda b, p, qi, lam: (b, p, 0, 0, 0)),
            pl.BlockSpec((DIFF_V_DIM, 1), lambda b, p, qi, lam: (0, 0)),
        ],
        out_specs=pl.BlockSpec((1, TQ, 2 * LANES), lambda b, p, qi, lam: (b, qi, p)),
        scratch_shapes=_attn_scratch(),
    )
    return pl.pallas_call(
        kern,
        grid_spec=grid_spec,
        out_shape=jax.ShapeDtypeStruct((batch, seq, D_DIFF_OUT), BF16),
        compiler_params=_params(("arbitrary", "arbitrary", "arbitrary")),
    )(lam, qk, qk, qk, qk, qk, qk, vt, g_col)


def _norm_matmul_kernel(x_ref, g_ref, w_ref, o_ref):
    x = x_ref[...]
    xn = ((x * _rms_scale(x, EPS)) * g_ref[...]).astype(BF16)
    o_ref[...] = jnp.dot(xn, w_ref[...], preferred_element_type=F32).astype(o_ref.dtype)


def _norm_matmul_call(x, g, w, tm, tn):
    t, d = x.shape
    n = w.shape[1]
    return pl.pallas_call(
        _norm_matmul_kernel,
        grid=(t // tm, n // tn),
        in_specs=[pl.BlockSpec((tm, d), lambda i, j: (i, 0)),
                  pl.BlockSpec((1, d), lambda i, j: (0, 0)),
                  pl.BlockSpec((d, tn), lambda i, j: (0, j))],
        out_specs=pl.BlockSpec((tm, tn), lambda i, j: (i, j)),
        out_shape=jax.ShapeDtypeStruct((t, n), BF16),
        compiler_params=_params(("arbitrary", "arbitrary")),
    )(x, g, w)


def _mix_mem_kernel(h_ref, od_ref, of_ref, wod_ref, wof_ref, g_ref, wq_ref, k_ref, v_ref, wo_ref,
                    o_ref, *, mem_scale):
    h1 = (h_ref[...]
          + jnp.dot(od_ref[...], wod_ref[...], preferred_element_type=F32)
          + jnp.dot(of_ref[...], wof_ref[...], preferred_element_type=F32))
    xn = ((h1 * _rms_scale(h1, EPS)) * g_ref[...]).astype(BF16)
    q = jnp.dot(xn, wq_ref[...], preferred_element_type=F32).astype(BF16)
    d = q.shape[1]
    hd = d // MEM_HEADS
    outs = []
    for hh in range(MEM_HEADS):
        qh = q[:, hh * hd:(hh + 1) * hd]
        kh = k_ref[:, hh * hd:(hh + 1) * hd]
        vh = v_ref[:, hh * hd:(hh + 1) * hd]
        s = lax.dot_general(qh, kh, (((1,), (1,)), ((), ())),
                            preferred_element_type=F32) * mem_scale
        m = jnp.max(s, axis=-1, keepdims=True)
        e = jnp.exp(s - m)
        p = e * (1.0 / jnp.sum(e, axis=-1, keepdims=True))
        outs.append(jnp.dot(p.astype(BF16), vh, preferred_element_type=F32).astype(BF16))
    o = jnp.concatenate(outs, axis=1)
    o_ref[...] = h1 + jnp.dot(o, wo_ref[...], preferred_element_type=F32)


def _mix_mem_call(h, o_d, o_f, w_od, w_of, g, w_q, kv, w_o, seq, n_mem):
    t, d = h.shape
    tm = TM_MIX
    tiles_per_seq = seq // tm
    kern = functools.partial(_mix_mem_kernel, mem_scale=1.0 / math.sqrt(d // MEM_HEADS))
    const = lambda i: (0, 0)
    return pl.pallas_call(
        kern,
        grid=(t // tm,),
        in_specs=[
            pl.BlockSpec((tm, d), lambda i: (i, 0)),
            pl.BlockSpec((tm, D_DIFF_OUT), lambda i: (i, 0)),
            pl.BlockSpec((tm, D_FOX), lambda i: (i, 0)),
            pl.BlockSpec((D_DIFF_OUT, d), const),
            pl.BlockSpec((D_FOX, d), const),
            pl.BlockSpec((1, d), const),
            pl.BlockSpec((d, d), const),
            pl.BlockSpec((n_mem, d), lambda i: (i // tiles_per_seq, 0)),
            pl.BlockSpec((n_mem, d), lambda i: (i // tiles_per_seq, 1)),
            pl.BlockSpec((d, d), const),
        ],
        out_specs=pl.BlockSpec((tm, d), lambda i: (i, 0)),
        out_shape=jax.ShapeDtypeStruct((t, d), F32),
        compiler_params=_params(("arbitrary",)),
    )(h, o_d, o_f, w_od, w_of, g, w_q, kv, kv, w_o)


def _swiglu_chunk(x, w1, w3, w2):
    a = jnp.dot(x, w1, preferred_element_type=F32)
    b = jnp.dot(x, w3, preferred_element_type=F32)
    mid = (a * (1.0 / (1.0 + jnp.exp(-a)))) * b
    return jnp.dot(mid.astype(BF16), w2, preferred_element_type=F32)


def _dense_ffn_kernel(h_ref, g_ref, w1_ref, w3_ref, w2_ref, o_ref, xn_ref):
    f = pl.program_id(1)

    @pl.when(f == 0)
    def _():
        x = h_ref[...]
        xn_ref[...] = ((x * _rms_scale(x, EPS)) * g_ref[...]).astype(BF16)
        o_ref[...] = x

    o_ref[...] += _swiglu_chunk(xn_ref[...], w1_ref[...], w3_ref[...], w2_ref[...])


def _dense_ffn_call(h, g, w1, w3, w2):
    t, d = h.shape
    tm, tf = TM_FFN, TF_DENSE
    dff = w1.shape[1]
    return pl.pallas_call(
        _dense_ffn_kernel,
        grid=(t // tm, dff // tf),
        in_specs=[pl.BlockSpec((tm, d), lambda i, f: (i, 0)),
                  pl.BlockSpec((1, d), lambda i, f: (0, 0)),
                  pl.BlockSpec((d, tf), lambda i, f: (0, f)),
                  pl.BlockSpec((d, tf), lambda i, f: (0, f)),
                  pl.BlockSpec((tf, d), lambda i, f: (f, 0))],
        out_specs=pl.BlockSpec((tm, d), lambda i, f: (i, 0)),
        out_shape=jax.ShapeDtypeStruct((t, d), F32),
        scratch_shapes=[pltpu.VMEM((tm, d), BF16)],
        compiler_params=_params(("arbitrary", "arbitrary")),
    )(h, g, w1, w3, w2)


def _router_kernel(h_ref, g_ref, wr_ref, hn_ref, idx_ref, gate_ref):
    x = h_ref[...]
    xn = ((x * _rms_scale(x, EPS)) * g_ref[...]).astype(BF16)
    hn_ref[...] = xn
    logits = jnp.dot(xn, wr_ref[...], preferred_element_type=F32)
    lane = lax.broadcasted_iota(jnp.int32, logits.shape, 1)
    logits = jnp.where(lane < N_EXPERTS, logits, -jnp.inf)
    v1 = jnp.max(logits, axis=-1, keepdims=True)
    i1 = jnp.min(jnp.where(logits == v1, lane, LANES), axis=-1, keepdims=True)
    rest = jnp.where(lane == i1, -jnp.inf, logits)
    v2 = jnp.max(rest, axis=-1, keepdims=True)
    i2 = jnp.min(jnp.where(rest == v2, lane, LANES), axis=-1, keepdims=True)
    e = jnp.exp(v2 - v1)
    inv = 1.0 / (1.0 + e)
    idx_ref[...] = jnp.where(lane == 0, i1, jnp.where(lane == 1, i2, 0))
    gate_ref[...] = jnp.where(lane == 0, inv, jnp.where(lane == 1, e * inv, 0.0))


def _router_call(h, g, w_r):
    t, d = h.shape
    tm = TM_FFN
    return pl.pallas_call(
        _router_kernel,
        grid=(t // tm,),
        in_specs=[pl.BlockSpec((tm, d), lambda i: (i, 0)),
                  pl.BlockSpec((1, d), lambda i: (0, 0)),
                  pl.BlockSpec((d, LANES), lambda i: (0, 0))],
        out_specs=[pl.BlockSpec((tm, d), lambda i: (i, 0)),
                   pl.BlockSpec((tm, LANES), lambda i: (i, 0)),
                   pl.BlockSpec((tm, LANES), lambda i: (i, 0))],
        out_shape=[jax.ShapeDtypeStruct((t, d), BF16),
                   jax.ShapeDtypeStruct((t, LANES), jnp.int32),
                   jax.ShapeDtypeStruct((t, LANES), F32)],
        compiler_params=_params(("arbitrary",)),
    )(h, g, w_r)


def _expert_ffn_kernel(te_ref, nu_ref, x_ref, w1_ref, w3_ref, w2_ref, *rest, n_chunks, tile0):
    o_ref, acc_ref = rest[-2:]
    i = pl.program_id(0)
    f = pl.program_id(1)

    @pl.when(tile0 + i < nu_ref[0])
    def _():
        @pl.when(f == 0)
        def _():
            acc_ref[...] = jnp.zeros_like(acc_ref)

        acc_ref[...] += _swiglu_chunk(x_ref[...], w1_ref[0, 0].astype(BF16), w3_ref[0, 0].astype(BF16),
                                      w2_ref[0, 0].astype(BF16))

        @pl.when(f == n_chunks - 1)
        def _():
            o_ref[...] = acc_ref[...].astype(o_ref.dtype)

    @pl.when(jnp.logical_and(tile0 >= nu_ref[0], jnp.logical_and(i == 0, f == 0)))
    def _():
        o_ref[...] = jnp.zeros_like(o_ref)


def _expert_ffn_call(tile_expert, n_used, x_rows, w1, w3, w2, layer, tile0, p_rows, y_prev=None):
    rows, d = x_rows.shape
    tm, tf = TM_FFN, TF_MOE
    n_local = rows // tm
    dff = w1.shape[3]
    n_chunks = dff // tf
    kern = functools.partial(_expert_ffn_kernel, n_chunks=n_chunks, tile0=tile0)

    def local(i, nu):
        return jnp.minimum(i, jnp.clip(nu[0] - 1 - tile0, 0, n_local - 1))

    def chunk(i, f, nu):
        return jnp.where(tile0 + i < nu[0], f, n_chunks - 1)

    def expert(i, te):
        return te[tile0 + i]

    in_specs = [
        pl.BlockSpec((tm, d), lambda i, f, te, nu: (local(i, nu), 0)),
        pl.BlockSpec((1, 1, d, tf), lambda i, f, te, nu: (layer, expert(i, te), 0, chunk(i, f, nu))),
        pl.BlockSpec((1, 1, d, tf), lambda i, f, te, nu: (layer, expert(i, te), 0, chunk(i, f, nu))),
        pl.BlockSpec((1, 1, tf, d), lambda i, f, te, nu: (layer, expert(i, te), chunk(i, f, nu), 0)),
    ]
    args = [tile_expert, n_used, x_rows, w1, w3, w2]
    aliases = {}
    if y_prev is not None:
        in_specs.append(pl.BlockSpec(memory_space=pl.ANY))
        aliases = {len(args): 0}
        args.append(y_prev)
    grid_spec = pltpu.PrefetchScalarGridSpec(
        num_scalar_prefetch=2,
        grid=(n_local, n_chunks),
        in_specs=in_specs,
        out_specs=pl.BlockSpec((tm, d), lambda i, f, te, nu: (tile0 + local(i, nu), 0)),
        scratch_shapes=[pltpu.VMEM((tm, d), F32)],
    )
    return pl.pallas_call(
        kern,
        grid_spec=grid_spec,
        out_shape=jax.ShapeDtypeStruct((p_rows, d), BF16),
        input_output_aliases=aliases,
        compiler_params=_params(("arbitrary", "arbitrary")),
    )(*args)


def _route(idx, tm):
    t = idx.shape[0]
    n_assign = 2 * t
    p_rows = n_assign + N_EXPERTS * tm
    e_flat = idx.reshape(-1)
    onehot = (e_flat[None, :] == jnp.arange(N_EXPERTS, dtype=jnp.int32)[:, None]).astype(jnp.int32)
    running = jnp.cumsum(onehot, axis=1)
    counts = running[:, -1]
    padded = ((counts + tm - 1) // tm) * tm
    pend = jnp.cumsum(padded)
    pstart = pend - padded
    gstart = jnp.cumsum(counts) - counts
    dest = jnp.sum(onehot * (running - 1 + pstart[:, None]), axis=0)
    n_tiles = p_rows // tm
    n_used = (pend[-1] // tm).astype(jnp.int32)
    tile_start = jnp.arange(n_tiles, dtype=jnp.int32) * tm
    tile_expert = jnp.sum((tile_start[:, None] >= pend[None, :]).astype(jnp.int32), axis=1)
    last_expert = jnp.sum((((n_used - 1) * tm) >= pend).astype(jnp.int32))
    tile_expert = jnp.where(tile_start < pend[-1], tile_expert, last_expert).astype(jnp.int32)
    order = jnp.sort(e_flat * n_assign + jnp.arange(n_assign, dtype=jnp.int32)) % n_assign
    within = ((tile_start - pstart[tile_expert])[:, None]
              + jnp.arange(tm, dtype=jnp.int32)[None, :])
    src = jnp.clip(gstart[tile_expert][:, None] + within, 0, n_assign - 1)
    row_token = jnp.where(within < counts[tile_expert][:, None], order[src] // 2, 0)
    return row_token.reshape(p_rows), dest.reshape(t, 2), tile_expert, n_used.reshape(1)


def _moe_ffn(h, g, w_r, w1, w3, w2, layer):
    hn, idx_l, gate_l = _router_call(h, g, w_r)
    row_token, slot, tile_expert, n_used = _route(idx_l[:, :2], TM_FFN)
    p_rows = row_token.shape[0]
    assert p_rows % (MOE_SPLIT * TM_FFN) == 0
    rows = p_rows // MOE_SPLIT
    y = None
    for c in range(MOE_SPLIT):
        x_rows = jnp.take(hn, row_token[c * rows:(c + 1) * rows], axis=0)
        y = _expert_ffn_call(tile_expert, n_used, x_rows, w1, w3, w2, layer, c * (rows // TM_FFN), p_rows, y)
    return jnp.take(y, slot[:, 0], axis=0), jnp.take(y, slot[:, 1], axis=0), gate_l


def _final_norm_kernel(x_ref, g_ref, o_ref):
    x = x_ref[...]
    o_ref[...] = (x * _rms_scale(x, EPS)) * g_ref[...]


def _combine_norm_kernel(h_ref, y0_ref, y1_ref, gate_ref, g_ref, o_ref):
    x = (h_ref[...] + gate_ref[:, 0:1] * y0_ref[...].astype(F32)
         + gate_ref[:, 1:2] * y1_ref[...].astype(F32))
    o_ref[...] = (x * _rms_scale(x, EPS)) * g_ref[...]


def _final_norm_call(h, g, moe_parts=None):
    t, d = h.shape
    tm = TM_FFN
    tile = pl.BlockSpec((tm, d), lambda i: (i, 0))
    gain = pl.BlockSpec((1, d), lambda i: (0, 0))
    if moe_parts is None:
        kern, in_specs, args = _final_norm_kernel, [tile, gain], (h, g)
    else:
        y0, y1, gates = moe_parts
        kern = _combine_norm_kernel
        in_specs = [tile, tile, tile, pl.BlockSpec((tm, LANES), lambda i: (i, 0)), gain]
        args = (h, y0, y1, gates, g)
    return pl.pallas_call(
        kern,
        grid=(t // tm,),
        in_specs=in_specs,
        out_specs=tile,
        out_shape=jax.ShapeDtypeStruct((t, d), F32),
        compiler_params=_params(("arbitrary",)),
    )(*args)


ROPE_ROWS = 8


def _rope_table_kernel(pos_ref, freq_ref, cos_ref, sa_ref, sb_ref):
    dim = lax.broadcasted_iota(jnp.int32, (LANES, LANES), 1) % HEAD_DIM
    first = dim < ROPE_HALF
    second = jnp.logical_and(dim >= ROPE_HALF, dim < ROPE_DIM)
    freq = freq_ref[...]
    for r in range(ROPE_ROWS):
        ang = (freq * pos_ref[r:r + 1, :].astype(F32)).T
        cos, sin = jnp.cos(ang), jnp.sin(ang)
        rows = slice(r * LANES, (r + 1) * LANES)
        cos_ref[rows, :] = jnp.where(jnp.logical_or(first, second), cos, 1.0)
        sa_ref[rows, :] = jnp.where(second, sin, 0.0)
        sb_ref[rows, :] = jnp.where(first, -sin, 0.0)


def _rope_tables(positions):
    inv_freq = ROPE_THETA ** (-jnp.arange(0, ROPE_DIM, 2, dtype=F32) / ROPE_DIM)
    freq = jnp.tile(inv_freq, LANES // ROPE_HALF).reshape(LANES, 1)
    t = positions.size
    tm = ROPE_ROWS * LANES
    table = jax.ShapeDtypeStruct((t, LANES), F32)
    tile = pl.BlockSpec((tm, LANES), lambda i: (i, 0))
    return pl.pallas_call(
        _rope_table_kernel,
        grid=(t // tm,),
        in_specs=[pl.BlockSpec((ROPE_ROWS, LANES), lambda i: (i, 0)),
                  pl.BlockSpec((LANES, 1), lambda i: (0, 0))],
        out_specs=[tile, tile, tile],
        out_shape=[table, table, table],
        compiler_params=_params(("arbitrary",)),
    )(positions.reshape(t // LANES, LANES), freq)


def _decay_selectors():
    dst = jnp.arange(LANES, dtype=jnp.int32)[:, None]
    src = jnp.arange(GATE_ROWS, dtype=jnp.int32)[None, :]
    sels = []
    for s in range(N_SPLIT):
        hit = jnp.logical_and(src < N_FOX_HEADS, dst == N_SPLIT * src + s)
        sels.append(hit.astype(BF16))
    return jnp.stack(sels)


def _split_w_in(w):
    w_f = jnp.pad(w[:, O_F:O_F + N_FOX_HEADS].T, ((0, GATE_ROWS - N_FOX_HEADS), (0, 0))).astype(BF16)
    return w.astype(BF16), w_f


def kernel(x, mem, positions, norm_mix_g, w_in, lam_q1, lam_k1, lam_q2, lam_k2, diff_subln_g, fox_bias, w_out, norm_mem_g, mem_norm_g, w_mq, w_mkv, w_mo, norm_ffn_g, w_ff1, w_ff3, w_ff2, w_router, w_e1, w_e3, w_e2, final_norm_g):
    batch, seq, d = x.shape
    n_mem = mem.shape[1]
    depth = w_in.shape[0]
    t = batch * seq
    cos_t, sa_t, sb_t = _rope_tables(positions)
    sel = _decay_selectors()
    h = x.reshape(t, d)
    mem2 = mem.reshape(batch * n_mem, d)
    row = lambda v: v.reshape(1, -1).astype(F32)

    moe_parts = None
    for l in range(depth):
        lam_init = 0.8 - 0.6 * math.exp(-0.3 * l)
        lam = (jnp.exp(jnp.sum(lam_q1[l] * lam_k1[l])) - jnp.exp(jnp.sum(lam_q2[l] * lam_k2[l]))
               + lam_init).reshape(1).astype(F32)
        w_main, w_f = _split_w_in(w_in[l])
        f_bias = jnp.pad(fox_bias[l], (0, GATE_ROWS - N_FOX_HEADS)).reshape(GATE_ROWS, 1)

        qk, vt, ct = _proj_call(h, row(norm_mix_g[l]), w_main, w_f, f_bias, sel, cos_t, sa_t, sb_t, batch, seq)
        qk3 = qk.reshape(batch, seq, qk.shape[1])
        o_d = _diff_call(lam, qk3, vt, diff_subln_g[l].reshape(DIFF_V_DIM, 1), 1.0 - lam_init, batch, seq)
        o_f = _fox_call(qk3, vt, ct, batch, seq)

        w_o = w_out[l].astype(BF16)
        kv = _norm_matmul_call(mem2, row(mem_norm_g[l]), w_mkv[l].astype(BF16), TM_KV, TN_KV)
        h = _mix_mem_call(h, o_d.reshape(t, D_DIFF_OUT), o_f.reshape(t, D_FOX),
                          w_o[:D_DIFF_OUT], w_o[D_DIFF_OUT:], row(norm_mem_g[l]),
                          w_mq[l].astype(BF16), kv, w_mo[l].astype(BF16), seq, n_mem)

        i = l // 2
        if l % 2 == 0:
            h = _dense_ffn_call(h, row(norm_ffn_g[l]), w_ff1[i].astype(BF16), w_ff3[i].astype(BF16),
                                w_ff2[i].astype(BF16))
        else:
            w_r = jnp.pad(w_router[i], ((0, 0), (0, LANES - N_EXPERTS))).astype(BF16)
            y0, y1, gates = _moe_ffn(h, row(norm_ffn_g[l]), w_r, w_e1, w_e3, w_e2, i)
            if l == depth - 1:
                moe_parts = (y0, y1, gates)
            else:
                h = h + gates[:, 0:1] * y0.astype(F32) + gates[:, 1:2] * y1.astype(F32)

    return _final_norm_call(h, row(final_norm_g), moe_parts).reshape(batch, seq, d)
```

```python
import functools
import math

import jax
import jax.numpy as jnp
from jax import lax
from jax.experimental import pallas as pl
from jax.experimental.pallas import tpu as pltpu

F32 = jnp.float32
BF16 = jnp.bfloat16

HEAD_DIM = 64
N_DIFF_HEADS = 4
DIFF_V_DIM = 128
N_FOX_HEADS = 8
D_DIFF_QK = N_DIFF_HEADS * HEAD_DIM
D_DIFF_OUT = N_DIFF_HEADS * DIFF_V_DIM
D_FOX = N_FOX_HEADS * HEAD_DIM
ATTN_SCALE = 1.0 / math.sqrt(HEAD_DIM)
ROPE_DIM = HEAD_DIM // 4
ROPE_HALF = ROPE_DIM // 2
ROPE_THETA = 500000.0
MEM_HEADS = 4
N_EXPERTS = 8
EPS = 1e-6
SUBLN_EPS = 1e-5
LOG2E = 1.0 / math.log(2.0)

LANES = 128
VMEM_LIMIT = 48 * 1024 * 1024

TM_PROJ = 512
TN_PROJ = 512
N_QK_BLK = 4
TQ = 512
TK = 512
TM_MIX = 512
TM_KV = 512
TN_KV = 1024
TM_FFN = 1024
TF_DENSE = 256
TF_MOE = 512
MOE_SPLIT = 4
V_ROWS = LANES + 8
N_SPLIT = 3
GATE_ROWS = 16


def _params(sem):
    return pltpu.CompilerParams(dimension_semantics=sem, vmem_limit_bytes=VMEM_LIMIT)


def _rms_scale(x, eps):
    return lax.rsqrt(jnp.mean(x * x, axis=-1, keepdims=True) + eps)


def _split3(v):
    hi = v.astype(BF16)
    r1 = v - hi.astype(F32)
    mid = r1.astype(BF16)
    lo = (r1 - mid.astype(F32)).astype(BF16)
    return hi, mid, lo


O_VD = 4 * D_DIFF_QK
O_QF = O_VD + D_DIFF_OUT
O_KF = O_QF + D_FOX
O_VF = O_KF + D_FOX
O_F = O_VF + D_FOX
Q_SCALE = ATTN_SCALE * LOG2E
QK_COLS = 2 * TN_PROJ + 2 * N_FOX_HEADS * LANES
_PROJ_BLOCKS = (
    (0, "rope", 0, Q_SCALE),
    (2 * D_DIFF_QK, "rope", TN_PROJ, 1.0),
    (O_QF, "fox_q", 2 * TN_PROJ, Q_SCALE),
    (O_KF, "fox_k", 2 * TN_PROJ + N_FOX_HEADS * LANES, 1.0),
    (O_VD, "vt", 0, 1.0),
    (O_VF, "vt", N_DIFF_HEADS, 1.0),
)


def _proj_kernel(x_ref, g_ref, w_ref, wf_ref, fb_ref, sel_ref, cos_ref, sa_ref, sb_ref,
                 qk_ref, vt_ref, c_ref, carry_ref, *, tiles_per_seq):
    i = pl.program_id(0)
    tm = x_ref.shape[0]
    x = x_ref[...]
    xnb = ((x * _rms_scale(x, EPS)) * g_ref[...]).astype(BF16)

    logit = lax.dot_general(wf_ref[...], xnb, (((1,), (1,)), ((), ())),
                            preferred_element_type=F32) + fb_ref[...]
    logf = jnp.minimum(logit, 0.0) - jnp.log1p(jnp.exp(-jnp.abs(logit)))

    @pl.when(i % tiles_per_seq == 0)
    def _():
        carry_ref[...] = jnp.zeros_like(carry_ref)

    r = lax.broadcasted_iota(jnp.int32, (tm, tm), 0)
    cidx = lax.broadcasted_iota(jnp.int32, (tm, tm), 1)
    tri = (r <= cidx).astype(BF16)
    hi, mid, lo = _split3(logf)
    cs = (jnp.dot(hi, tri, preferred_element_type=F32)
          + jnp.dot(mid, tri, preferred_element_type=F32)
          + jnp.dot(lo, tri, preferred_element_type=F32)) + carry_ref[:, 0:1]
    carry_ref[...] = jnp.broadcast_to(cs[:, tm - 1:tm], carry_ref.shape)
    c2 = cs * LOG2E
    c_ref[0] = c2
    hi, mid, lo = _split3(-c2)
    kx = (jnp.dot(sel_ref[0], hi, preferred_element_type=F32)
          + jnp.dot(sel_ref[1], mid, preferred_element_type=F32)
          + jnp.dot(sel_ref[2], lo, preferred_element_type=F32)).T
    lane = lax.broadcasted_iota(jnp.int32, (tm, LANES), 1)
    low_half = lane < HEAD_DIM
    extra = [jnp.logical_and(lane >= HEAD_DIM, lane < HEAD_DIM + N_SPLIT), lane < N_SPLIT]

    cos = cos_ref[...]
    sa = sa_ref[...]
    sb = sb_ref[...]
    ones_row = (lax.broadcasted_iota(jnp.int32, (V_ROWS - LANES, TK), 0) == 0).astype(vt_ref.dtype)
    for src, kind, dst, scale in _PROJ_BLOCKS:
        acc = jnp.dot(xnb, w_ref[:, src:src + TN_PROJ], preferred_element_type=F32)
        if scale != 1.0:
            acc = acc * scale
        if kind == "rope":
            for s in range(TN_PROJ // LANES):
                t = acc[:, s * LANES:(s + 1) * LANES]
                rot = (t * cos + pltpu.roll(t, ROPE_HALF, 1) * sa
                       + pltpu.roll(t, LANES - ROPE_HALF, 1) * sb)
                qk_ref[:, dst + s * LANES:dst + (s + 1) * LANES] = rot.astype(qk_ref.dtype)
        elif kind in ("fox_q", "fox_k"):
            for pr in range(TN_PROJ // LANES):
                pair = acc[:, pr * LANES:(pr + 1) * LANES]
                for par in range(2):
                    head = 2 * pr + par
                    if kind == "fox_q":
                        other = extra[par].astype(F32)
                    else:
                        first = HEAD_DIM if par == 0 else 0
                        moved = pltpu.roll(kx, (first - N_SPLIT * head) % LANES, 1)
                        other = jnp.where(extra[par], moved, 0.0)
                    keep = low_half if par == 0 else jnp.logical_not(low_half)
                    qk_ref[:, dst + head * LANES:dst + (head + 1) * LANES] = (
                        jnp.where(keep, pair, other).astype(qk_ref.dtype))
        else:
            at = acc.T
            for hd in range(TN_PROJ // LANES):
                vt_ref[0, dst + hd, 0, 0:LANES, :] = at[hd * LANES:(hd + 1) * LANES, :].astype(vt_ref.dtype)
                vt_ref[0, dst + hd, 0, LANES:V_ROWS, :] = ones_row


def _proj_call(h, g, w_main, w_f, f_bias, sel, cos_t, sa_t, sb_t, batch, seq):
    t, d = h.shape
    tm = TM_PROJ
    assert tm == TK
    tiles_per_seq = seq // tm
    n_col = w_main.shape[1]
    n_vh = (len(_PROJ_BLOCKS) - N_QK_BLK) * (TN_PROJ // LANES)
    kern = functools.partial(_proj_kernel, tiles_per_seq=tiles_per_seq)
    const2 = lambda i: (0, 0)
    return pl.pallas_call(
        kern,
        grid=(t // tm,),
        in_specs=[
            pl.BlockSpec((tm, d), lambda i: (i, 0)),
            pl.BlockSpec((1, d), const2),
            pl.BlockSpec((d, n_col), const2),
            pl.BlockSpec((GATE_ROWS, d), const2),
            pl.BlockSpec((GATE_ROWS, 1), const2),
            pl.BlockSpec((N_SPLIT, LANES, GATE_ROWS), lambda i: (0, 0, 0)),
            pl.BlockSpec((tm, LANES), lambda i: (i, 0)),
            pl.BlockSpec((tm, LANES), lambda i: (i, 0)),
            pl.BlockSpec((tm, LANES), lambda i: (i, 0)),
        ],
        out_specs=[
            pl.BlockSpec((tm, QK_COLS), lambda i: (i, 0)),
            pl.BlockSpec((1, n_vh, 1, V_ROWS, TK), lambda i: (i // tiles_per_seq, 0, i % tiles_per_seq, 0, 0)),
            pl.BlockSpec((1, GATE_ROWS, tm), lambda i: (i // tiles_per_seq, 0, i % tiles_per_seq)),
        ],
        out_shape=[
            jax.ShapeDtypeStruct((t, QK_COLS), BF16),
            jax.ShapeDtypeStruct((batch, n_vh, seq // TK, V_ROWS, TK), BF16),
            jax.ShapeDtypeStruct((batch, GATE_ROWS, seq), F32),
        ],
        scratch_shapes=[pltpu.VMEM((GATE_ROWS, LANES), F32)],
        compiler_params=_params(("arbitrary",)),
    )(h, g, w_main, w_f, f_bias, sel, cos_t, sa_t, sb_t)


def _head_mask(q, half):
    lane = lax.broadcasted_iota(jnp.int32, q.shape, 1)
    return jnp.where(lane // HEAD_DIM == half, q, jnp.zeros_like(q))


def _causal_mask():
    kpos = lax.broadcasted_iota(jnp.int32, (TK, TQ), 0)
    qpos = lax.broadcasted_iota(jnp.int32, (TK, TQ), 1)
    return kpos <= qpos


def _scores_t(k, q):
    return lax.dot_general(k, q, (((1,), (1,)), ((), ())), preferred_element_type=F32)


def _online_update(u, vt, shift, acc_ref, m_ref):
    mu = jnp.max(u, axis=0, keepdims=True) + shift
    m_old = m_ref[...]
    m_new = jnp.maximum(m_old, mu)
    p = jnp.exp2(u - (m_new - shift)).astype(BF16)
    pv = jnp.dot(vt, p, preferred_element_type=F32)
    acc_ref[...] = jnp.exp2(m_old - m_new) * acc_ref[...] + pv
    m_ref[...] = m_new


def _normalised(acc_ref):
    acc = acc_ref[...]
    return acc[:LANES] * (1.0 / acc[LANES:LANES + 1])


N_TILES = 4
UNROLL = 4


def _attn_scratch():
    return ([pltpu.VMEM((TK, TQ), F32)] * (3 * N_TILES) + [pltpu.VMEM((V_ROWS, TQ), F32)] * N_TILES
            + [pltpu.VMEM((1, TQ), F32)] * N_TILES)


def _attend(qi, qk_fn, qk_next_fn, vt_fn, shifts, scratch, finish):
    assert TQ == TK
    s_a, s_b, s_c = (scratch[n * N_TILES:(n + 1) * N_TILES] for n in range(3))
    acc = scratch[3 * N_TILES:4 * N_TILES]
    m = scratch[4 * N_TILES:]
    for t in range(N_TILES):
        m[t][...] = jnp.full(m[t].shape, -jnp.inf, F32)
        acc[t][...] = jnp.zeros(acc[t].shape, F32)

    def stage(j, cur, nxt, masked):
        for t in range(N_TILES):
            nxt[t][...] = qk_next_fn(t) if masked else qk_fn(j + 1, t)
            u = cur[t][...]
            if masked:
                u = jnp.where(_causal_mask(), u, -jnp.inf)
            _online_update(u, vt_fn(j, t), shifts[t], acc[t], m[t])

    @pl.when(qi == 0)
    def _():
        for t in range(N_TILES):
            s_a[t][...] = qk_fn(0, t)
        stage(0, s_a, s_c, True)
        finish(acc)

    @pl.when(qi > 0)
    def _():
        stage(0, s_c, s_a, False)
        bufs = (s_a, s_b)

        def body(i, carry):
            for n in range(UNROLL):
                stage(UNROLL * i + 1 + n, bufs[n % 2], bufs[(n + 1) % 2], False)
            return carry

        lax.fori_loop(0, (qi - 1) // UNROLL, body, 0)

        for rem in range(UNROLL):
            @pl.when((qi - 1) % UNROLL == rem)
            def _(rem=rem):
                for n in range(rem):
                    stage(qi - rem + n, bufs[n % 2], bufs[(n + 1) % 2], False)
                stage(qi, bufs[rem % 2], s_c, True)
                finish(acc)


def _fox_kernel(q_ref, qn_ref, k_ref, vt_ref, ct_ref, o_ref, *scratch):
    grp = pl.program_id(1)
    qi = pl.program_id(2)
    cqs = [ct_ref[0, pl.ds(N_TILES * grp + t, 1), :] for t in range(N_TILES)]

    def scores(j, t, qr):
        start = pl.multiple_of(j * TK, TK)
        return _scores_t(k_ref[0, pl.ds(start, TK), t * LANES:(t + 1) * LANES],
                         qr[0, :, t * LANES:(t + 1) * LANES])

    def vt_fn(j, t):
        return vt_ref[0, t // 2, j]

    def finish(acc):
        for pp in range(2):
            ot = jnp.concatenate([_normalised(acc[2 * pp])[:HEAD_DIM],
                                  _normalised(acc[2 * pp + 1])[HEAD_DIM:]], axis=0)
            o_ref[0, :, pp * LANES:(pp + 1) * LANES] = ot.T.astype(o_ref.dtype)

    _attend(qi, lambda j, t: scores(j, t, q_ref), lambda t: scores(0, t, qn_ref), vt_fn, cqs, scratch, finish)


def _fox_call(qk, vt, ct, batch, seq):
    nkb = seq // TK
    nq = seq // TQ
    w4 = N_TILES * LANES
    return pl.pallas_call(
        _fox_kernel,
        grid=(batch, N_FOX_HEADS // N_TILES, nq),
        in_specs=[
            pl.BlockSpec((1, TQ, w4), lambda b, g, qi: (b, qi, 2 + g)),
            pl.BlockSpec((1, TQ, w4), lambda b, g, qi: (b, jnp.minimum(qi + 1, nq - 1), 2 + g)),
            pl.BlockSpec((1, seq, w4), lambda b, g, qi: (b, 0, 4 + g)),
            pl.BlockSpec((1, 2, nkb, V_ROWS, TK), lambda b, g, qi: (b, N_DIFF_HEADS // 2 + g, 0, 0, 0)),
            pl.BlockSpec((1, GATE_ROWS, TQ), lambda b, g, qi: (b, 0, qi)),
        ],
        out_specs=pl.BlockSpec((1, TQ, 2 * LANES), lambda b, g, qi: (b, qi, g)),
        out_shape=jax.ShapeDtypeStruct((batch, seq, D_FOX), BF16),
        scratch_shapes=_attn_scratch(),
        compiler_params=_params(("arbitrary", "arbitrary", "arbitrary")),
    )(qk, qk, qk, vt, ct)


def _diff_kernel(lam_ref, q1_ref, q2_ref, q1n_ref, q2n_ref, k1_ref, k2_ref, vt_ref, g_ref, o_ref, *scratch,
                 out_scale):
    qi = pl.program_id(2)
    k_refs = [k1_ref, k2_ref]
    qms = [_head_mask(r[0], hh) for hh in range(2) for r in (q1_ref, q2_ref)]
    qms_next = [_head_mask(r[0], hh) for hh in range(2) for r in (q1n_ref, q2n_ref)]

    def scores(j, t, qm):
        start = pl.multiple_of(j * TK, TK)
        return _scores_t(k_refs[t % 2][0, pl.ds(start, TK), :], qm[t])

    def vt_fn(j, t):
        return vt_ref[0, t // 2, j]

    def finish(acc):
        for hh in range(2):
            ot = _normalised(acc[2 * hh]) - lam_ref[0] * _normalised(acc[2 * hh + 1])
            y = ot * lax.rsqrt(jnp.mean(ot * ot, axis=0, keepdims=True) + SUBLN_EPS)
            y = (y * g_ref[...]) * out_scale
            o_ref[0, :, hh * LANES:(hh + 1) * LANES] = y.T.astype(o_ref.dtype)

    _attend(qi, lambda j, t: scores(j, t, qms), lambda t: scores(0, t, qms_next), vt_fn,
            [0.0] * N_TILES, scratch, finish)


def _diff_call(lam, qk, vt, g_col, out_scale, batch, seq):
    nkb = seq // TK
    nq = seq // TQ
    n_pairs = N_DIFF_HEADS // 2
    kern = functools.partial(_diff_kernel, out_scale=out_scale)
    grid_spec = pltpu.PrefetchScalarGridSpec(
        num_scalar_prefetch=1,
        grid=(batch, n_pairs, nq),
        in_specs=[
            pl.BlockSpec((1, TQ, LANES), lambda b, p, qi, lam: (b, qi, p)),
            pl.BlockSpec((1, TQ, LANES), lambda b, p, qi, lam: (b, qi, 2 + p)),
            pl.BlockSpec((1, TQ, LANES), lambda b, p, qi, lam: (b, jnp.minimum(qi + 1, nq - 1), p)),
            pl.BlockSpec((1, TQ, LANES), lambda b, p, qi, lam: (b, jnp.minimum(qi + 1, nq - 1), 2 + p)),
            pl.BlockSpec((1, seq, LANES), lambda b, p, qi, lam: (b, 0, 4 + p)),
            pl.BlockSpec((1, seq, LANES), lambda b, p, qi, lam: (b, 0, 6 + p)),
            pl.BlockSpec((1, 2, nkb, V_ROWS, TK), lambda b, p, qi, lam: (b, p, 0, 0, 0)),
            pl.BlockSpec((DIFF_V_DIM, 1), lambda b, p, qi, lam: (0, 0)),
        ],
        out_specs=pl.BlockSpec((1, TQ, 2 * LANES), lambda b, p, qi, lam: (b, qi, p)),
        scratch_shapes=_attn_scratch(),
    )
    return pl.pallas_call(
        kern,
        grid_spec=grid_spec,
        out_shape=jax.ShapeDtypeStruct((batch, seq, D_DIFF_OUT), BF16),
        compiler_params=_params(("arbitrary", "arbitrary", "arbitrary")),
    )(lam, qk, qk, qk, qk, qk, qk, vt, g_col)


def _norm_matmul_kernel(x_ref, g_ref, w_ref, o_ref):
    x = x_ref[...]
    xn = ((x * _rms_scale(x, EPS)) * g_ref[...]).astype(BF16)
    o_ref[...] = jnp.dot(xn, w_ref[...], preferred_element_type=F32).astype(o_ref.dtype)


def _norm_matmul_call(x, g, w, tm, tn):
    t, d = x.shape
    n = w.shape[1]
    return pl.pallas_call(
        _norm_matmul_kernel,
        grid=(t // tm, n // tn),
        in_specs=[pl.BlockSpec((tm, d), lambda i, j: (i, 0)),
                  pl.BlockSpec((1, d), lambda i, j: (0, 0)),
                  pl.BlockSpec((d, tn), lambda i, j: (0, j))],
        out_specs=pl.BlockSpec((tm, tn), lambda i, j: (i, j)),
        out_shape=jax.ShapeDtypeStruct((t, n), BF16),
        compiler_params=_params(("arbitrary", "arbitrary")),
    )(x, g, w)


def _mix_mem_kernel(h_ref, od_ref, of_ref, wod_ref, wof_ref, g_ref, wq_ref, k_ref, v_ref, wo_ref, gf_ref,
                    o_ref, hn_ref, *, mem_scale):
    h1 = (h_ref[...]
          + jnp.dot(od_ref[...], wod_ref[...], preferred_element_type=F32)
          + jnp.dot(of_ref[...], wof_ref[...], preferred_element_type=F32))
    xn = ((h1 * _rms_scale(h1, EPS)) * g_ref[...]).astype(BF16)
    q = jnp.dot(xn, wq_ref[...], preferred_element_type=F32).astype(BF16)
    d = q.shape[1]
    hd = d // MEM_HEADS
    outs = []
    for hh in range(MEM_HEADS):
        qh = q[:, hh * hd:(hh + 1) * hd]
        kh = k_ref[:, hh * hd:(hh + 1) * hd]
        vh = v_ref[:, hh * hd:(hh + 1) * hd]
        s = lax.dot_general(qh, kh, (((1,), (1,)), ((), ())),
                            preferred_element_type=F32) * mem_scale
        m = jnp.max(s, axis=-1, keepdims=True)
        e = jnp.exp(s - m)
        p = e * (1.0 / jnp.sum(e, axis=-1, keepdims=True))
        outs.append(jnp.dot(p.astype(BF16), vh, preferred_element_type=F32).astype(BF16))
    o = jnp.concatenate(outs, axis=1)
    h2 = h1 + jnp.dot(o, wo_ref[...], preferred_element_type=F32)
    o_ref[...] = h2
    hn_ref[...] = ((h2 * _rms_scale(h2, EPS)) * gf_ref[...]).astype(hn_ref.dtype)


def _mix_mem_call(h, o_d, o_f, w_od, w_of, g, w_q, kv, w_o, g_ffn, seq, n_mem):
    t, d = h.shape
    tm = TM_MIX
    tiles_per_seq = seq // tm
    kern = functools.partial(_mix_mem_kernel, mem_scale=1.0 / math.sqrt(d // MEM_HEADS))
    const = lambda i: (0, 0)
    return pl.pallas_call(
        kern,
        grid=(t // tm,),
        in_specs=[
            pl.BlockSpec((tm, d), lambda i: (i, 0)),
            pl.BlockSpec((tm, D_DIFF_OUT), lambda i: (i, 0)),
            pl.BlockSpec((tm, D_FOX), lambda i: (i, 0)),
            pl.BlockSpec((D_DIFF_OUT, d), const),
            pl.BlockSpec((D_FOX, d), const),
            pl.BlockSpec((1, d), const),
            pl.BlockSpec((d, d), const),
            pl.BlockSpec((n_mem, d), lambda i: (i // tiles_per_seq, 0)),
            pl.BlockSpec((n_mem, d), lambda i: (i // tiles_per_seq, 1)),
            pl.BlockSpec((d, d), const),
            pl.BlockSpec((1, d), const),
        ],
        out_specs=[pl.BlockSpec((tm, d), lambda i: (i, 0)), pl.BlockSpec((tm, d), lambda i: (i, 0))],
        out_shape=[jax.ShapeDtypeStruct((t, d), F32), jax.ShapeDtypeStruct((t, d), BF16)],
        compiler_params=_params(("arbitrary",)),
    )(h, o_d, o_f, w_od, w_of, g, w_q, kv, kv, w_o, g_ffn)


def _swiglu_chunk(x, w1, w3, w2):
    a = jnp.dot(x, w1, preferred_element_type=F32)
    b = jnp.dot(x, w3, preferred_element_type=F32)
    mid = (a * (1.0 / (1.0 + jnp.exp(-a)))) * b
    return jnp.dot(mid.astype(BF16), w2, preferred_element_type=F32)


def _dense_ffn_kernel(h_ref, xn_ref, w1_ref, w3_ref, w2_ref, o_ref):
    f = pl.program_id(1)

    @pl.when(f == 0)
    def _():
        o_ref[...] = h_ref[...]

    o_ref[...] += _swiglu_chunk(xn_ref[...], w1_ref[...], w3_ref[...], w2_ref[...])


def _dense_ffn_call(h, hn, w1, w3, w2):
    t, d = h.shape
    tm, tf = TM_FFN, TF_DENSE
    dff = w1.shape[1]
    return pl.pallas_call(
        _dense_ffn_kernel,
        grid=(t // tm, dff // tf),
        in_specs=[pl.BlockSpec((tm, d), lambda i, f: (i, 0)),
                  pl.BlockSpec((tm, d), lambda i, f: (i, 0)),
                  pl.BlockSpec((d, tf), lambda i, f: (0, f)),
                  pl.BlockSpec((d, tf), lambda i, f: (0, f)),
                  pl.BlockSpec((tf, d), lambda i, f: (f, 0))],
        out_specs=pl.BlockSpec((tm, d), lambda i, f: (i, 0)),
        out_shape=jax.ShapeDtypeStruct((t, d), F32),
        compiler_params=_params(("arbitrary", "arbitrary")),
    )(h, hn, w1, w3, w2)


def _router_kernel(hn_ref, wr_ref, idx_ref, gate_ref):
    xn = hn_ref[...]
    logits = jnp.dot(xn, wr_ref[...], preferred_element_type=F32)
    lane = lax.broadcasted_iota(jnp.int32, logits.shape, 1)
    logits = jnp.where(lane < N_EXPERTS, logits, -jnp.inf)
    v1 = jnp.max(logits, axis=-1, keepdims=True)
    i1 = jnp.min(jnp.where(logits == v1, lane, LANES), axis=-1, keepdims=True)
    rest = jnp.where(lane == i1, -jnp.inf, logits)
    v2 = jnp.max(rest, axis=-1, keepdims=True)
    i2 = jnp.min(jnp.where(rest == v2, lane, LANES), axis=-1, keepdims=True)
    e = jnp.exp(v2 - v1)
    inv = 1.0 / (1.0 + e)
    idx_ref[...] = jnp.where(lane == 0, i1, jnp.where(lane == 1, i2, 0))
    gate_ref[...] = jnp.where(lane == 0, inv, jnp.where(lane == 1, e * inv, 0.0))


def _router_call(hn, w_r):
    t, d = hn.shape
    tm = TM_FFN
    return pl.pallas_call(
        _router_kernel,
        grid=(t // tm,),
        in_specs=[pl.BlockSpec((tm, d), lambda i: (i, 0)),
                  pl.BlockSpec((d, LANES), lambda i: (0, 0))],
        out_specs=[pl.BlockSpec((tm, LANES), lambda i: (i, 0)),
                   pl.BlockSpec((tm, LANES), lambda i: (i, 0))],
        out_shape=[jax.ShapeDtypeStruct((t, LANES), jnp.int32),
                   jax.ShapeDtypeStruct((t, LANES), F32)],
        compiler_params=_params(("arbitrary",)),
    )(hn, w_r)


def _expert_ffn_kernel(te_ref, nu_ref, x_ref, w1_ref, w3_ref, w2_ref, *rest, n_chunks, tile0):
    o_ref, acc_ref = rest[-2:]
    i = pl.program_id(0)
    f = pl.program_id(1)

    @pl.when(tile0 + i < nu_ref[0])
    def _():
        @pl.when(f == 0)
        def _():
            acc_ref[...] = jnp.zeros_like(acc_ref)

        acc_ref[...] += _swiglu_chunk(x_ref[...], w1_ref[0, 0].astype(BF16), w3_ref[0, 0].astype(BF16),
                                      w2_ref[0, 0].astype(BF16))

        @pl.when(f == n_chunks - 1)
        def _():
            o_ref[...] = acc_ref[...].astype(o_ref.dtype)

    @pl.when(jnp.logical_and(tile0 >= nu_ref[0], jnp.logical_and(i == 0, f == 0)))
    def _():
        o_ref[...] = jnp.zeros_like(o_ref)


def _expert_ffn_call(tile_expert, n_used, x_rows, w1, w3, w2, layer, tile0, p_rows, y_prev=None):
    rows, d = x_rows.shape
    tm, tf = TM_FFN, TF_MOE
    n_local = rows // tm
    dff = w1.shape[3]
    n_chunks = dff // tf
    kern = functools.partial(_expert_ffn_kernel, n_chunks=n_chunks, tile0=tile0)

    def local(i, nu):
        return jnp.minimum(i, jnp.clip(nu[0] - 1 - tile0, 0, n_local - 1))

    def chunk(i, f, nu):
        return jnp.where(tile0 + i < nu[0], f, n_chunks - 1)

    def expert(i, te):
        return te[tile0 + i]

    in_specs = [
        pl.BlockSpec((tm, d), lambda i, f, te, nu: (local(i, nu), 0)),
        pl.BlockSpec((1, 1, d, tf), lambda i, f, te, nu: (layer, expert(i, te), 0, chunk(i, f, nu))),
        pl.BlockSpec((1, 1, d, tf), lambda i, f, te, nu: (layer, expert(i, te), 0, chunk(i, f, nu))),
        pl.BlockSpec((1, 1, tf, d), lambda i, f, te, nu: (layer, expert(i, te), chunk(i, f, nu), 0)),
    ]
    args = [tile_expert, n_used, x_rows, w1, w3, w2]
    aliases = {}
    if y_prev is not None:
        in_specs.append(pl.BlockSpec(memory_space=pl.ANY))
        aliases = {len(args): 0}
        args.append(y_prev)
    grid_spec = pltpu.PrefetchScalarGridSpec(
        num_scalar_prefetch=2,
        grid=(n_local, n_chunks),
        in_specs=in_specs,
        out_specs=pl.BlockSpec((tm, d), lambda i, f, te, nu: (tile0 + local(i, nu), 0)),
        scratch_shapes=[pltpu.VMEM((tm, d), F32)],
    )
    return pl.pallas_call(
        kern,
        grid_spec=grid_spec,
        out_shape=jax.ShapeDtypeStruct((p_rows, d), BF16),
        input_output_aliases=aliases,
        compiler_params=_params(("arbitrary", "arbitrary")),
    )(*args)


def _route(idx, tm):
    t = idx.shape[0]
    n_assign = 2 * t
    p_rows = n_assign + N_EXPERTS * tm
    e_flat = idx.reshape(-1)
    onehot = (e_flat[None, :] == jnp.arange(N_EXPERTS, dtype=jnp.int32)[:, None]).astype(jnp.int32)
    running = jnp.cumsum(onehot, axis=1)
    counts = running[:, -1]
    padded = ((counts + tm - 1) // tm) * tm
    pend = jnp.cumsum(padded)
    pstart = pend - padded
    gstart = jnp.cumsum(counts) - counts
    dest = jnp.sum(onehot * (running - 1 + pstart[:, None]), axis=0)
    n_tiles = p_rows // tm
    n_used = (pend[-1] // tm).astype(jnp.int32)
    tile_start = jnp.arange(n_tiles, dtype=jnp.int32) * tm
    tile_expert = jnp.sum((tile_start[:, None] >= pend[None, :]).astype(jnp.int32), axis=1)
    last_expert = jnp.sum((((n_used - 1) * tm) >= pend).astype(jnp.int32))
    tile_expert = jnp.where(tile_start < pend[-1], tile_expert, last_expert).astype(jnp.int32)
    order = jnp.sort(e_flat * n_assign + jnp.arange(n_assign, dtype=jnp.int32)) % n_assign
    within = ((tile_start - pstart[tile_expert])[:, None]
              + jnp.arange(tm, dtype=jnp.int32)[None, :])
    src = jnp.clip(gstart[tile_expert][:, None] + within, 0, n_assign - 1)
    row_token = jnp.where(within < counts[tile_expert][:, None], order[src] // 2, 0)
    return row_token.reshape(p_rows), dest.reshape(t, 2), tile_expert, n_used.reshape(1)


def _moe_ffn(hn, w_r, w1, w3, w2, layer):
    idx_l, gate_l = _router_call(hn, w_r)
    row_token, slot, tile_expert, n_used = _route(idx_l[:, :2], TM_FFN)
    p_rows = row_token.shape[0]
    assert p_rows % (MOE_SPLIT * TM_FFN) == 0
    rows = p_rows // MOE_SPLIT
    y = None
    for c in range(MOE_SPLIT):
        x_rows = jnp.take(hn, row_token[c * rows:(c + 1) * rows], axis=0)
        y = _expert_ffn_call(tile_expert, n_used, x_rows, w1, w3, w2, layer, c * (rows // TM_FFN), p_rows, y)
    return jnp.take(y, slot[:, 0], axis=0), jnp.take(y, slot[:, 1], axis=0), gate_l


def _final_norm_kernel(x_ref, g_ref, o_ref):
    x = x_ref[...]
    o_ref[...] = (x * _rms_scale(x, EPS)) * g_ref[...]


def _combine_norm_kernel(h_ref, y0_ref, y1_ref, gate_ref, g_ref, o_ref):
    x = (h_ref[...] + gate_ref[:, 0:1] * y0_ref[...].astype(F32)
         + gate_ref[:, 1:2] * y1_ref[...].astype(F32))
    o_ref[...] = (x * _rms_scale(x, EPS)) * g_ref[...]


def _final_norm_call(h, g, moe_parts=None):
    t, d = h.shape
    tm = TM_FFN
    tile = pl.BlockSpec((tm, d), lambda i: (i, 0))
    gain = pl.BlockSpec((1, d), lambda i: (0, 0))
    if moe_parts is None:
        kern, in_specs, args = _final_norm_kernel, [tile, gain], (h, g)
    else:
        y0, y1, gates = moe_parts
        kern = _combine_norm_kernel
        in_specs = [tile, tile, tile, pl.BlockSpec((tm, LANES), lambda i: (i, 0)), gain]
        args = (h, y0, y1, gates, g)
    return pl.pallas_call(
        kern,
        grid=(t // tm,),
        in_specs=in_specs,
        out_specs=tile,
        out_shape=jax.ShapeDtypeStruct((t, d), F32),
        compiler_params=_params(("arbitrary",)),
    )(*args)


ROPE_ROWS = 8


def _rope_table_kernel(pos_ref, freq_ref, cos_ref, sa_ref, sb_ref):
    dim = lax.broadcasted_iota(jnp.int32, (LANES, LANES), 1) % HEAD_DIM
    first = dim < ROPE_HALF
    second = jnp.logical_and(dim >= ROPE_HALF, dim < ROPE_DIM)
    freq = freq_ref[...]
    for r in range(ROPE_ROWS):
        ang = (freq * pos_ref[r:r + 1, :].astype(F32)).T
        cos, sin = jnp.cos(ang), jnp.sin(ang)
        rows = slice(r * LANES, (r + 1) * LANES)
        cos_ref[rows, :] = jnp.where(jnp.logical_or(first, second), cos, 1.0)
        sa_ref[rows, :] = jnp.where(second, sin, 0.0)
        sb_ref[rows, :] = jnp.where(first, -sin, 0.0)


def _rope_tables(positions):
    inv_freq = ROPE_THETA ** (-jnp.arange(0, ROPE_DIM, 2, dtype=F32) / ROPE_DIM)
    freq = jnp.tile(inv_freq, LANES // ROPE_HALF).reshape(LANES, 1)
    t = positions.size
    tm = ROPE_ROWS * LANES
    table = jax.ShapeDtypeStruct((t, LANES), F32)
    tile = pl.BlockSpec((tm, LANES), lambda i: (i, 0))
    return pl.pallas_call(
        _rope_table_kernel,
        grid=(t // tm,),
        in_specs=[pl.BlockSpec((ROPE_ROWS, LANES), lambda i: (i, 0)),
                  pl.BlockSpec((LANES, 1), lambda i: (0, 0))],
        out_specs=[tile, tile, tile],
        out_shape=[table, table, table],
        compiler_params=_params(("arbitrary",)),
    )(positions.reshape(t // LANES, LANES), freq)


def _decay_selectors():
    dst = jnp.arange(LANES, dtype=jnp.int32)[:, None]
    src = jnp.arange(GATE_ROWS, dtype=jnp.int32)[None, :]
    sels = []
    for s in range(N_SPLIT):
        hit = jnp.logical_and(src < N_FOX_HEADS, dst == N_SPLIT * src + s)
        sels.append(hit.astype(BF16))
    return jnp.stack(sels)


def _split_w_in(w):
    w_f = jnp.pad(w[:, O_F:O_F + N_FOX_HEADS].T, ((0, GATE_ROWS - N_FOX_HEADS), (0, 0))).astype(BF16)
    return w.astype(BF16), w_f


def kernel(x, mem, positions, norm_mix_g, w_in, lam_q1, lam_k1, lam_q2, lam_k2, diff_subln_g, fox_bias, w_out, norm_mem_g, mem_norm_g, w_mq, w_mkv, w_mo, norm_ffn_g, w_ff1, w_ff3, w_ff2, w_router, w_e1, w_e3, w_e2, final_norm_g):
    batch, seq, d = x.shape
    n_mem = mem.shape[1]
    depth = w_in.shape[0]
    t = batch * seq
    cos_t, sa_t, sb_t = _rope_tables(positions)
    sel = _decay_selectors()
    h = x.reshape(t, d)
    mem2 = mem.reshape(batch * n_mem, d)
    row = lambda v: v.reshape(1, -1).astype(F32)

    moe_parts = None
    for l in range(depth):
        lam_init = 0.8 - 0.6 * math.exp(-0.3 * l)
        lam = (jnp.exp(jnp.sum(lam_q1[l] * lam_k1[l])) - jnp.exp(jnp.sum(lam_q2[l] * lam_k2[l]))
               + lam_init).reshape(1).astype(F32)
        w_main, w_f = _split_w_in(w_in[l])
        f_bias = jnp.pad(fox_bias[l], (0, GATE_ROWS - N_FOX_HEADS)).reshape(GATE_ROWS, 1)

        qk, vt, ct = _proj_call(h, row(norm_mix_g[l]), w_main, w_f, f_bias, sel, cos_t, sa_t, sb_t, batch, seq)
        qk3 = qk.reshape(batch, seq, qk.shape[1])
        o_d = _diff_call(lam, qk3, vt, diff_subln_g[l].reshape(DIFF_V_DIM, 1), 1.0 - lam_init, batch, seq)
        o_f = _fox_call(qk3, vt, ct, batch, seq)

        w_o = w_out[l].astype(BF16)
        kv = _norm_matmul_call(mem2, row(mem_norm_g[l]), w_mkv[l].astype(BF16), TM_KV, TN_KV)
        h, hn = _mix_mem_call(h, o_d.reshape(t, D_DIFF_OUT), o_f.reshape(t, D_FOX),
                              w_o[:D_DIFF_OUT], w_o[D_DIFF_OUT:], row(norm_mem_g[l]),
                              w_mq[l].astype(BF16), kv, w_mo[l].astype(BF16), row(norm_ffn_g[l]), seq, n_mem)

        i = l // 2
        if l % 2 == 0:
            h = _dense_ffn_call(h, hn, w_ff1[i].astype(BF16), w_ff3[i].astype(BF16), w_ff2[i].astype(BF16))
        else:
            w_r = jnp.pad(w_router[i], ((0, 0), (0, LANES - N_EXPERTS))).astype(BF16)
            y0, y1, gates = _moe_ffn(hn, w_r, w_e1, w_e3, w_e2, i)
            if l == depth - 1:
                moe_parts = (y0, y1, gates)
            else:
                h = h + gates[:, 0:1] * y0.astype(F32) + gates[:, 1:2] * y1.astype(F32)

    return _final_norm_call(h, row(final_norm_g), moe_parts).reshape(batch, seq, d)
```
